```python
import math
import jax, jax.numpy as jnp
from jax import lax
import numpy as np

D_MODEL = 2048
BATCH = 2
SEQ = 4096
DEPTH = 1

CHUNK = 64
GDN_QK_HEADS = 16
GDN_V_HEADS = 32
GDN_HEAD_DIM = 128
GDN_CONV = 4
GDN_QK_DIM = GDN_QK_HEADS * GDN_HEAD_DIM
GDN_V_DIM = GDN_V_HEADS * GDN_HEAD_DIM
GDN_QKV_DIM = 2 * GDN_QK_DIM + GDN_V_DIM
V_PER_QK = GDN_V_HEADS // GDN_QK_HEADS
SC_WIDTH = D_MODEL
SC_CONV = 3
N_EXPERTS = 32
TOP_K = 4
D_FF = D_MODEL
SWIGLU_LIMIT = 7.0
SWIGLU_ALPHA = 1.702
EXPERT_BLOCK = 128
LN_EPS = 1e-5
RMS_EPS = 1e-6
DN_ALPHA = (2 * DEPTH) ** 0.25
DN_BETA = (8 * DEPTH) ** -0.25
SPLITS = (GDN_QKV_DIM, GDN_V_DIM, GDN_V_HEADS, GDN_V_HEADS, SC_WIDTH, SC_WIDTH, SC_WIDTH, D_MODEL, D_MODEL)
IN_PROJ_DIM = sum(SPLITS)

kernel_name = "hybrid_gdn_shortconv_moe_deepnorm"


def layer_norm(x, g, b):
    xf = x.astype(jnp.float32)
    mu = jnp.mean(xf, axis=-1, keepdims=True)
    var = jnp.mean(jnp.square(xf - mu), axis=-1, keepdims=True)
    return ((xf - mu) * lax.rsqrt(var + LN_EPS) * g + b).astype(x.dtype)


def l2_normalize(x):
    xf = x.astype(jnp.float32)
    return xf * lax.rsqrt(jnp.sum(xf * xf, axis=-1, keepdims=True) + RMS_EPS)


def causal_depthwise_conv(x, w):
    k_width, channels = w.shape
    xp = jnp.pad(x, ((0, 0), (k_width - 1, 0), (0, 0)))
    return lax.conv_general_dilated(xp, w[:, None, :].astype(x.dtype), window_strides=(1,), padding='VALID',
                                    dimension_numbers=('NWC', 'WIO', 'NWC'), feature_group_count=channels)


def gated_delta_rule(q, k, v, beta, g):
    f32 = jnp.float32
    bsz, seq, heads, dk = q.shape
    dv = v.shape[-1]
    n_chunks = seq // CHUNK

    def to_chunks(t):
        return t.astype(f32).reshape(bsz, n_chunks, CHUNK, heads, -1).transpose(0, 3, 1, 2, 4)

    q = to_chunks(q) * (dk ** -0.5)
    k = to_chunks(k)
    v = to_chunks(v)
    beta = to_chunks(beta[..., None])
    g = jnp.cumsum(to_chunks(g[..., None])[..., 0], axis=-1)

    incl = jnp.tril(jnp.ones((CHUNK, CHUNK), dtype=bool))
    strict = jnp.tril(jnp.ones((CHUNK, CHUNK), dtype=bool), -1)
    decay = jnp.exp(jnp.where(incl, g[..., :, None] - g[..., None, :], -jnp.inf))

    k_beta = k * beta
    lower = jnp.where(strict, jnp.einsum('bhnid,bhnjd->bhnij', k_beta, k) * decay, 0.0)
    eye = jnp.eye(CHUNK, dtype=f32)
    rhs = jnp.concatenate([v * beta, k_beta * jnp.exp(g)[..., None]], axis=-1)
    sol = lax.linalg.triangular_solve(eye + lower, rhs, left_side=True, lower=True, unit_diagonal=True)
    u, w = sol[..., :dv], sol[..., dv:]

    attn = jnp.where(incl, jnp.einsum('bhnid,bhnjd->bhnij', q, k) * decay, 0.0)
    q_dec = q * jnp.exp(g)[..., None]
    k_dec = k * jnp.exp(g[..., -1:] - g)[..., None]
    g_last = jnp.exp(g[..., -1])

    def step(state, xs):
        q_c, k_c, u_c, w_c, a_c, gl_c = xs
        v_new = u_c - jnp.einsum('bhck,bhkv->bhcv', w_c, state)
        o = jnp.einsum('bhck,bhkv->bhcv', q_c, state) + jnp.einsum('bhij,bhjv->bhiv', a_c, v_new)
        state = state * gl_c[..., None, None] + jnp.einsum('bhck,bhcv->bhkv', k_c, v_new)
        return state, o

    lead = lambda t: jnp.moveaxis(t, 2, 0)
    state0 = jnp.zeros((bsz, heads, dk, dv), f32)
    _, o = lax.scan(step, state0, (lead(q_dec), lead(k_dec), lead(u), lead(w), lead(attn), lead(g_last)))
    return o.transpose(1, 0, 3, 2, 4).reshape(bsz, seq, heads, dv)


def gated_rms_norm(o, z, w):
    zf = z.astype(jnp.float32)
    var = jnp.mean(o * o, axis=-1, keepdims=True)
    return (o * lax.rsqrt(var + RMS_EPS) * w * jax.nn.silu(zf)).astype(z.dtype)


def hybrid_mixer(x, w_in, gdn_conv_w, gdn_a_log, gdn_dt_bias, gdn_norm_w, w_out_gdn, sc_conv_w, w_out_sc, w_out):
    bsz, seq, _ = x.shape
    proj = x @ w_in
    offs = np.cumsum(SPLITS)[:-1].tolist()
    qkv, z, b_raw, a_raw, sc_b, sc_c, sc_x, gate_a, gate_b = jnp.split(proj, offs, axis=-1)

    qkv = jax.nn.silu(causal_depthwise_conv(qkv, gdn_conv_w))
    q, k, v = jnp.split(qkv, [GDN_QK_DIM, 2 * GDN_QK_DIM], axis=-1)
    q = jnp.repeat(l2_normalize(q.reshape(bsz, seq, GDN_QK_HEADS, GDN_HEAD_DIM)), V_PER_QK, axis=2)
    k = jnp.repeat(l2_normalize(k.reshape(bsz, seq, GDN_QK_HEADS, GDN_HEAD_DIM)), V_PER_QK, axis=2)
    v = v.reshape(bsz, seq, GDN_V_HEADS, GDN_HEAD_DIM)
    beta = jax.nn.sigmoid(b_raw.astype(jnp.float32))
    g = -jnp.exp(gdn_a_log.astype(jnp.float32)) * jax.nn.softplus(a_raw.astype(jnp.float32) + gdn_dt_bias.astype(jnp.float32))
    o = gated_delta_rule(q, k, v, beta, g)
    o = gated_rms_norm(o, z.reshape(bsz, seq, GDN_V_HEADS, GDN_HEAD_DIM), gdn_norm_w)
    y_a = o.reshape(bsz, seq, GDN_V_DIM) @ w_out_gdn

    y_b = (sc_b * causal_depthwise_conv(sc_c * sc_x, sc_conv_w)) @ w_out_sc

    merged = jax.nn.sigmoid(gate_a) * y_a + jax.nn.sigmoid(gate_b) * y_b
    return merged @ w_out


def moe_ffn(h, w_router, b_router, w_gate_up, b_gate_up, w_down, b_down):
    n_tok = h.shape[0]
    n_assign = n_tok * TOP_K
    logits = (h @ w_router + b_router).astype(jnp.float32)
    top_val, top_idx = lax.top_k(logits, TOP_K)
    gate = jax.nn.softmax(top_val, axis=-1)

    flat_e = top_idx.reshape(-1)
    flat_w = gate.reshape(-1)
    order = jnp.argsort(flat_e)
    sorted_e = flat_e[order]
    token_of = order // TOP_K
    counts = jax.ops.segment_sum(jnp.ones_like(flat_e), flat_e, num_segments=N_EXPERTS)
    padded = (counts + EXPERT_BLOCK - 1) // EXPERT_BLOCK * EXPERT_BLOCK
    start = jnp.cumsum(counts) - counts
    pend = jnp.cumsum(padded)
    pstart = pend - padded
    dest = pstart[sorted_e] + (jnp.arange(n_assign) - start[sorted_e])

    n_blocks = -(-n_assign // EXPERT_BLOCK) + N_EXPERTS
    rows = jnp.zeros((n_blocks * EXPERT_BLOCK, h.shape[1]), h.dtype).at[dest].set(h[token_of])
    block_e = jnp.minimum(jnp.searchsorted(pend, jnp.arange(n_blocks) * EXPERT_BLOCK, side='right'), N_EXPERTS - 1)

    def expert_block(args):
        xb, e = args
        gu = xb @ w_gate_up[e] + b_gate_up[e]
        gt = jnp.minimum(gu[:, 0::2], SWIGLU_LIMIT)
        up = jnp.clip(gu[:, 1::2], -SWIGLU_LIMIT, SWIGLU_LIMIT)
        act = (up + 1.0) * (gt * jax.nn.sigmoid(SWIGLU_ALPHA * gt))
        return act @ w_down[e] + b_down[e]

    y = lax.map(expert_block, (rows.reshape(n_blocks, EXPERT_BLOCK, -1), block_e))
    y = y.reshape(n_blocks * EXPERT_BLOCK, -1)[dest]
    weighted = y * flat_w[order][:, None].astype(y.dtype)
    return jax.ops.segment_sum(weighted, token_of, num_segments=n_tok)


def setup_inputs(seed: int = 0) -> dict:
    key = jax.random.key(seed)
    ks = jax.random.split(key, 24)
    f32 = jnp.float32
    nrm = lambda k, shape, scale: jax.random.normal(k, shape, f32) * scale
    L = DEPTH
    dt = jnp.exp(jax.random.uniform(ks[6], (L, GDN_V_HEADS), f32, math.log(1e-3), math.log(1e-1)))
    return {
        "x": nrm(ks[0], (BATCH, SEQ, D_MODEL), 1.0),
        "ln_in_g": 1.0 + nrm(ks[1], (D_MODEL,), 0.02),
        "ln_in_b": nrm(ks[2], (D_MODEL,), 0.02),
        "w_in": nrm(ks[3], (L, D_MODEL, IN_PROJ_DIM), D_MODEL ** -0.5),
        "gdn_conv_w": nrm(ks[4], (L, GDN_CONV, GDN_QKV_DIM), GDN_CONV ** -0.5),
        "gdn_a_log": jnp.log(jax.random.uniform(ks[5], (L, GDN_V_HEADS), f32, 1.0, 16.0)),
        "gdn_dt_bias": dt + jnp.log(-jnp.expm1(-dt)),
        "gdn_norm_w": 1.0 + nrm(ks[7], (L, GDN_HEAD_DIM), 0.02),
        "w_out_gdn": nrm(ks[8], (L, GDN_V_DIM, D_MODEL), GDN_V_DIM ** -0.5 * DN_BETA),
        "sc_conv_w": nrm(ks[9], (L, SC_CONV, SC_WIDTH), SC_CONV ** -0.5),
        "w_out_sc": nrm(ks[10], (L, SC_WIDTH, D_MODEL), SC_WIDTH ** -0.5 * DN_BETA),
        "w_out": nrm(ks[11], (L, D_MODEL, D_MODEL), D_MODEL ** -0.5 * DN_BETA),
        "ln_mix_g": 1.0 + nrm(ks[12], (L, D_MODEL), 0.02),
        "ln_mix_b": nrm(ks[13], (L, D_MODEL), 0.02),
        "w_router": nrm(ks[14], (L, D_MODEL, N_EXPERTS), D_MODEL ** -0.5),
        "b_router": nrm(ks[15], (L, N_EXPERTS), 0.01),
        "w_gate_up": nrm(ks[16], (L, N_EXPERTS, D_MODEL, 2 * D_FF), D_MODEL ** -0.5),
        "b_gate_up": nrm(ks[17], (L, N_EXPERTS, 2 * D_FF), 0.01),
        "w_down": nrm(ks[18], (L, N_EXPERTS, D_FF, D_MODEL), D_FF ** -0.5 * DN_BETA),
        "b_down": nrm(ks[19], (L, N_EXPERTS, D_MODEL), 0.01),
        "ln_ffn_g": 1.0 + nrm(ks[20], (L, D_MODEL), 0.02),
        "ln_ffn_b": nrm(ks[21], (L, D_MODEL), 0.02),
    }


def reference(x, ln_in_g, ln_in_b, w_in, gdn_conv_w, gdn_a_log, gdn_dt_bias, gdn_norm_w, w_out_gdn,
              sc_conv_w, w_out_sc, w_out, ln_mix_g, ln_mix_b, w_router, b_router, w_gate_up, b_gate_up,
              w_down, b_down, ln_ffn_g, ln_ffn_b):
    bsz, seq, _ = x.shape
    h = layer_norm(x, ln_in_g, ln_in_b)
    for l in range(DEPTH):
        mix = hybrid_mixer(h, w_in[l], gdn_conv_w[l], gdn_a_log[l], gdn_dt_bias[l], gdn_norm_w[l],
                           w_out_gdn[l], sc_conv_w[l], w_out_sc[l], w_out[l])
        h = layer_norm(DN_ALPHA * h + mix, ln_mix_g[l], ln_mix_b[l])
        ffn = moe_ffn(h.reshape(bsz * seq, D_MODEL), w_router[l], b_router[l], w_gate_up[l], b_gate_up[l],
                      w_down[l], b_down[l]).reshape(bsz, seq, D_MODEL)
        h = layer_norm(DN_ALPHA * h + ffn, ln_ffn_g[l], ln_ffn_b[l])
    return h
```

```python
import functools

import jax
import jax.numpy as jnp
from jax import lax
from jax.experimental import pallas as pl
from jax.experimental.pallas import tpu as pltpu

F32 = jnp.float32
BF16 = jnp.bfloat16
I32 = jnp.int32

LANES = 128
SUBLANES = 8
VMEM_LIMIT = 56 << 20

TOP_K = 4
SWIGLU_LIMIT = 7.0
SWIGLU_ALPHA = 1.702
LN_EPS = 1e-5
RMS_EPS = 1e-6
GDN_CHUNK = 64
GDN_STEP = 256
MOE_TM = 256
MOE_XMAX = 2048
MOE_TF = 256


def _params(sem):
    return pltpu.CompilerParams(dimension_semantics=sem, vmem_limit_bytes=VMEM_LIMIT)


def _pick(n, candidates):
    for c in candidates:
        if n % c == 0:
            return c
    raise ValueError(f"no tile for {n} in {candidates}")


def _layer_norm(xf, g, b):
    mu = jnp.mean(xf, axis=-1, keepdims=True)
    xc = xf - mu
    var = jnp.mean(xc * xc, axis=-1, keepdims=True)
    return xc * lax.rsqrt(var + LN_EPS) * g + b


def _ln_in_kernel(x_ref, g_ref, b_ref, h_ref, hb_ref):
    h = _layer_norm(x_ref[...], g_ref[...], b_ref[...])
    h_ref[...] = h
    hb_ref[...] = h.astype(BF16)


def _ln_in(x2, g, b):
    t, d = x2.shape
    tm = _pick(t, (512, 256, 128, 64, 32, 16))
    return pl.pallas_call(
        _ln_in_kernel,
        grid=(t // tm,),
        in_specs=[pl.BlockSpec((tm, d), lambda i: (i, 0)),
                  pl.BlockSpec((1, d), lambda i: (0, 0)),
                  pl.BlockSpec((1, d), lambda i: (0, 0))],
        out_specs=[pl.BlockSpec((tm, d), lambda i: (i, 0)),
                   pl.BlockSpec((tm, d), lambda i: (i, 0))],
        out_shape=[jax.ShapeDtypeStruct((t, d), F32), jax.ShapeDtypeStruct((t, d), BF16)],
        compiler_params=_params(("arbitrary",)),
        name="ln_in",
    )(x2, g.reshape(1, d), b.reshape(1, d))


def _mm_kernel(*refs, n_extra, n_out, epilogue):
    x_ref, w_ref = refs[0], refs[1]
    extras = refs[2:2 + n_extra]
    o_refs = refs[2 + n_extra:2 + n_extra + n_out]
    wb_ref = refs[-1]

    @pl.when(pl.program_id(1) == 0)
    def _():
        wb_ref[...] = w_ref[...].astype(BF16)

    acc = jnp.dot(x_ref[...], wb_ref[...], preferred_element_type=F32)
    outs = epilogue(acc, *[e[...] for e in extras])
    for o_ref, o in zip(o_refs, outs):
        o_ref[...] = o.astype(o_ref.dtype)


def _matmul(x, w, *, col0, n_cols, tm, tn, out_dtypes, epilogue=None, extras=()):
    t, k = x.shape
    assert w.shape[0] == k and col0 % tn == 0 and n_cols % tn == 0 and t % tm == 0
    j0 = col0 // tn
    if epilogue is None:
        epilogue = lambda acc: (acc,)
    kern = functools.partial(_mm_kernel, n_extra=len(extras), n_out=len(out_dtypes), epilogue=epilogue)
    in_specs = [pl.BlockSpec((tm, k), lambda j, i: (i, 0)),
                pl.BlockSpec((k, tn), lambda j, i: (0, j + j0))]
    in_specs += [pl.BlockSpec(bs, im) for (_, bs, im) in extras]
    return pl.pallas_call(
        kern,
        grid=(n_cols // tn, t // tm),
        in_specs=in_specs,
        out_specs=[pl.BlockSpec((tm, tn), lambda j, i: (i, j)) for _ in out_dtypes],
        out_shape=[jax.ShapeDtypeStruct((t, n_cols), dt) for dt in out_dtypes],
        scratch_shapes=[pltpu.VMEM((k, tn), BF16)],
        compiler_params=_params(("arbitrary", "arbitrary")),
        name="matmul",
    )(x, w, *[a for (a, _, _) in extras])


def _gates_kernel(ba_ref, alog_ref, dt_ref, col_ref, row_ref, *, hv):
    x = ba_ref[...]
    lane = lax.broadcasted_iota(I32, x.shape, 1)
    beta = jax.nn.sigmoid(x)
    xs = x + dt_ref[...]
    softplus = jnp.maximum(xs, 0.0) + jnp.log(1.0 + jnp.exp(-jnp.abs(xs)))
    g = -jnp.exp(alog_ref[...]) * softplus
    g = jnp.where((lane >= hv) & (lane < 2 * hv), g, 0.0)
    row = lax.broadcasted_iota(I32, x.shape, 0) & (GDN_CHUNK - 1)
    sh = 1
    while sh < GDN_CHUNK:
        g = g + jnp.where(row >= sh, pltpu.roll(g, sh, 0), 0.0)
        sh *= 2
    out = jnp.where(lane < hv, beta, g)
    col_ref[...] = out
    row_ref[...] = out.T


def _gdn_gates(ba, a_log, dt_bias, hv):
    t = ba.shape[0]
    tm = _pick(t, (512, 256, 128))
    pad = lambda v: jnp.zeros((1, LANES), F32).at[0, hv:2 * hv].set(v.astype(F32))
    return pl.pallas_call(
        functools.partial(_gates_kernel, hv=hv),
        grid=(t // tm,),
        in_specs=[pl.BlockSpec((tm, LANES), lambda i: (i, 0)),
                  pl.BlockSpec((1, LANES), lambda i: (0, 0)),
                  pl.BlockSpec((1, LANES), lambda i: (0, 0))],
        out_specs=[pl.BlockSpec((tm, LANES), lambda i: (i, 0)),
                   pl.BlockSpec((LANES, tm), lambda i: (0, i))],
        out_shape=[jax.ShapeDtypeStruct((t, LANES), F32), jax.ShapeDtypeStruct((LANES, t), F32)],
        compiler_params=_params(("arbitrary",)),
        name="gdn_gates",
    )(ba, pad(a_log), pad(dt_bias))


def _shift_rows(x, tail, d):
    xs = pltpu.roll(x, d, 0)
    ts = pltpu.roll(tail, d, 0)
    row = lax.broadcasted_iota(I32, tail.shape, 0)
    head = jnp.where(row < d, ts, xs[:SUBLANES])
    return jnp.concatenate([head, xs[SUBLANES:]], axis=0)


def _causal_conv(x, w, tail_ref):
    kw = w.shape[0]
    tail = tail_ref[...]
    acc = x * w[kw - 1:kw]
    for d in range(1, kw):
        acc = acc + _shift_rows(x, tail, d) * w[kw - 1 - d:kw - d]
    tail_ref[...] = x[x.shape[0] - SUBLANES:]
    return acc


def _silu(x):
    return x * jax.nn.sigmoid(x)


def _unit_lower_inverse(low, ii, jj):
    c = low.shape[0]
    x = jnp.where(ii == jj, 1.0, 0.0) - jnp.where((ii >> 1) == (jj >> 1), low, 0.0)
    s = 1
    while (2 << s) <= c:
        m = ((ii >> (s + 1)) == (jj >> (s + 1))) & ((ii >> s) != (jj >> s))
        cb = jnp.where(m, low, 0.0).astype(BF16)
        xb = x.astype(BF16)
        t = jnp.dot(cb, xb, preferred_element_type=F32)
        x = x - jnp.dot(xb, t.astype(BF16), preferred_element_type=F32)
        s += 1
    return x


def _gdn_kernel(q_ref, k_ref, v_ref, z_ref, cwq_ref, cwk_ref, cwv_ref, gcol_ref, grow_ref, nw_ref,
                o_ref, s_ref, tq_ref, tk_ref, tv_ref, *, vper, dk):
    @pl.when(pl.program_id(2) == 0)
    def _():
        s_ref[...] = jnp.zeros_like(s_ref)
        tq_ref[...] = jnp.zeros_like(tq_ref)
        tk_ref[...] = jnp.zeros_like(tk_ref)
        tv_ref[...] = jnp.zeros_like(tv_ref)

    tb = q_ref.shape[0]
    c = GDN_CHUNK
    q = _silu(_causal_conv(q_ref[...].astype(F32), cwq_ref[...], tq_ref))
    k = _silu(_causal_conv(k_ref[...].astype(F32), cwk_ref[...], tk_ref))
    v = _silu(_causal_conv(v_ref[...].astype(F32), cwv_ref[...], tv_ref))
    q = q * (lax.rsqrt(jnp.sum(q * q, axis=-1, keepdims=True) + RMS_EPS) * (dk ** -0.5))
    k = k * lax.rsqrt(jnp.sum(k * k, axis=-1, keepdims=True) + RMS_EPS)

    gcols = gcol_ref[...]
    grows = grow_ref[...]
    nw = nw_ref[...]
    ii = lax.broadcasted_iota(I32, (c, c), 0)
    jj = lax.broadcasted_iota(I32, (c, c), 1)
    incl = ii >= jj
    strict = ii > jj
    states = [s_ref[h] for h in range(vper)]

    for ci in range(tb // c):
        r0 = ci * c
        kc = k[r0:r0 + c]
        qc = q[r0:r0 + c]
        kcb = kc.astype(BF16)
        a = lax.dot_general(jnp.concatenate([kcb, qc.astype(BF16)], axis=0), kcb,
                            (((1,), (1,)), ((), ())), preferred_element_type=F32)
        a_kk = a[:c]
        a_qk = a[c:]
        for h in range(vper):
            bcol = gcols[r0:r0 + c, h:h + 1]
            gcol = gcols[r0:r0 + c, vper + h:vper + h + 1]
            grow = grows[vper + h:vper + h + 1, r0:r0 + c]
            g_last = grow[:, c - 1:c]
            decay = jnp.where(incl, jnp.exp(jnp.minimum(gcol - grow, 0.0)), 0.0)
            low = jnp.where(strict, bcol * a_kk * decay, 0.0)
            attn = jnp.where(incl, a_qk * decay, 0.0)
            x = _unit_lower_inverse(low, ii, jj)
            eg = jnp.exp(gcol)
            vc = v[r0:r0 + c, h * dk:(h + 1) * dk]
            rhs = jnp.concatenate([vc * bcol, kc * (bcol * eg)], axis=1).astype(BF16)
            uw = jnp.dot(x.astype(BF16), rhs, preferred_element_type=F32)
            u = uw[:, :dk]
            w = uw[:, dk:]
            qd = qc * eg
            kd = kc * jnp.exp(g_last - gcol)
            st = states[h]
            ws_qs = jnp.dot(jnp.concatenate([w, qd], axis=0).astype(BF16), st.astype(BF16),
                            preferred_element_type=F32)
            v_new = (u - ws_qs[:c]).astype(BF16)
            o = ws_qs[c:] + jnp.dot(attn.astype(BF16), v_new, preferred_element_type=F32)
            states[h] = st * jnp.exp(g_last) + lax.dot_general(
                kd.astype(BF16), v_new, (((0,), (0,)), ((), ())), preferred_element_type=F32)
            zc = z_ref[r0:r0 + c, h * dk:(h + 1) * dk].astype(F32)
            var = jnp.mean(o * o, axis=-1, keepdims=True)
            o_ref[r0:r0 + c, h * dk:(h + 1) * dk] = (o * lax.rsqrt(var + RMS_EPS) * nw * _silu(zc)).astype(BF16)

    for h in range(vper):
        s_ref[h] = states[h]


def _gdn(proj_a, conv_w, gcol, grow, norm_w, *, bsz, seq, hq, hv, dk):
    t = bsz * seq
    vper = hv // hq
    tb = _pick(seq, (GDN_STEP, 128, 64))
    ns = seq // tb
    vw = vper * dk
    qk_dim = hq * dk
    v_blk0 = 2 * qk_dim // vw
    z_blk0 = (2 * qk_dim + hv * dk) // vw
    kw = conv_w.shape[0]
    rowblk = lambda b, h, s: b * ns + s
    kern = functools.partial(_gdn_kernel, vper=vper, dk=dk)
    return pl.pallas_call(
        kern,
        grid=(bsz, hq, ns),
        in_specs=[
            pl.BlockSpec((tb, dk), lambda b, h, s: (rowblk(b, h, s), h)),
            pl.BlockSpec((tb, dk), lambda b, h, s: (rowblk(b, h, s), hq + h)),
            pl.BlockSpec((tb, vw), lambda b, h, s: (rowblk(b, h, s), v_blk0 + h)),
            pl.BlockSpec((tb, vw), lambda b, h, s: (rowblk(b, h, s), z_blk0 + h)),
            pl.BlockSpec((kw, dk), lambda b, h, s: (0, h)),
            pl.BlockSpec((kw, dk), lambda b, h, s: (0, hq + h)),
            pl.BlockSpec((kw, vw), lambda b, h, s: (0, v_blk0 + h)),
            pl.BlockSpec((None, tb, 2 * vper), lambda b, h, s: (h, rowblk(b, h, s), 0)),
            pl.BlockSpec((None, 2 * vper, tb), lambda b, h, s: (h, 0, rowblk(b, h, s))),
            pl.BlockSpec((1, dk), lambda b, h, s: (0, 0)),
        ],
        out_specs=pl.BlockSpec((tb, vw), lambda b, h, s: (rowblk(b, h, s), h)),
        out_shape=jax.ShapeDtypeStruct((t, hv * dk), BF16),
        scratch_shapes=[pltpu.VMEM((vper, dk, dk), F32),
                        pltpu.VMEM((SUBLANES, dk), F32),
                        pltpu.VMEM((SUBLANES, dk), F32),
                        pltpu.VMEM((SUBLANES, vw), F32)],
        compiler_params=_params(("arbitrary", "arbitrary", "arbitrary")),
        name="gdn",
    )(proj_a, proj_a, proj_a, proj_a, conv_w, conv_w, conv_w, gcol, grow, norm_w.reshape(1, dk))


def _sc_kernel(b_ref, c_ref, x_ref, w_ref, u_ref, tail_ref):
    @pl.when(pl.program_id(1) == 0)
    def _():
        tail_ref[...] = jnp.zeros_like(tail_ref)

    p = c_ref[...].astype(F32) * x_ref[...].astype(F32)
    u_ref[...] = (b_ref[...].astype(F32) * _causal_conv(p, w_ref[...], tail_ref)).astype(BF16)


def _short_conv(proj_b, conv_w, *, bsz, seq, width):
    t = bsz * seq
    ts = _pick(seq, (512, 256, 128, 64))
    ns = seq // ts
    kw = conv_w.shape[0]
    return pl.pallas_call(
        _sc_kernel,
        grid=(bsz, ns),
        in_specs=[pl.BlockSpec((ts, width), lambda b, s: (b * ns + s, 0)),
                  pl.BlockSpec((ts, width), lambda b, s: (b * ns + s, 1)),
                  pl.BlockSpec((ts, width), lambda b, s: (b * ns + s, 2)),
                  pl.BlockSpec((kw, width), lambda b, s: (0, 0))],
        out_specs=pl.BlockSpec((ts, width), lambda b, s: (b * ns + s, 0)),
        out_shape=jax.ShapeDtypeStruct((t, width), BF16),
        scratch_shapes=[pltpu.VMEM((SUBLANES, width), F32)],
        compiler_params=_params(("arbitrary", "arbitrary")),
        name="short_conv",
    )(proj_b, proj_b, proj_b, conv_w)


def _router_kernel(h_ref, w_ref, b_ref, idx_ref, gate_ref, rank_ref, cnt_ref, run_ref):
    @pl.when(pl.program_id(0) == 0)
    def _():
        run_ref[...] = jnp.zeros_like(run_ref)

    tm = h_ref.shape[0]
    logits = jnp.dot(h_ref[...], w_ref[...], preferred_element_type=F32,
                     precision=lax.Precision.HIGHEST) + b_ref[...]
    lane = lax.broadcasted_iota(I32, logits.shape, 1)
    lane_f = lane.astype(F32)
    cur = logits
    idxs, vals = [], []
    for _ in range(TOP_K):
        m = jnp.max(cur, axis=-1, keepdims=True)
        ix = jnp.min(jnp.where(cur == m, lane_f, float(LANES)), axis=-1, keepdims=True).astype(I32)
        idxs.append(ix)
        vals.append(m)
        cur = jnp.where(lane == ix, -jnp.inf, cur)
    es = [jnp.exp(val - vals[0]) for val in vals]
    den = es[0]
    for e in es[1:]:
        den = den + e
    onehot = jnp.zeros(logits.shape, F32)
    for ix in idxs:
        onehot = onehot + jnp.where(lane == ix, 1.0, 0.0)
    ri = lax.broadcasted_iota(I32, (tm, tm), 0)
    ci = lax.broadcasted_iota(I32, (tm, tm), 1)
    tri = jnp.where(ri > ci, 1.0, 0.0).astype(BF16)
    before = jnp.dot(tri, onehot.astype(BF16), preferred_element_type=F32) + run_ref[...]
    idx_out = jnp.zeros(logits.shape, I32)
    rank_out = jnp.zeros(logits.shape, I32)
    gate_out = jnp.zeros(logits.shape, F32)
    for kk in range(TOP_K):
        rank = jnp.sum(jnp.where(lane == idxs[kk], before, 0.0), axis=-1, keepdims=True).astype(I32)
        idx_out = jnp.where(lane == kk, idxs[kk], idx_out)
        rank_out = jnp.where(lane == kk, rank, rank_out)
        gate_out = jnp.where(lane == kk, es[kk] / den, gate_out)
    idx_ref[...] = idx_out
    rank_ref[...] = rank_out
    gate_ref[...] = gate_out
    run_ref[...] = run_ref[...] + jnp.sum(onehot, axis=0, keepdims=True)
    cnt_ref[...] = run_ref[...].astype(I32)


def _router(h, w_router, b_router):
    t, d = h.shape
    e = w_router.shape[1]
    tm = _pick(t, (256, 128, 64, 32, 16, 8))
    wp = jnp.zeros((d, LANES), F32).at[:, :e].set(w_router)
    bp = jnp.full((1, LANES), -1e30, F32).at[0, :e].set(b_router)
    tile = pl.BlockSpec((tm, LANES), lambda i: (i, 0))
    return pl.pallas_call(
        _router_kernel,
        grid=(t // tm,),
        in_specs=[pl.BlockSpec((tm, d), lambda i: (i, 0)),
                  pl.BlockSpec((d, LANES), lambda i: (0, 0)),
                  pl.BlockSpec((1, LANES), lambda i: (0, 0))],
        out_specs=[tile, tile, tile, pl.BlockSpec((1, LANES), lambda i: (0, 0))],
        out_shape=[jax.ShapeDtypeStruct((t, LANES), I32), jax.ShapeDtypeStruct((t, LANES), F32),
                   jax.ShapeDtypeStruct((t, LANES), I32), jax.ShapeDtypeStruct((1, LANES), I32)],
        scratch_shapes=[pltpu.VMEM((1, LANES), F32)],
        compiler_params=_params(("arbitrary",)),
        name="router",
    )(h, wp, bp)


def _row_copy(src, s_row, dst, d_row, sem):
    return pltpu.make_async_copy(src.at[pl.ds(s_row, 1)], dst.at[pl.ds(d_row, 1)], sem)


def _dispatch_kernel(pos_ref, x_ref, rows_in_ref, rows_ref, sem):
    del rows_in_ref
    tm = x_ref.shape[0]
    base = pl.program_id(0) * (tm * TOP_K)

    def issue(t, carry):
        for kk in range(TOP_K):
            _row_copy(x_ref, t, rows_ref, pos_ref[base + t * TOP_K + kk], sem).start()
        return carry

    def drain(t, carry):
        for kk in range(TOP_K):
            _row_copy(x_ref, 0, rows_ref, 0, sem).wait()
        return carry

    lax.fori_loop(0, tm, issue, 0)
    lax.fori_loop(0, tm, drain, 0)


def _dispatch(h, pos, n_rows):
    t, d = h.shape
    tm = _pick(t, (256, 128, 64, 32, 16, 8))
    rows0 = jnp.zeros((n_rows, d), h.dtype)
    return pl.pallas_call(
        _dispatch_kernel,
        grid_spec=pltpu.PrefetchScalarGridSpec(
            num_scalar_prefetch=1,
            grid=(t // tm,),
            in_specs=[pl.BlockSpec((tm, d), lambda i, pos: (i, 0)),
                      pl.BlockSpec(memory_space=pl.ANY)],
            out_specs=pl.BlockSpec(memory_space=pl.ANY),
            scratch_shapes=[pltpu.SemaphoreType.DMA(())],
        ),
        out_shape=jax.ShapeDtypeStruct((n_rows, d), h.dtype),
        input_output_aliases={2: 0},
        compiler_params=_params(("arbitrary",)),
        name="dispatch",
    )(pos, h, rows0)


def _swiglu_interleaved(gu):
    lane = lax.broadcasted_iota(I32, (gu.shape[0], LANES), 1)
    even = (lane & 1) == 0
    parts = []
    for c in range(gu.shape[1] // LANES):
        g = gu[:, c * LANES:(c + 1) * LANES]
        gt = jnp.minimum(g, SWIGLU_LIMIT)
        glu = gt * jax.nn.sigmoid(SWIGLU_ALPHA * gt)
        up = jnp.clip(g, -SWIGLU_LIMIT, SWIGLU_LIMIT) + 1.0
        parts.append(jnp.where(even, pltpu.roll(up, LANES - 1, 1) * glu, 0.0))
    outs = [parts[2 * m] + pltpu.roll(parts[2 * m + 1], 1, 1) for m in range(len(parts) // 2)]
    return jnp.concatenate(outs, axis=1) if len(outs) > 1 else outs[0]


def _interleave_rows_bf16(w_ref, out_ref):
    half = LANES // 2
    for m in range(w_ref.shape[0] // LANES):
        lo = w_ref[m * LANES:m * LANES + half, :].astype(BF16).astype(F32)
        hi = w_ref[m * LANES + half:(m + 1) * LANES, :].astype(BF16).astype(F32)
        packed = (lax.bitcast_convert_type(lo, jnp.uint32) >> 16) | (
            lax.bitcast_convert_type(hi, jnp.uint32) & jnp.uint32(0xFFFF0000))
        out_ref[m * LANES:(m + 1) * LANES, :] = pltpu.bitcast(packed, BF16)


def _expert_kernel(ie_ref, ist_ref, inb_ref, rows_ref, wgu_ref, bgu_ref, wd_ref, bd_ref, y_ref,
                   xb_ref, acc_ref, stage_ref, wgub_ref, wdb_ref, sem, *, tm, n_j):
    del ie_ref
    wi = pl.program_id(0)
    j = pl.program_id(1)
    nblk = inb_ref[wi]
    start = ist_ref[wi]

    def block(b):
        return pl.ds(pl.multiple_of(b * tm, tm), tm)

    def hbm_block(b):
        return pl.ds(pl.multiple_of(start + b * tm, tm), tm)

    @pl.when((j == 0) & (nblk > 0))
    def _():
        def load(b, carry):
            cp = pltpu.make_async_copy(rows_ref.at[hbm_block(b)], stage_ref, sem)
            cp.start()
            cp.wait()
            xb_ref[block(b), :] = stage_ref[...].astype(BF16)
            return carry
        lax.fori_loop(0, nblk, load, 0)

    @pl.when(nblk > 0)
    def _():
        wgub_ref[...] = wgu_ref[...].astype(BF16)
        _interleave_rows_bf16(wd_ref, wdb_ref)
        bias = bgu_ref[...]

        def body(b, carry):
            gu = jnp.dot(xb_ref[block(b), :], wgub_ref[...], preferred_element_type=F32) + bias
            act = _swiglu_interleaved(gu).astype(BF16)
            contrib = jnp.dot(act, wdb_ref[...], preferred_element_type=F32)

            @pl.when(j == 0)
            def _():
                acc_ref[block(b), :] = contrib + bd_ref[...]

            @pl.when(j > 0)
            def _():
                acc_ref[block(b), :] += contrib
            return carry
        lax.fori_loop(0, nblk, body, 0)

    @pl.when((j == n_j - 1) & (nblk > 0))
    def _():
        def store(b, carry):
            cp = pltpu.make_async_copy(acc_ref.at[block(b)], y_ref.at[hbm_block(b)], sem)
            cp.start()
            cp.wait()
            return carry
        lax.fori_loop(0, nblk, store, 0)

    n_items = pl.num_programs(0)

    @pl.when((wi == n_items - 1) & (j == n_j - 1))
    def _():
        acc_ref[0:tm, :] = jnp.zeros((tm, acc_ref.shape[1]), F32)

        def fill(b, carry):
            cp = pltpu.make_async_copy(acc_ref.at[0:tm], y_ref.at[pl.ds(pl.multiple_of(b * tm, tm), tm)], sem)
            cp.start()
            cp.wait()
            return carry
        lax.fori_loop(ist_ref[n_items] // tm, y_ref.shape[0] // tm, fill, 0)


def _experts(rows, item_e, item_start, item_nblk, w_gate_up, b_gate_up, w_down, b_down, *, tm, xmax):
    n_rows, d = rows.shape
    n_exp, _, f2 = w_gate_up.shape
    f = f2 // 2
    tf = _pick(f, (MOE_TF, LANES))
    n_j = f // tf
    n_items = item_e.shape[0]

    def jeff(j, inb, wi):
        return jnp.where(inb[wi] > 0, j, n_j - 1)

    kern = functools.partial(_expert_kernel, tm=tm, n_j=n_j)
    return pl.pallas_call(
        kern,
        grid_spec=pltpu.PrefetchScalarGridSpec(
            num_scalar_prefetch=3,
            grid=(n_items, n_j),
            in_specs=[
                pl.BlockSpec(memory_space=pl.ANY),
                pl.BlockSpec((None, d, 2 * tf), lambda wi, j, ie, ist, inb: (ie[wi], 0, jeff(j, inb, wi))),
                pl.BlockSpec((None, 1, 2 * tf), lambda wi, j, ie, ist, inb: (ie[wi], 0, jeff(j, inb, wi))),
                pl.BlockSpec((None, tf, d), lambda wi, j, ie, ist, inb: (ie[wi], jeff(j, inb, wi), 0)),
                pl.BlockSpec((None, 1, d), lambda wi, j, ie, ist, inb: (ie[wi], 0, 0)),
            ],
            out_specs=pl.BlockSpec(memory_space=pl.ANY),
            scratch_shapes=[pltpu.VMEM((xmax, d), BF16),
                            pltpu.VMEM((xmax, d), F32),
                            pltpu.VMEM((tm, d), rows.dtype),
                            pltpu.VMEM((d, 2 * tf), BF16),
                            pltpu.VMEM((tf, d), BF16),
                            pltpu.SemaphoreType.DMA(())],
        ),
        out_shape=jax.ShapeDtypeStruct((n_rows, d), F32),
        compiler_params=_params(("arbitrary", "arbitrary")),
        name="experts",
    )(item_e, item_start, item_nblk, rows, w_gate_up, b_gate_up.reshape(n_exp, 1, f2),
      w_down, b_down.reshape(n_exp, 1, d))


def _combine_kernel(pos_ref, y_ref, gate_ref, h_ref, g_ref, b_ref, o_ref, ybuf_ref, sem, *, alpha):
    tm = h_ref.shape[0]
    base = pl.program_id(0) * (tm * TOP_K)

    def issue(t, carry):
        for kk in range(TOP_K):
            _row_copy(y_ref, pos_ref[base + t * TOP_K + kk], ybuf_ref.at[kk], t, sem).start()
        return carry

    def drain(t, carry):
        for kk in range(TOP_K):
            _row_copy(y_ref, 0, ybuf_ref.at[kk], 0, sem).wait()
        return carry

    lax.fori_loop(0, tm, issue, 0)
    lax.fori_loop(0, tm, drain, 0)
    gate = gate_ref[...]
    ffn = gate[:, 0:1] * ybuf_ref[0]
    for kk in range(1, TOP_K):
        ffn = ffn + gate[:, kk:kk + 1] * ybuf_ref[kk]
    o_ref[...] = _layer_norm(alpha * h_ref[...] + ffn, g_ref[...], b_ref[...])


def _combine(y, pos, gate, h, ln_g, ln_b, alpha):
    t, d = h.shape
    tm = _pick(t, (128, 64, 32, 16, 8))
    return pl.pallas_call(
        functools.partial(_combine_kernel, alpha=alpha),
        grid_spec=pltpu.PrefetchScalarGridSpec(
            num_scalar_prefetch=1,
            grid=(t // tm,),
            in_specs=[pl.BlockSpec(memory_space=pl.ANY),
                      pl.BlockSpec((tm, LANES), lambda i, pos: (i, 0)),
                      pl.BlockSpec((tm, d), lambda i, pos: (i, 0)),
                      pl.BlockSpec((1, d), lambda i, pos: (0, 0)),
                      pl.BlockSpec((1, d), lambda i, pos: (0, 0))],
            out_specs=pl.BlockSpec((tm, d), lambda i, pos: (i, 0)),
            scratch_shapes=[pltpu.VMEM((TOP_K, tm, d), F32), pltpu.SemaphoreType.DMA(())],
        ),
        out_shape=jax.ShapeDtypeStruct((t, d), F32),
        compiler_params=_params(("arbitrary",)),
        name="combine",
    )(pos, y, gate, h, ln_g.reshape(1, d), ln_b.reshape(1, d))


def _moe_plan(idx, rank, counts, *, tm, xmax, n_items):
    n_exp = counts.shape[0]
    padded = (counts + tm - 1) // tm * tm
    pstart = jnp.cumsum(padded) - padded
    pos = (pstart[idx] + rank).reshape(-1).astype(I32)
    per_e = (padded + xmax - 1) // xmax
    cum = jnp.cumsum(per_e)
    total = cum[-1]
    wi = jnp.arange(n_items, dtype=I32)
    valid = wi < total
    e_w = jnp.minimum(jnp.searchsorted(cum, jnp.minimum(wi, total - 1), side="right"), n_exp - 1).astype(I32)
    local = jnp.minimum(wi, total - 1) - (cum - per_e)[e_w]
    start = (pstart[e_w] + local * xmax).astype(I32)
    nblk = jnp.where(valid, jnp.clip(padded[e_w] - local * xmax, 0, xmax) // tm, 0).astype(I32)
    start = jnp.concatenate([start, jnp.sum(padded, keepdims=True).astype(I32)])
    return pos, e_w, start, nblk


def _layer(h0, h0b, p, *, bsz, seq, alpha):
    t, d = h0.shape
    w_in = p["w_in"]
    hv = p["gdn_a_log"].shape[0]
    dk = p["gdn_norm_w"].shape[0]
    qkv_dim = p["gdn_conv_w"].shape[1]
    v_dim = hv * dk
    hq = (qkv_dim - v_dim) // (2 * dk)
    scw = p["sc_conv_w"].shape[1]
    assert 2 * hv <= LANES and hv % hq == 0
    n_a = qkv_dim + v_dim
    off_ba = n_a
    off_b = n_a + 2 * hv
    n_b = 3 * scw + 2 * d
    assert w_in.shape[1] == off_b + n_b

    tm = _pick(t, (1024, 512, 256, 128))
    tn_a = _pick(n_a, (1024, 512, 256, 128))
    (proj_a,) = _matmul(h0b, w_in, col0=0, n_cols=n_a, tm=tm, tn=tn_a, out_dtypes=(BF16,))
    tn_b = _pick(n_b, (1024, 512, 256, 128))
    (proj_b,) = _matmul(h0b, w_in[:, off_b:], col0=0, n_cols=n_b, tm=tm, tn=tn_b, out_dtypes=(BF16,))
    w_ba = jnp.zeros((d, LANES), F32).at[:, :2 * hv].set(w_in[:, off_ba:off_b])
    (ba,) = _matmul(h0b, w_ba, col0=0, n_cols=LANES, tm=tm, tn=LANES, out_dtypes=(F32,))

    gcol, grow = _gdn_gates(ba, p["gdn_a_log"], p["gdn_dt_bias"], hv)
    vper = hv // hq
    gcol = gcol[:, :2 * hv].reshape(t, 2, hq, vper).transpose(2, 0, 1, 3).reshape(hq, t, 2 * vper)
    grow = grow[:2 * hv].reshape(2, hq, vper, t).transpose(1, 0, 2, 3).reshape(hq, 2 * vper, t)
    o_n = _gdn(proj_a, p["gdn_conv_w"], gcol, grow, p["gdn_norm_w"], bsz=bsz, seq=seq, hq=hq, hv=hv, dk=dk)

    u = _short_conv(proj_b, p["sc_conv_w"], bsz=bsz, seq=seq, width=scw)

    tn = _pick(d, (512, 256, 128))
    ga0 = 3 * scw // tn
    gb0 = (3 * scw + d) // tn
    (part_a,) = _matmul(
        o_n, p["w_out_gdn"], col0=0, n_cols=d, tm=tm, tn=tn, out_dtypes=(F32,),
        epilogue=lambda acc, ga: (jax.nn.sigmoid(ga.astype(F32)) * acc,),
        extras=[(proj_b, (tm, tn), lambda j, i: (i, ga0 + j))])
    (merged,) = _matmul(
        u, p["w_out_sc"], col0=0, n_cols=d, tm=tm, tn=tn, out_dtypes=(BF16,),
        epilogue=lambda acc, gb, pa: (pa + jax.nn.sigmoid(gb.astype(F32)) * acc,),
        extras=[(proj_b, (tm, tn), lambda j, i: (i, gb0 + j)),
                (part_a, (tm, tn), lambda j, i: (i, j))])
    tm3 = _pick(t, (256, 128))
    (h1,) = _matmul(
        merged, p["w_out"].astype(BF16), col0=0, n_cols=d, tm=tm3, tn=d, out_dtypes=(F32,),
        epilogue=lambda acc, hh, g, b: (_layer_norm(alpha * hh + acc, g, b),),
        extras=[(h0, (tm3, d), lambda j, i: (i, 0)),
                (p["ln_mix_g"].reshape(1, d), (1, d), lambda j, i: (0, 0)),
                (p["ln_mix_b"].reshape(1, d), (1, d), lambda j, i: (0, 0))])

    n_exp = p["w_router"].shape[1]
    idx, gate, rank, cnt = _router(h1, p["w_router"], p["b_router"])
    mtm = MOE_TM
    xmax = MOE_XMAX
    n_rows = t * TOP_K + n_exp * mtm
    n_items = n_exp + n_rows // xmax
    pos, item_e, item_start, item_nblk = _moe_plan(
        idx[:, :TOP_K], rank[:, :TOP_K], cnt[0, :n_exp], tm=mtm, xmax=xmax, n_items=n_items)
    rows = _dispatch(h1, pos, n_rows)
    y = _experts(rows, item_e, item_start, item_nblk, p["w_gate_up"], p["b_gate_up"], p["w_down"], p["b_down"],
                 tm=mtm, xmax=xmax)
    h2 = _combine(y, pos, gate, h1, p["ln_ffn_g"], p["ln_ffn_b"], alpha)
    return h2


_LAYER_PARAMS = ("w_in", "gdn_conv_w", "gdn_a_log", "gdn_dt_bias", "gdn_norm_w", "w_out_gdn", "sc_conv_w",
                 "w_out_sc", "w_out", "ln_mix_g", "ln_mix_b", "w_router", "b_router", "w_gate_up", "b_gate_up",
                 "w_down", "b_down", "ln_ffn_g", "ln_ffn_b")


def kernel(x, ln_in_g, ln_in_b, w_in, gdn_conv_w, gdn_a_log, gdn_dt_bias, gdn_norm_w, w_out_gdn, sc_conv_w,
           w_out_sc, w_out, ln_mix_g, ln_mix_b, w_router, b_router, w_gate_up, b_gate_up, w_down, b_down,
           ln_ffn_g, ln_ffn_b):
    stacked = dict(zip(_LAYER_PARAMS, (w_in, gdn_conv_w, gdn_a_log, gdn_dt_bias, gdn_norm_w, w_out_gdn,
                                       sc_conv_w, w_out_sc, w_out, ln_mix_g, ln_mix_b, w_router, b_router,
                                       w_gate_up, b_gate_up, w_down, b_down, ln_ffn_g, ln_ffn_b)))
    bsz, seq, d = x.shape
    depth = w_in.shape[0]
    alpha = (2 * depth) ** 0.25
    h, hb = _ln_in(x.reshape(bsz * seq, d), ln_in_g, ln_in_b)
    for l in range(depth):
        p = {name: arr[l] for name, arr in stacked.items()}
        h = _layer(h, hb, p, bsz=bsz, seq=seq, alpha=alpha)
        if l + 1 < depth:
            hb = h.astype(BF16)
    return h.reshape(bsz, seq, d)
```

```python
import functools

import jax
import jax.numpy as jnp
from jax import lax
from jax.experimental import pallas as pl
from jax.experimental.pallas import tpu as pltpu

F32 = jnp.float32
BF16 = jnp.bfloat16
I32 = jnp.int32
U32 = jnp.uint32

LANES = 128
SUBLANES = 8
VMEM_LIMIT = 56 << 20

TOP_K = 4
SWIGLU_LIMIT = 7.0
SWIGLU_ALPHA = 1.702
LN_EPS = 1e-5
RMS_EPS = 1e-6
GDN_CHUNK = 64
GDN_STEP = 256
GDN_QK_GROUP = 4
GDN_INV_GROUP = 16
MOE_TM = 256
MOE_XMAX = 1024
MOE_TF = 512


def _params(sem):
    return pltpu.CompilerParams(dimension_semantics=sem, vmem_limit_bytes=VMEM_LIMIT)


def _pick(n, candidates):
    for c in candidates:
        if n % c == 0:
            return c
    raise ValueError(f"no tile for {n} in {candidates}")


def _layer_norm(xf, g, b):
    mu = jnp.mean(xf, axis=-1, keepdims=True)
    xc = xf - mu
    var = jnp.mean(xc * xc, axis=-1, keepdims=True)
    return xc * lax.rsqrt(var + LN_EPS) * g + b


def _pack_halves_bf16(x):
    half = x.shape[1] // 2
    lo = lax.bitcast_convert_type(x[:, :half].astype(BF16).astype(F32), U32) >> 16
    hi = lax.bitcast_convert_type(x[:, half:].astype(BF16).astype(F32), U32) & jnp.uint32(0xFFFF0000)
    return lo | hi


def _unpack_halves_bf16(p):
    lo = lax.bitcast_convert_type(p << 16, F32).astype(BF16)
    hi = lax.bitcast_convert_type(p & jnp.uint32(0xFFFF0000), F32).astype(BF16)
    return lo, hi


def _ln_in_kernel(x_ref, g_ref, b_ref, h_ref, hb_ref):
    h = _layer_norm(x_ref[...], g_ref[...], b_ref[...])
    h_ref[...] = h
    hb_ref[...] = h.astype(BF16)


def _ln_in(x2, g, b):
    t, d = x2.shape
    tm = _pick(t, (512, 256, 128, 64, 32, 16))
    return pl.pallas_call(
        _ln_in_kernel,
        grid=(t // tm,),
        in_specs=[pl.BlockSpec((tm, d), lambda i: (i, 0)),
                  pl.BlockSpec((1, d), lambda i: (0, 0)),
                  pl.BlockSpec((1, d), lambda i: (0, 0))],
        out_specs=[pl.BlockSpec((tm, d), lambda i: (i, 0)),
                   pl.BlockSpec((tm, d), lambda i: (i, 0))],
        out_shape=[jax.ShapeDtypeStruct((t, d), F32), jax.ShapeDtypeStruct((t, d), BF16)],
        compiler_params=_params(("arbitrary",)),
        name="ln_in",
    )(x2, g.reshape(1, d), b.reshape(1, d))


def _mm_kernel(*refs, n_extra, n_out, epilogue):
    x_ref, w_ref = refs[0], refs[1]
    extras = refs[2:2 + n_extra]
    o_refs = refs[2 + n_extra:2 + n_extra + n_out]
    wb_ref = refs[-1]

    @pl.when(pl.program_id(1) == 0)
    def _():
        wb_ref[...] = w_ref[...].astype(BF16)

    acc = jnp.dot(x_ref[...], wb_ref[...], preferred_element_type=F32)
    outs = epilogue(acc, *[e[...] for e in extras])
    for o_ref, o in zip(o_refs, outs):
        o_ref[...] = o.astype(o_ref.dtype)


def _matmul(x, w, *, col0, n_cols, tm, tn, out_dtypes, epilogue=None, extras=(), out_div=None):
    t, k = x.shape
    assert w.shape[0] == k and col0 % tn == 0 and n_cols % tn == 0 and t % tm == 0
    j0 = col0 // tn
    if epilogue is None:
        epilogue = lambda acc: (acc,)
    if out_div is None:
        out_div = (1,) * len(out_dtypes)
    kern = functools.partial(_mm_kernel, n_extra=len(extras), n_out=len(out_dtypes), epilogue=epilogue)
    in_specs = [pl.BlockSpec((tm, k), lambda j, i: (i, 0)),
                pl.BlockSpec((k, tn), lambda j, i: (0, j + j0))]
    in_specs += [pl.BlockSpec(bs, im) for (_, bs, im) in extras]
    return pl.pallas_call(
        kern,
        grid=(n_cols // tn, t // tm),
        in_specs=in_specs,
        out_specs=[pl.BlockSpec((tm, tn // dv), lambda j, i: (i, j)) for dv in out_div],
        out_shape=[jax.ShapeDtypeStruct((t, n_cols // dv), dt) for dt, dv in zip(out_dtypes, out_div)],
        scratch_shapes=[pltpu.VMEM((k, tn), BF16)],
        compiler_params=_params(("arbitrary", "arbitrary")),
        name="matmul",
    )(x, w, *[a for (a, _, _) in extras])


def _gates_kernel(ba_ref, alog_ref, dt_ref, col_ref, row_ref, *, hv):
    x = ba_ref[...]
    lane = lax.broadcasted_iota(I32, x.shape, 1)
    beta = jax.nn.sigmoid(x)
    xs = x + dt_ref[...]
    softplus = jnp.maximum(xs, 0.0) + jnp.log(1.0 + jnp.exp(-jnp.abs(xs)))
    g = -jnp.exp(alog_ref[...]) * softplus
    g = jnp.where((lane >= hv) & (lane < 2 * hv), g, 0.0)
    n = x.shape[0]
    row = lax.broadcasted_iota(I32, x.shape, 0) & (GDN_CHUNK - 1)
    cum = g
    rev = g
    sh = 1
    while sh < GDN_CHUNK:
        cum = cum + jnp.where(row >= sh, pltpu.roll(cum, sh, 0), 0.0)
        rev = rev + jnp.where(row < GDN_CHUNK - sh, pltpu.roll(rev, n - sh, 0), 0.0)
        sh *= 2
    e_cum = pltpu.roll(jnp.exp(cum), hv, 1)
    e_rest = pltpu.roll(jnp.exp(rev - g), 2 * hv, 1)
    out = jnp.where(lane < hv, beta, jnp.where(lane < 2 * hv, cum, jnp.where(lane < 3 * hv, e_cum, e_rest)))
    col_ref[...] = out
    row_ref[...] = out.T


def _gdn_gates(ba, a_log, dt_bias, hv):
    assert 4 * hv <= LANES
    t = ba.shape[0]
    tm = _pick(t, (512, 256, 128))
    pad = lambda v: jnp.zeros((1, LANES), F32).at[0, hv:2 * hv].set(v.astype(F32))
    return pl.pallas_call(
        functools.partial(_gates_kernel, hv=hv),
        grid=(t // tm,),
        in_specs=[pl.BlockSpec((tm, LANES), lambda i: (i, 0)),
                  pl.BlockSpec((1, LANES), lambda i: (0, 0)),
                  pl.BlockSpec((1, LANES), lambda i: (0, 0))],
        out_specs=[pl.BlockSpec((tm, LANES), lambda i: (i, 0)),
                   pl.BlockSpec((LANES, tm), lambda i: (0, i))],
        out_shape=[jax.ShapeDtypeStruct((t, LANES), F32), jax.ShapeDtypeStruct((LANES, t), F32)],
        compiler_params=_params(("arbitrary",)),
        name="gdn_gates",
    )(ba, pad(a_log), pad(dt_bias))


def _shift_rows(x, tail, d):
    xs = pltpu.roll(x, d, 0)
    ts = pltpu.roll(tail, d, 0)
    row = lax.broadcasted_iota(I32, tail.shape, 0)
    head = jnp.where(row < d, ts, xs[:SUBLANES])
    return jnp.concatenate([head, xs[SUBLANES:]], axis=0)


def _causal_conv(x, w, tail_ref):
    kw = w.shape[0]
    tail = tail_ref[...]
    acc = x * w[kw - 1:kw]
    for d in range(1, kw):
        acc = acc + _shift_rows(x, tail, d) * w[kw - 1 - d:kw - d]
    tail_ref[...] = x[x.shape[0] - SUBLANES:]
    return acc


def _silu(x):
    return x * jax.nn.sigmoid(x)


def _unit_lower_inverses(lows, ii, jj):
    c = lows[0].shape[0]
    eye = jnp.where(ii == jj, 1.0, 0.0)
    pair = (ii >> 1) == (jj >> 1)
    xs = [eye - jnp.where(pair, low, 0.0) for low in lows]
    s = 1
    while (2 << s) <= c:
        m = ((ii >> (s + 1)) == (jj >> (s + 1))) & ((ii >> s) != (jj >> s))
        xbs = [x.astype(BF16) for x in xs]
        ts = [jnp.dot(jnp.where(m, low, 0.0).astype(BF16), xb, preferred_element_type=F32)
              for low, xb in zip(lows, xbs)]
        xs = [x - jnp.dot(xb, t.astype(BF16), preferred_element_type=F32) for x, xb, t in zip(xs, xbs, ts)]
        s += 1
    return xs


def _gdn_kernel(q_ref, k_ref, v_ref, z_ref, cwq_ref, cwk_ref, cwv_ref, gcol_ref, grow_ref, nw_ref,
                o_ref, s_ref, tq_ref, tk_ref, tv_ref, qn_ref, kn_ref, vn_ref, u_ref, wq_ref, attn_ref, kd_ref,
                *, nq, vper, dk):
    @pl.when(pl.program_id(2) == 0)
    def _():
        s_ref[...] = jnp.zeros_like(s_ref)
        tq_ref[...] = jnp.zeros_like(tq_ref)
        tk_ref[...] = jnp.zeros_like(tk_ref)
        tv_ref[...] = jnp.zeros_like(tv_ref)

    tb = q_ref.shape[0]
    c = GDN_CHUNK
    nc = tb // c
    nv = nq * vper
    head = lambda h: slice(h * dk, (h + 1) * dk)

    q = _silu(_causal_conv(q_ref[...].astype(F32), cwq_ref[...], tq_ref))
    k = _silu(_causal_conv(k_ref[...].astype(F32), cwk_ref[...], tk_ref))
    for hq in range(nq):
        qh = q[:, head(hq)]
        kh = k[:, head(hq)]
        qn_ref[:, head(hq)] = qh * (lax.rsqrt(jnp.sum(qh * qh, axis=-1, keepdims=True) + RMS_EPS) * (dk ** -0.5))
        kn_ref[:, head(hq)] = kh * lax.rsqrt(jnp.sum(kh * kh, axis=-1, keepdims=True) + RMS_EPS)
    vn_ref[...] = _silu(_causal_conv(v_ref[...].astype(F32), cwv_ref[...], tv_ref))

    cols = gcol_ref[...]
    grows = grow_ref[...]
    nw = nw_ref[...]
    ii = lax.broadcasted_iota(I32, (c, c), 0)
    jj = lax.broadcasted_iota(I32, (c, c), 1)
    incl = ii >= jj
    strict = ii > jj

    def col(ci, which, h):
        return cols[ci * c:(ci + 1) * c, which * nv + h:which * nv + h + 1]

    problems = [(ci, h) for ci in range(nc) for h in range(nv)]
    gram = {}
    for g0 in range(0, len(problems), GDN_INV_GROUP):
        group = problems[g0:g0 + GDN_INV_GROUP]
        lows = []
        for ci, h in group:
            hq = h // vper
            rows = slice(ci * c, (ci + 1) * c)
            if (ci, hq) not in gram:
                kcb = kn_ref[rows, head(hq)].astype(BF16)
                qcb = qn_ref[rows, head(hq)].astype(BF16)
                gram[(ci, hq)] = lax.dot_general(jnp.concatenate([kcb, qcb], axis=0), kcb,
                                                 (((1,), (1,)), ((), ())), preferred_element_type=F32)
            a = gram[(ci, hq)]
            grow = grows[h:h + 1, rows]
            decay = jnp.where(incl, jnp.exp(jnp.minimum(col(ci, 1, h) - grow, 0.0)), 0.0)
            lows.append(jnp.where(strict, col(ci, 0, h) * a[:c] * decay, 0.0))
            attn_ref[ci * nv + h] = jnp.where(incl, a[c:] * decay, 0.0).astype(BF16)
        xs = _unit_lower_inverses(lows, ii, jj)
        for x, (ci, h) in zip(xs, group):
            hq = h // vper
            rows = slice(ci * c, (ci + 1) * c)
            p = ci * nv + h
            kc = kn_ref[rows, head(hq)]
            bcol = col(ci, 0, h)
            eg = col(ci, 2, h)
            rhs = jnp.concatenate([vn_ref[rows, head(h)] * bcol, kc * (bcol * eg)], axis=1).astype(BF16)
            uw = jnp.dot(x.astype(BF16), rhs, preferred_element_type=F32)
            u_ref[p] = uw[:, :dk]
            wq_ref[p, :c, :] = uw[:, dk:].astype(BF16)
            wq_ref[p, c:, :] = (qn_ref[rows, head(hq)] * eg).astype(BF16)
            kd_ref[p] = (kc * col(ci, 3, h)).astype(BF16)

    for ci in range(nc):
        rows = slice(ci * c, (ci + 1) * c)
        ps = [ci * nv + h for h in range(nv)]
        r1 = [jnp.dot(wq_ref[p], s_ref[h].astype(BF16), preferred_element_type=F32) for h, p in enumerate(ps)]
        v_new = [(u_ref[p] - r[:c]).astype(BF16) for p, r in zip(ps, r1)]
        outs = [r[c:] + jnp.dot(attn_ref[p], vn, preferred_element_type=F32) for p, r, vn in zip(ps, r1, v_new)]
        for h, (p, vn) in enumerate(zip(ps, v_new)):
            g_tot = cols[(ci + 1) * c - 1:(ci + 1) * c, 2 * nv + h:2 * nv + h + 1]
            s_ref[h] = s_ref[h] * g_tot + lax.dot_general(
                kd_ref[p], vn, (((0,), (0,)), ((), ())), preferred_element_type=F32)
        for h, o in enumerate(outs):
            zc = z_ref[rows, head(h)].astype(F32)
            var = jnp.mean(o * o, axis=-1, keepdims=True)
            o_ref[rows, head(h)] = (o * lax.rsqrt(var + RMS_EPS) * nw * _silu(zc)).astype(BF16)


def _gdn(proj_a, conv_w, gcol, grow, norm_w, *, bsz, seq, hq, hv, dk):
    t = bsz * seq
    vper = hv // hq
    nq = _pick(hq, (GDN_QK_GROUP, 2, 1))
    nv = nq * vper
    ng = hq // nq
    tb = _pick(seq, (GDN_STEP, 128, 64))
    ns = seq // tb
    nprob = (tb // GDN_CHUNK) * nv
    qw = nq * dk
    vw = nv * dk
    v_blk0 = 2 * hq * dk // vw
    z_blk0 = (2 * hq * dk + hv * dk) // vw
    kw = conv_w.shape[0]
    gcol = gcol[:, :4 * hv].reshape(t, 4, ng, nv).transpose(2, 0, 1, 3).reshape(ng, t, 4 * nv)
    grow = grow[hv:2 * hv].reshape(ng, nv, t)
    rowblk = lambda b, g, s: b * ns + s
    kern = functools.partial(_gdn_kernel, nq=nq, vper=vper, dk=dk)
    return pl.pallas_call(
        kern,
        grid=(bsz, ng, ns),
        in_specs=[
            pl.BlockSpec((tb, qw), lambda b, g, s: (rowblk(b, g, s), g)),
            pl.BlockSpec((tb, qw), lambda b, g, s: (rowblk(b, g, s), ng + g)),
            pl.BlockSpec((tb, vw), lambda b, g, s: (rowblk(b, g, s), v_blk0 + g)),
            pl.BlockSpec((tb, vw), lambda b, g, s: (rowblk(b, g, s), z_blk0 + g)),
            pl.BlockSpec((kw, qw), lambda b, g, s: (0, g)),
            pl.BlockSpec((kw, qw), lambda b, g, s: (0, ng + g)),
            pl.BlockSpec((kw, vw), lambda b, g, s: (0, v_blk0 + g)),
            pl.BlockSpec((None, tb, 4 * nv), lambda b, g, s: (g, rowblk(b, g, s), 0)),
            pl.BlockSpec((None, nv, tb), lambda b, g, s: (g, 0, rowblk(b, g, s))),
            pl.BlockSpec((1, dk), lambda b, g, s: (0, 0)),
        ],
        out_specs=pl.BlockSpec((tb, vw), lambda b, g, s: (rowblk(b, g, s), g)),
        out_shape=jax.ShapeDtypeStruct((t, hv * dk), BF16),
        scratch_shapes=[pltpu.VMEM((nv, dk, dk), F32),
                        pltpu.VMEM((SUBLANES, qw), F32),
                        pltpu.VMEM((SUBLANES, qw), F32),
                        pltpu.VMEM((SUBLANES, vw), F32),
                        pltpu.VMEM((tb, qw), F32),
                        pltpu.VMEM((tb, qw), F32),
                        pltpu.VMEM((tb, vw), F32),
                        pltpu.VMEM((nprob, GDN_CHUNK, dk), F32),
                        pltpu.VMEM((nprob, 2 * GDN_CHUNK, dk), BF16),
                        pltpu.VMEM((nprob, GDN_CHUNK, GDN_CHUNK), BF16),
                        pltpu.VMEM((nprob, GDN_CHUNK, dk), BF16)],
        compiler_params=_params(("arbitrary", "arbitrary", "arbitrary")),
        name="gdn",
    )(proj_a, proj_a, proj_a, proj_a, conv_w, conv_w, conv_w, gcol, grow, norm_w.reshape(1, dk))


def _sc_kernel(b_ref, c_ref, x_ref, w_ref, u_ref, tail_ref):
    @pl.when(pl.program_id(1) == 0)
    def _():
        tail_ref[...] = jnp.zeros_like(tail_ref)

    p = c_ref[...].astype(F32) * x_ref[...].astype(F32)
    u_ref[...] = (b_ref[...].astype(F32) * _causal_conv(p, w_ref[...], tail_ref)).astype(BF16)


def _short_conv(proj_b, conv_w, *, bsz, seq, width):
    t = bsz * seq
    ts = _pick(seq, (512, 256, 128, 64))
    ns = seq // ts
    kw = conv_w.shape[0]
    return pl.pallas_call(
        _sc_kernel,
        grid=(bsz, ns),
        in_specs=[pl.BlockSpec((ts, width), lambda b, s: (b * ns + s, 0)),
                  pl.BlockSpec((ts, width), lambda b, s: (b * ns + s, 1)),
                  pl.BlockSpec((ts, width), lambda b, s: (b * ns + s, 2)),
                  pl.BlockSpec((kw, width), lambda b, s: (0, 0))],
        out_specs=pl.BlockSpec((ts, width), lambda b, s: (b * ns + s, 0)),
        out_shape=jax.ShapeDtypeStruct((t, width), BF16),
        scratch_shapes=[pltpu.VMEM((SUBLANES, width), F32)],
        compiler_params=_params(("arbitrary", "arbitrary")),
        name="short_conv",
    )(proj_b, proj_b, proj_b, conv_w)


def _router_kernel(h_ref, w_ref, b_ref, idx_ref, gate_ref, rank_ref, cnt_ref, run_ref):
    @pl.when(pl.program_id(0) == 0)
    def _():
        run_ref[...] = jnp.zeros_like(run_ref)

    tm = h_ref.shape[0]
    logits = jnp.dot(h_ref[...], w_ref[...], preferred_element_type=F32,
                     precision=lax.Precision.HIGHEST) + b_ref[...]
    lane = lax.broadcasted_iota(I32, logits.shape, 1)
    lane_f = lane.astype(F32)
    cur = logits
    idxs, vals = [], []
    for _ in range(TOP_K):
        m = jnp.max(cur, axis=-1, keepdims=True)
        ix = jnp.min(jnp.where(cur == m, lane_f, float(LANES)), axis=-1, keepdims=True).astype(I32)
        idxs.append(ix)
        vals.append(m)
        cur = jnp.where(lane == ix, -jnp.inf, cur)
    es = [jnp.exp(val - vals[0]) for val in vals]
    den = es[0]
    for e in es[1:]:
        den = den + e
    onehot = jnp.zeros(logits.shape, F32)
    for ix in idxs:
        onehot = onehot + jnp.where(lane == ix, 1.0, 0.0)
    ri = lax.broadcasted_iota(I32, (tm, tm), 0)
    ci = lax.broadcasted_iota(I32, (tm, tm), 1)
    tri = jnp.where(ri > ci, 1.0, 0.0).astype(BF16)
    before = jnp.dot(tri, onehot.astype(BF16), preferred_element_type=F32) + run_ref[...]
    idx_out = jnp.zeros(logits.shape, I32)
    rank_out = jnp.zeros(logits.shape, I32)
    gate_out = jnp.zeros(logits.shape, F32)
    for kk in range(TOP_K):
        rank = jnp.sum(jnp.where(lane == idxs[kk], before, 0.0), axis=-1, keepdims=True).astype(I32)
        idx_out = jnp.where(lane == kk, idxs[kk], idx_out)
        rank_out = jnp.where(lane == kk, rank, rank_out)
        gate_out = jnp.where(lane == kk, es[kk] / den, gate_out)
    idx_ref[...] = idx_out
    rank_ref[...] = rank_out
    gate_ref[...] = gate_out
    run_ref[...] = run_ref[...] + jnp.sum(onehot, axis=0, keepdims=True)
    cnt_ref[...] = run_ref[...].astype(I32)


def _router(h, w_router, b_router):
    t, d = h.shape
    e = w_router.shape[1]
    tm = _pick(t, (256, 128, 64, 32, 16, 8))
    wp = jnp.zeros((d, LANES), F32).at[:, :e].set(w_router)
    bp = jnp.full((1, LANES), -1e30, F32).at[0, :e].set(b_router)
    tile = pl.BlockSpec((tm, LANES), lambda i: (i, 0))
    return pl.pallas_call(
        _router_kernel,
        grid=(t // tm,),
        in_specs=[pl.BlockSpec((tm, d), lambda i: (i, 0)),
                  pl.BlockSpec((d, LANES), lambda i: (0, 0)),
                  pl.BlockSpec((1, LANES), lambda i: (0, 0))],
        out_specs=[tile, tile, tile, pl.BlockSpec((1, LANES), lambda i: (0, 0))],
        out_shape=[jax.ShapeDtypeStruct((t, LANES), I32), jax.ShapeDtypeStruct((t, LANES), F32),
                   jax.ShapeDtypeStruct((t, LANES), I32), jax.ShapeDtypeStruct((1, LANES), I32)],
        scratch_shapes=[pltpu.VMEM((1, LANES), F32)],
        compiler_params=_params(("arbitrary",)),
        name="router",
    )(h, wp, bp)


def _row_copy(src, s_row, dst, d_row, sem):
    return pltpu.make_async_copy(src.at[pl.ds(s_row, 1)], dst.at[pl.ds(d_row, 1)], sem)


def _dispatch_kernel(pos_ref, x_ref, rows_in_ref, rows_ref, sem):
    del rows_in_ref
    tm = x_ref.shape[0]
    base = pl.program_id(0) * (tm * TOP_K)

    def issue(t, carry):
        for kk in range(TOP_K):
            _row_copy(x_ref, t, rows_ref, pos_ref[base + t * TOP_K + kk], sem).start()
        return carry

    def drain(t, carry):
        for kk in range(TOP_K):
            _row_copy(x_ref, 0, rows_ref, 0, sem).wait()
        return carry

    lax.fori_loop(0, tm, issue, 0)
    lax.fori_loop(0, tm, drain, 0)


def _dispatch(h, pos, n_rows):
    t, d = h.shape
    tm = _pick(t, (256, 128, 64, 32, 16, 8))
    rows0 = jnp.zeros((n_rows, d), h.dtype)
    return pl.pallas_call(
        _dispatch_kernel,
        grid_spec=pltpu.PrefetchScalarGridSpec(
            num_scalar_prefetch=1,
            grid=(t // tm,),
            in_specs=[pl.BlockSpec((tm, d), lambda i, pos: (i, 0)),
                      pl.BlockSpec(memory_space=pl.ANY)],
            out_specs=pl.BlockSpec(memory_space=pl.ANY),
            scratch_shapes=[pltpu.SemaphoreType.DMA(())],
        ),
        out_shape=jax.ShapeDtypeStruct((n_rows, d), h.dtype),
        input_output_aliases={2: 0},
        compiler_params=_params(("arbitrary",)),
        name="dispatch",
    )(pos, h, rows0)


def _swiglu_interleaved(gu):
    lane = lax.broadcasted_iota(I32, (gu.shape[0], LANES), 1)
    even = (lane & 1) == 0
    parts = []
    for c in range(gu.shape[1] // LANES):
        g = gu[:, c * LANES:(c + 1) * LANES]
        gt = jnp.minimum(g, SWIGLU_LIMIT)
        glu = gt * jax.nn.sigmoid(SWIGLU_ALPHA * gt)
        up = jnp.clip(g, -SWIGLU_LIMIT, SWIGLU_LIMIT) + 1.0
        parts.append(jnp.where(even, pltpu.roll(up, LANES - 1, 1) * glu, 0.0))
    outs = [parts[2 * m] + pltpu.roll(parts[2 * m + 1], 1, 1) for m in range(len(parts) // 2)]
    return jnp.concatenate(outs, axis=1) if len(outs) > 1 else outs[0]


def _interleave_rows_bf16(w_ref, out_ref):
    half = LANES // 2
    for m in range(w_ref.shape[0] // LANES):
        lo = w_ref[m * LANES:m * LANES + half, :].astype(BF16).astype(F32)
        hi = w_ref[m * LANES + half:(m + 1) * LANES, :].astype(BF16).astype(F32)
        packed = (lax.bitcast_convert_type(lo, jnp.uint32) >> 16) | (
            lax.bitcast_convert_type(hi, jnp.uint32) & jnp.uint32(0xFFFF0000))
        out_ref[m * LANES:(m + 1) * LANES, :] = pltpu.bitcast(packed, BF16)


def _expert_kernel(ie_ref, ist_ref, inb_ref, rows_ref, wgu_ref, bgu_ref, wd_ref, bd_ref, y_ref,
                   xb_ref, acc_ref, stage_ref, wgub_ref, wdb_ref, sem_in, sem_out, *, tm, n_j):
    del ie_ref
    wi = pl.program_id(0)
    j = pl.program_id(1)
    n_items = pl.num_programs(0)
    nblk = inb_ref[wi]
    half = stage_ref.shape[1]

    def block(b):
        return pl.ds(pl.multiple_of(b * tm, tm), tm)

    def hbm_block(item, b):
        return pl.ds(pl.multiple_of(ist_ref[item] + b * tm, tm), tm)

    def rows_copy(item, b):
        return pltpu.make_async_copy(rows_ref.at[hbm_block(item, b)], stage_ref.at[block(b)], sem_in)

    def y_copy(item, b):
        return pltpu.make_async_copy(acc_ref.at[block(b)], y_ref.at[hbm_block(item, b)], sem_out)

    def for_blocks(item, fn):
        def body(b, carry):
            fn(item, b)
            return carry
        lax.fori_loop(0, inb_ref[item], body, 0)

    def unpack(item, b):
        lo, hi = _unpack_halves_bf16(stage_ref[block(b), :])
        xb_ref[block(b), :half] = lo
        xb_ref[block(b), half:] = hi

    @pl.when(j == 0)
    def _():
        @pl.when(wi == 0)
        def _():
            for_blocks(0, lambda it, b: rows_copy(it, b).start())

        @pl.when(wi > 0)
        def _():
            for_blocks(wi - 1, lambda it, b: y_copy(it, b).wait())

        for_blocks(wi, lambda it, b: rows_copy(it, b).wait())
        for_blocks(wi, unpack)

        @pl.when(wi + 1 < n_items)
        def _():
            for_blocks(wi + 1, lambda it, b: rows_copy(it, b).start())

    @pl.when(nblk > 0)
    def _():
        wgub_ref[...] = wgu_ref[...].astype(BF16)
        _interleave_rows_bf16(wd_ref, wdb_ref)
        bias = bgu_ref[...]

        def body(b, carry):
            gu = jnp.dot(xb_ref[block(b), :], wgub_ref[...], preferred_element_type=F32) + bias
            act = _swiglu_interleaved(gu).astype(BF16)
            contrib = jnp.dot(act, wdb_ref[...], preferred_element_type=F32)

            @pl.when(j == 0)
            def _():
                acc_ref[block(b), :] = contrib + bd_ref[...]

            @pl.when(j > 0)
            def _():
                acc_ref[block(b), :] += contrib
            return carry
        lax.fori_loop(0, nblk, body, 0)

    @pl.when(j == n_j - 1)
    def _():
        for_blocks(wi, lambda it, b: y_copy(it, b).start())

    @pl.when((wi == n_items - 1) & (j == n_j - 1))
    def _():
        for_blocks(wi, lambda it, b: y_copy(it, b).wait())
        acc_ref[0:tm, :] = jnp.zeros((tm, acc_ref.shape[1]), F32)

        def fill(b, carry):
            cp = pltpu.make_async_copy(acc_ref.at[0:tm], y_ref.at[pl.ds(pl.multiple_of(b * tm, tm), tm)], sem_out)
            cp.start()
            cp.wait()
            return carry
        lax.fori_loop(ist_ref[n_items] // tm, y_ref.shape[0] // tm, fill, 0)


def _experts(rows, item_e, item_start, item_nblk, w_gate_up, b_gate_up, w_down, b_down, *, tm, xmax):
    n_rows = rows.shape[0]
    n_exp, d, f2 = w_gate_up.shape
    assert rows.shape[1] * 2 == d
    f = f2 // 2
    tf = _pick(f, (MOE_TF, LANES))
    n_j = f // tf
    n_items = item_e.shape[0]

    def jeff(j, inb, wi):
        return jnp.where(inb[wi] > 0, j, n_j - 1)

    kern = functools.partial(_expert_kernel, tm=tm, n_j=n_j)
    return pl.pallas_call(
        kern,
        grid_spec=pltpu.PrefetchScalarGridSpec(
            num_scalar_prefetch=3,
            grid=(n_items, n_j),
            in_specs=[
                pl.BlockSpec(memory_space=pl.ANY),
                pl.BlockSpec((None, d, 2 * tf), lambda wi, j, ie, ist, inb: (ie[wi], 0, jeff(j, inb, wi))),
                pl.BlockSpec((None, 1, 2 * tf), lambda wi, j, ie, ist, inb: (ie[wi], 0, jeff(j, inb, wi))),
                pl.BlockSpec((None, tf, d), lambda wi, j, ie, ist, inb: (ie[wi], jeff(j, inb, wi), 0)),
                pl.BlockSpec((None, 1, d), lambda wi, j, ie, ist, inb: (ie[wi], 0, 0)),
            ],
            out_specs=pl.BlockSpec(memory_space=pl.ANY),
            scratch_shapes=[pltpu.VMEM((xmax, d), BF16),
                            pltpu.VMEM((xmax, d), F32),
                            pltpu.VMEM((xmax, d // 2), U32),
                            pltpu.VMEM((d, 2 * tf), BF16),
                            pltpu.VMEM((tf, d), BF16),
                            pltpu.SemaphoreType.DMA(()),
                            pltpu.SemaphoreType.DMA(())],
        ),
        out_shape=jax.ShapeDtypeStruct((n_rows, d), F32),
        compiler_params=_params(("arbitrary", "arbitrary")),
        name="experts",
    )(item_e, item_start, item_nblk, rows, w_gate_up, b_gate_up.reshape(n_exp, 1, f2),
      w_down, b_down.reshape(n_exp, 1, d))


def _combine_kernel(pos_ref, y_ref, gate_ref, h_ref, g_ref, b_ref, o_ref, ybuf_ref, sem, *, alpha):
    tm = h_ref.shape[0]
    base = pl.program_id(0) * (tm * TOP_K)

    def issue(t, carry):
        for kk in range(TOP_K):
            _row_copy(y_ref, pos_ref[base + t * TOP_K + kk], ybuf_ref.at[kk], t, sem).start()
        return carry

    def drain(t, carry):
        for kk in range(TOP_K):
            _row_copy(y_ref, 0, ybuf_ref.at[kk], 0, sem).wait()
        return carry

    lax.fori_loop(0, tm, issue, 0)
    lax.fori_loop(0, tm, drain, 0)
    gate = gate_ref[...]
    ffn = gate[:, 0:1] * ybuf_ref[0]
    for kk in range(1, TOP_K):
        ffn = ffn + gate[:, kk:kk + 1] * ybuf_ref[kk]
    o_ref[...] = _layer_norm(alpha * h_ref[...] + ffn, g_ref[...], b_ref[...])


def _combine(y, pos, gate, h, ln_g, ln_b, alpha):
    t, d = h.shape
    tm = _pick(t, (128, 64, 32, 16, 8))
    return pl.pallas_call(
        functools.partial(_combine_kernel, alpha=alpha),
        grid_spec=pltpu.PrefetchScalarGridSpec(
            num_scalar_prefetch=1,
            grid=(t // tm,),
            in_specs=[pl.BlockSpec(memory_space=pl.ANY),
                      pl.BlockSpec((tm, LANES), lambda i, pos: (i, 0)),
                      pl.BlockSpec((tm, d), lambda i, pos: (i, 0)),
                      pl.BlockSpec((1, d), lambda i, pos: (0, 0)),
                      pl.BlockSpec((1, d), lambda i, pos: (0, 0))],
            out_specs=pl.BlockSpec((tm, d), lambda i, pos: (i, 0)),
            scratch_shapes=[pltpu.VMEM((TOP_K, tm, d), F32), pltpu.SemaphoreType.DMA(())],
        ),
        out_shape=jax.ShapeDtypeStruct((t, d), F32),
        compiler_params=_params(("arbitrary",)),
        name="combine",
    )(pos, y, gate, h, ln_g.reshape(1, d), ln_b.reshape(1, d))


def _moe_plan(idx, rank, counts, *, tm, xmax, n_items):
    n_exp = counts.shape[0]
    padded = (counts + tm - 1) // tm * tm
    pstart = jnp.cumsum(padded) - padded
    pos = (pstart[idx] + rank).reshape(-1).astype(I32)
    per_e = (padded + xmax - 1) // xmax
    cum = jnp.cumsum(per_e)
    total = cum[-1]
    wi = jnp.arange(n_items, dtype=I32)
    valid = wi < total
    e_w = jnp.minimum(jnp.searchsorted(cum, jnp.minimum(wi, total - 1), side="right"), n_exp - 1).astype(I32)
    local = jnp.minimum(wi, total - 1) - (cum - per_e)[e_w]
    start = (pstart[e_w] + local * xmax).astype(I32)
    nblk = jnp.where(valid, jnp.clip(padded[e_w] - local * xmax, 0, xmax) // tm, 0).astype(I32)
    start = jnp.concatenate([start, jnp.sum(padded, keepdims=True).astype(I32)])
    return pos, e_w, start, nblk


def _layer(h0, h0b, p, *, bsz, seq, alpha):
    t, d = h0.shape
    w_in = p["w_in"]
    hv = p["gdn_a_log"].shape[0]
    dk = p["gdn_norm_w"].shape[0]
    qkv_dim = p["gdn_conv_w"].shape[1]
    v_dim = hv * dk
    hq = (qkv_dim - v_dim) // (2 * dk)
    scw = p["sc_conv_w"].shape[1]
    assert 2 * hv <= LANES and hv % hq == 0
    n_a = qkv_dim + v_dim
    off_ba = n_a
    off_b = n_a + 2 * hv
    n_b = 3 * scw + 2 * d
    assert w_in.shape[1] == off_b + n_b

    tm = _pick(t, (1024, 512, 256, 128))
    tn_a = _pick(n_a, (1024, 512, 256, 128))
    (proj_a,) = _matmul(h0b, w_in, col0=0, n_cols=n_a, tm=tm, tn=tn_a, out_dtypes=(BF16,))
    tn_b = _pick(n_b, (1024, 512, 256, 128))
    (proj_b,) = _matmul(h0b, w_in[:, off_b:], col0=0, n_cols=n_b, tm=tm, tn=tn_b, out_dtypes=(BF16,))
    w_ba = jnp.zeros((d, LANES), F32).at[:, :2 * hv].set(w_in[:, off_ba:off_b])
    (ba,) = _matmul(h0b, w_ba, col0=0, n_cols=LANES, tm=tm, tn=LANES, out_dtypes=(F32,))

    gcol, grow = _gdn_gates(ba, p["gdn_a_log"], p["gdn_dt_bias"], hv)
    o_n = _gdn(proj_a, p["gdn_conv_w"], gcol, grow, p["gdn_norm_w"], bsz=bsz, seq=seq, hq=hq, hv=hv, dk=dk)

    u = _short_conv(proj_b, p["sc_conv_w"], bsz=bsz, seq=seq, width=scw)

    tn = _pick(d, (512, 256, 128))
    ga0 = 3 * scw // tn
    gb0 = (3 * scw + d) // tn
    (part_a,) = _matmul(
        o_n, p["w_out_gdn"], col0=0, n_cols=d, tm=tm, tn=tn, out_dtypes=(F32,),
        epilogue=lambda acc, ga: (jax.nn.sigmoid(ga.astype(F32)) * acc,),
        extras=[(proj_b, (tm, tn), lambda j, i: (i, ga0 + j))])
    (merged,) = _matmul(
        u, p["w_out_sc"], col0=0, n_cols=d, tm=tm, tn=tn, out_dtypes=(BF16,),
        epilogue=lambda acc, gb, pa: (pa + jax.nn.sigmoid(gb.astype(F32)) * acc,),
        extras=[(proj_b, (tm, tn), lambda j, i: (i, gb0 + j)),
                (part_a, (tm, tn), lambda j, i: (i, j))])
    tm3 = _pick(t, (256, 128))
    def mix_epilogue(acc, hh, g, b):
        hn = _layer_norm(alpha * hh + acc, g, b)
        return hn, _pack_halves_bf16(hn)

    h1, h1p = _matmul(
        merged, p["w_out"].astype(BF16), col0=0, n_cols=d, tm=tm3, tn=d, out_dtypes=(F32, U32), out_div=(1, 2),
        epilogue=mix_epilogue,
        extras=[(h0, (tm3, d), lambda j, i: (i, 0)),
                (p["ln_mix_g"].reshape(1, d), (1, d), lambda j, i: (0, 0)),
                (p["ln_mix_b"].reshape(1, d), (1, d), lambda j, i: (0, 0))])

    n_exp = p["w_router"].shape[1]
    idx, gate, rank, cnt = _router(h1, p["w_router"], p["b_router"])
    mtm = MOE_TM
    xmax = MOE_XMAX
    n_rows = t * TOP_K + n_exp * mtm
    n_items = n_exp + n_rows // xmax
    pos, item_e, item_start, item_nblk = _moe_plan(
        idx[:, :TOP_K], rank[:, :TOP_K], cnt[0, :n_exp], tm=mtm, xmax=xmax, n_items=n_items)
    rows = _dispatch(h1p, pos, n_rows)
    y = _experts(rows, item_e, item_start, item_nblk, p["w_gate_up"], p["b_gate_up"], p["w_down"], p["b_down"],
                 tm=mtm, xmax=xmax)
    h2 = _combine(y, pos, gate, h1, p["ln_ffn_g"], p["ln_ffn_b"], alpha)
    return h2


_LAYER_PARAMS = ("w_in", "gdn_conv_w", "gdn_a_log", "gdn_dt_bias", "gdn_norm_w", "w_out_gdn", "sc_conv_w",
                 "w_out_sc", "w_out", "ln_mix_g", "ln_mix_b", "w_router", "b_router", "w_gate_up", "b_gate_up",
                 "w_down", "b_down", "ln_ffn_g", "ln_ffn_b")


def kernel(x, ln_in_g, ln_in_b, w_in, gdn_conv_w, gdn_a_log, gdn_dt_bias, gdn_norm_w, w_out_gdn, sc_conv_w,
           w_out_sc, w_out, ln_mix_g, ln_mix_b, w_router, b_router, w_gate_up, b_gate_up, w_down, b_down,
           ln_ffn_g, ln_ffn_b):
    stacked = dict(zip(_LAYER_PARAMS, (w_in, gdn_conv_w, gdn_a_log, gdn_dt_bias, gdn_norm_w, w_out_gdn,
                                       sc_conv_w, w_out_sc, w_out, ln_mix_g, ln_mix_b, w_router, b_router,
                                       w_gate_up, b_gate_up, w_down, b_down, ln_ffn_g, ln_ffn_b)))
    bsz, seq, d = x.shape
    depth = w_in.shape[0]
    alpha = (2 * depth) ** 0.25
    h, hb = _ln_in(x.reshape(bsz * seq, d), ln_in_g, ln_in_b)
    for l in range(depth):
        p = {name: arr[l] for name, arr in stacked.items()}
        h = _layer(h, hb, p, bsz=bsz, seq=seq, alpha=alpha)
        if l + 1 < depth:
            hb = h.astype(BF16)
    return h.reshape(bsz, seq, d)
```

```python
import functools

import jax
import jax.numpy as jnp
from jax import lax
from jax.experimental import pallas as pl
from jax.experimental.pallas import tpu as pltpu

F32 = jnp.float32
BF16 = jnp.bfloat16
I32 = jnp.int32
U32 = jnp.uint32

LANES = 128
SUBLANES = 8
VMEM_LIMIT = 60 << 20

TOP_K = 4
SWIGLU_LIMIT = 7.0
SWIGLU_ALPHA = 1.702
LN_EPS = 1e-5
RMS_EPS = 1e-6
GDN_CHUNK = 64
GDN_STEP = 256
GDN_QK_GROUP = 4
GDN_INV_GROUP = 16
MOE_TM = 256
MOE_XMAX = 1280
MOE_TF = 512


def _params(sem):
    return pltpu.CompilerParams(dimension_semantics=sem, vmem_limit_bytes=VMEM_LIMIT)


def _pick(n, candidates):
    for c in candidates:
        if n % c == 0:
            return c
    raise ValueError(f"no tile for {n} in {candidates}")


def _layer_norm(xf, g, b):
    mu = jnp.mean(xf, axis=-1, keepdims=True)
    xc = xf - mu
    var = jnp.mean(xc * xc, axis=-1, keepdims=True)
    return xc * lax.rsqrt(var + LN_EPS) * g + b


def _pack_halves_bf16(x):
    half = x.shape[1] // 2
    lo = lax.bitcast_convert_type(x[:, :half].astype(BF16).astype(F32), U32) >> 16
    hi = lax.bitcast_convert_type(x[:, half:].astype(BF16).astype(F32), U32) & jnp.uint32(0xFFFF0000)
    return lo | hi


def _unpack_halves_bf16(p):
    lo = lax.bitcast_convert_type(p << 16, F32).astype(BF16)
    hi = lax.bitcast_convert_type(p & jnp.uint32(0xFFFF0000), F32).astype(BF16)
    return lo, hi


def _ln_in_kernel(x_ref, g_ref, b_ref, h_ref, hb_ref):
    h = _layer_norm(x_ref[...], g_ref[...], b_ref[...])
    h_ref[...] = h
    hb_ref[...] = h.astype(BF16)


def _ln_in(x2, g, b):
    t, d = x2.shape
    tm = _pick(t, (512, 256, 128, 64, 32, 16))
    return pl.pallas_call(
        _ln_in_kernel,
        grid=(t // tm,),
        in_specs=[pl.BlockSpec((tm, d), lambda i: (i, 0)),
                  pl.BlockSpec((1, d), lambda i: (0, 0)),
                  pl.BlockSpec((1, d), lambda i: (0, 0))],
        out_specs=[pl.BlockSpec((tm, d), lambda i: (i, 0)),
                   pl.BlockSpec((tm, d), lambda i: (i, 0))],
        out_shape=[jax.ShapeDtypeStruct((t, d), F32), jax.ShapeDtypeStruct((t, d), BF16)],
        compiler_params=_params(("arbitrary",)),
        name="ln_in",
    )(x2, g.reshape(1, d), b.reshape(1, d))


def _mm_kernel(*refs, shift, n_extra, n_out, epilogue):
    n_w = 2 if shift else 1
    x_ref, w_refs = refs[0], refs[1:1 + n_w]
    extras = refs[1 + n_w:1 + n_w + n_extra]
    o_refs = refs[1 + n_w + n_extra:1 + n_w + n_extra + n_out]
    wb_ref = refs[-1]

    @pl.when(pl.program_id(1) == 0)
    def _():
        if shift:
            w = jnp.concatenate([w_refs[0][:, shift:], w_refs[1][:, :shift]], axis=1)
        else:
            w = w_refs[0][...]
        wb_ref[...] = w.astype(BF16)

    acc = jnp.dot(x_ref[...], wb_ref[...], preferred_element_type=F32)
    outs = epilogue(acc, *[e[...] for e in extras])
    for o_ref, o in zip(o_refs, outs):
        o_ref[...] = o.astype(o_ref.dtype)


def _matmul(x, w, *, col0, n_cols, tm, tn, out_dtypes, epilogue=None, extras=(), out_div=None):
    t, k = x.shape
    assert w.shape[0] == k and n_cols % tn == 0 and t % tm == 0 and col0 + n_cols <= w.shape[1]
    j0, shift = divmod(col0, tn)
    if epilogue is None:
        epilogue = lambda acc: (acc,)
    if out_div is None:
        out_div = (1,) * len(out_dtypes)
    kern = functools.partial(_mm_kernel, shift=shift, n_extra=len(extras), n_out=len(out_dtypes),
                             epilogue=epilogue)
    in_specs = [pl.BlockSpec((tm, k), lambda j, i: (i, 0)),
                pl.BlockSpec((k, tn), lambda j, i: (0, j + j0))]
    ws = [w]
    if shift:
        in_specs.append(pl.BlockSpec((k, tn), lambda j, i: (0, j + j0 + 1)))
        ws.append(w)
    in_specs += [pl.BlockSpec(bs, im) for (_, bs, im) in extras]
    return pl.pallas_call(
        kern,
        grid=(n_cols // tn, t // tm),
        in_specs=in_specs,
        out_specs=[pl.BlockSpec((tm, tn // dv), lambda j, i: (i, j)) for dv in out_div],
        out_shape=[jax.ShapeDtypeStruct((t, n_cols // dv), dt) for dt, dv in zip(out_dtypes, out_div)],
        scratch_shapes=[pltpu.VMEM((k, tn), BF16)],
        compiler_params=_params(("arbitrary", "arbitrary")),
        name="matmul",
    )(x, *ws, *[a for (a, _, _) in extras])


def _gates_kernel(ba_ref, alog_ref, dt_ref, col_ref, row_ref, *, hv):
    x = ba_ref[...]
    lane = lax.broadcasted_iota(I32, x.shape, 1)
    beta = jax.nn.sigmoid(x)
    xs = x + dt_ref[...]
    softplus = jnp.maximum(xs, 0.0) + jnp.log(1.0 + jnp.exp(-jnp.abs(xs)))
    g = -jnp.exp(alog_ref[...]) * softplus
    g = jnp.where((lane >= hv) & (lane < 2 * hv), g, 0.0)
    n = x.shape[0]
    row = lax.broadcasted_iota(I32, x.shape, 0) & (GDN_CHUNK - 1)
    cum = g
    rev = g
    sh = 1
    while sh < GDN_CHUNK:
        cum = cum + jnp.where(row >= sh, pltpu.roll(cum, sh, 0), 0.0)
        rev = rev + jnp.where(row < GDN_CHUNK - sh, pltpu.roll(rev, n - sh, 0), 0.0)
        sh *= 2
    e_cum = pltpu.roll(jnp.exp(cum), hv, 1)
    e_rest = pltpu.roll(jnp.exp(rev - g), 2 * hv, 1)
    out = jnp.where(lane < hv, beta, jnp.where(lane < 2 * hv, cum, jnp.where(lane < 3 * hv, e_cum, e_rest)))
    col_ref[...] = out
    row_ref[...] = out.T


def _gdn_gates(ba, a_log, dt_bias, hv):
    assert 4 * hv <= LANES
    t = ba.shape[0]
    tm = _pick(t, (512, 256, 128))
    pad = lambda v: jnp.zeros((1, LANES), F32).at[0, hv:2 * hv].set(v.astype(F32))
    return pl.pallas_call(
        functools.partial(_gates_kernel, hv=hv),
        grid=(t // tm,),
        in_specs=[pl.BlockSpec((tm, LANES), lambda i: (i, 0)),
                  pl.BlockSpec((1, LANES), lambda i: (0, 0)),
                  pl.BlockSpec((1, LANES), lambda i: (0, 0))],
        out_specs=[pl.BlockSpec((tm, LANES), lambda i: (i, 0)),
                   pl.BlockSpec((LANES, tm), lambda i: (0, i))],
        out_shape=[jax.ShapeDtypeStruct((t, LANES), F32), jax.ShapeDtypeStruct((LANES, t), F32)],
        compiler_params=_params(("arbitrary",)),
        name="gdn_gates",
    )(ba, pad(a_log), pad(dt_bias))


def _shift_rows(x, tail, d):
    xs = pltpu.roll(x, d, 0)
    ts = pltpu.roll(tail, d, 0)
    row = lax.broadcasted_iota(I32, tail.shape, 0)
    head = jnp.where(row < d, ts, xs[:SUBLANES])
    return jnp.concatenate([head, xs[SUBLANES:]], axis=0)


def _causal_conv(x, w, tail_ref):
    kw = w.shape[0]
    tail = tail_ref[...]
    acc = x * w[kw - 1:kw]
    for d in range(1, kw):
        acc = acc + _shift_rows(x, tail, d) * w[kw - 1 - d:kw - d]
    tail_ref[...] = x[x.shape[0] - SUBLANES:]
    return acc


def _silu(x):
    return x * jax.nn.sigmoid(x)


def _unit_lower_inverses(lows, ii, jj):
    c = lows[0].shape[0]
    eye = jnp.where(ii == jj, 1.0, 0.0)
    pair = (ii >> 1) == (jj >> 1)
    xs = [eye - jnp.where(pair, low, 0.0) for low in lows]
    s = 1
    while (2 << s) <= c:
        m = ((ii >> (s + 1)) == (jj >> (s + 1))) & ((ii >> s) != (jj >> s))
        xbs = [x.astype(BF16) for x in xs]
        ts = [jnp.dot(jnp.where(m, low, 0.0).astype(BF16), xb, preferred_element_type=F32)
              for low, xb in zip(lows, xbs)]
        xs = [x - jnp.dot(xb, t.astype(BF16), preferred_element_type=F32) for x, xb, t in zip(xs, xbs, ts)]
        s += 1
    return xs


def _gdn_kernel(q_ref, k_ref, v_ref, z_ref, cwq_ref, cwk_ref, cwv_ref, gcol_ref, grow_ref, nw_ref,
                o_ref, s_ref, tq_ref, tk_ref, tv_ref, qn_ref, kn_ref, vn_ref, u_ref, wq_ref, attn_ref, kd_ref,
                *, nq, vper, dk):
    @pl.when(pl.program_id(2) == 0)
    def _():
        s_ref[...] = jnp.zeros_like(s_ref)
        tq_ref[...] = jnp.zeros_like(tq_ref)
        tk_ref[...] = jnp.zeros_like(tk_ref)
        tv_ref[...] = jnp.zeros_like(tv_ref)

    tb = q_ref.shape[0]
    c = GDN_CHUNK
    nc = tb // c
    nv = nq * vper
    head = lambda h: slice(h * dk, (h + 1) * dk)

    q = _silu(_causal_conv(q_ref[...].astype(F32), cwq_ref[...], tq_ref))
    k = _silu(_causal_conv(k_ref[...].astype(F32), cwk_ref[...], tk_ref))
    for hq in range(nq):
        qh = q[:, head(hq)]
        kh = k[:, head(hq)]
        qn_ref[:, head(hq)] = qh * (lax.rsqrt(jnp.sum(qh * qh, axis=-1, keepdims=True) + RMS_EPS) * (dk ** -0.5))
        kn_ref[:, head(hq)] = kh * lax.rsqrt(jnp.sum(kh * kh, axis=-1, keepdims=True) + RMS_EPS)
    vn_ref[...] = _silu(_causal_conv(v_ref[...].astype(F32), cwv_ref[...], tv_ref))

    cols = gcol_ref[...]
    grows = grow_ref[...]
    nw = nw_ref[...]
    ii = lax.broadcasted_iota(I32, (c, c), 0)
    jj = lax.broadcasted_iota(I32, (c, c), 1)
    incl = ii >= jj
    strict = ii > jj

    def col(ci, which, h):
        return cols[ci * c:(ci + 1) * c, which * nv + h:which * nv + h + 1]

    problems = [(ci, h) for ci in range(nc) for h in range(nv)]
    gram = {}
    for g0 in range(0, len(problems), GDN_INV_GROUP):
        group = problems[g0:g0 + GDN_INV_GROUP]
        lows = []
        for ci, h in group:
            hq = h // vper
            rows = slice(ci * c, (ci + 1) * c)
            if (ci, hq) not in gram:
                kcb = kn_ref[rows, head(hq)].astype(BF16)
                qcb = qn_ref[rows, head(hq)].astype(BF16)
                gram[(ci, hq)] = lax.dot_general(jnp.concatenate([kcb, qcb], axis=0), kcb,
                                                 (((1,), (1,)), ((), ())), preferred_element_type=F32)
            a = gram[(ci, hq)]
            grow = grows[h:h + 1, rows]
            decay = jnp.where(incl, jnp.exp(jnp.minimum(col(ci, 1, h) - grow, 0.0)), 0.0)
            lows.append(jnp.where(strict, col(ci, 0, h) * a[:c] * decay, 0.0))
            attn_ref[ci * nv + h] = jnp.where(incl, a[c:] * decay, 0.0).astype(BF16)
        xs = _unit_lower_inverses(lows, ii, jj)
        for x, (ci, h) in zip(xs, group):
            hq = h // vper
            rows = slice(ci * c, (ci + 1) * c)
            p = ci * nv + h
            kc = kn_ref[rows, head(hq)]
            bcol = col(ci, 0, h)
            eg = col(ci, 2, h)
            rhs = jnp.concatenate([vn_ref[rows, head(h)] * bcol, kc * (bcol * eg)], axis=1).astype(BF16)
            uw = jnp.dot(x.astype(BF16), rhs, preferred_element_type=F32)
            u_ref[p] = uw[:, :dk]
            wq_ref[p, :c, :] = uw[:, dk:].astype(BF16)
            wq_ref[p, c:, :] = (qn_ref[rows, head(hq)] * eg).astype(BF16)
            kd_ref[p] = (kc * col(ci, 3, h)).astype(BF16)

    for ci in range(nc):
        rows = slice(ci * c, (ci + 1) * c)
        ps = [ci * nv + h for h in range(nv)]
        r1 = [jnp.dot(wq_ref[p], s_ref[h].astype(BF16), preferred_element_type=F32) for h, p in enumerate(ps)]
        v_new = [(u_ref[p] - r[:c]).astype(BF16) for p, r in zip(ps, r1)]
        outs = [r[c:] + jnp.dot(attn_ref[p], vn, preferred_element_type=F32) for p, r, vn in zip(ps, r1, v_new)]
        for h, (p, vn) in enumerate(zip(ps, v_new)):
            g_tot = cols[(ci + 1) * c - 1:(ci + 1) * c, 2 * nv + h:2 * nv + h + 1]
            s_ref[h] = s_ref[h] * g_tot + lax.dot_general(
                kd_ref[p], vn, (((0,), (0,)), ((), ())), preferred_element_type=F32)
        for h, o in enumerate(outs):
            zc = z_ref[rows, head(h)].astype(F32)
            var = jnp.mean(o * o, axis=-1, keepdims=True)
            o_ref[rows, head(h)] = (o * lax.rsqrt(var + RMS_EPS) * nw * _silu(zc)).astype(BF16)


def _gdn(proj_a, conv_w, gcol, grow, norm_w, *, bsz, seq, hq, hv, dk):
    t = bsz * seq
    vper = hv // hq
    nq = _pick(hq, (GDN_QK_GROUP, 2, 1))
    nv = nq * vper
    ng = hq // nq
    tb = _pick(seq, (GDN_STEP, 128, 64))
    ns = seq // tb
    nprob = (tb // GDN_CHUNK) * nv
    qw = nq * dk
    vw = nv * dk
    v_blk0 = 2 * hq * dk // vw
    z_blk0 = (2 * hq * dk + hv * dk) // vw
    kw = conv_w.shape[0]
    gcol = gcol[:, :4 * hv].reshape(t, 4, ng, nv).transpose(2, 0, 1, 3).reshape(ng, t, 4 * nv)
    grow = grow[hv:2 * hv].reshape(ng, nv, t)
    rowblk = lambda b, g, s: b * ns + s
    kern = functools.partial(_gdn_kernel, nq=nq, vper=vper, dk=dk)
    return pl.pallas_call(
        kern,
        grid=(bsz, ng, ns),
        in_specs=[
            pl.BlockSpec((tb, qw), lambda b, g, s: (rowblk(b, g, s), g)),
            pl.BlockSpec((tb, qw), lambda b, g, s: (rowblk(b, g, s), ng + g)),
            pl.BlockSpec((tb, vw), lambda b, g, s: (rowblk(b, g, s), v_blk0 + g)),
            pl.BlockSpec((tb, vw), lambda b, g, s: (rowblk(b, g, s), z_blk0 + g)),
            pl.BlockSpec((kw, qw), lambda b, g, s: (0, g)),
            pl.BlockSpec((kw, qw), lambda b, g, s: (0, ng + g)),
            pl.BlockSpec((kw, vw), lambda b, g, s: (0, v_blk0 + g)),
            pl.BlockSpec((None, tb, 4 * nv), lambda b, g, s: (g, rowblk(b, g, s), 0)),
            pl.BlockSpec((None, nv, tb), lambda b, g, s: (g, 0, rowblk(b, g, s))),
            pl.BlockSpec((1, dk), lambda b, g, s: (0, 0)),
        ],
        out_specs=pl.BlockSpec((tb, vw), lambda b, g, s: (rowblk(b, g, s), g)),
        out_shape=jax.ShapeDtypeStruct((t, hv * dk), BF16),
        scratch_shapes=[pltpu.VMEM((nv, dk, dk), F32),
                        pltpu.VMEM((SUBLANES, qw), F32),
                        pltpu.VMEM((SUBLANES, qw), F32),
                        pltpu.VMEM((SUBLANES, vw), F32),
                        pltpu.VMEM((tb, qw), F32),
                        pltpu.VMEM((tb, qw), F32),
                        pltpu.VMEM((tb, vw), F32),
                        pltpu.VMEM((nprob, GDN_CHUNK, dk), F32),
                        pltpu.VMEM((nprob, 2 * GDN_CHUNK, dk), BF16),
                        pltpu.VMEM((nprob, GDN_CHUNK, GDN_CHUNK), BF16),
                        pltpu.VMEM((nprob, GDN_CHUNK, dk), BF16)],
        compiler_params=_params(("arbitrary", "arbitrary", "arbitrary")),
        name="gdn",
    )(proj_a, proj_a, proj_a, proj_a, conv_w, conv_w, conv_w, gcol, grow, norm_w.reshape(1, dk))


def _sc_kernel(b_ref, c_ref, x_ref, w_ref, u_ref, tail_ref):
    @pl.when(pl.program_id(1) == 0)
    def _():
        tail_ref[...] = jnp.zeros_like(tail_ref)

    p = c_ref[...].astype(F32) * x_ref[...].astype(F32)
    u_ref[...] = (b_ref[...].astype(F32) * _causal_conv(p, w_ref[...], tail_ref)).astype(BF16)


def _short_conv(proj_b, conv_w, *, bsz, seq, width):
    t = bsz * seq
    ts = _pick(seq, (512, 256, 128, 64))
    ns = seq // ts
    kw = conv_w.shape[0]
    return pl.pallas_call(
        _sc_kernel,
        grid=(bsz, ns),
        in_specs=[pl.BlockSpec((ts, width), lambda b, s: (b * ns + s, 0)),
                  pl.BlockSpec((ts, width), lambda b, s: (b * ns + s, 1)),
                  pl.BlockSpec((ts, width), lambda b, s: (b * ns + s, 2)),
                  pl.BlockSpec((kw, width), lambda b, s: (0, 0))],
        out_specs=pl.BlockSpec((ts, width), lambda b, s: (b * ns + s, 0)),
        out_shape=jax.ShapeDtypeStruct((t, width), BF16),
        scratch_shapes=[pltpu.VMEM((SUBLANES, width), F32)],
        compiler_params=_params(("arbitrary", "arbitrary")),
        name="short_conv",
    )(proj_b, proj_b, proj_b, conv_w)


def _router_kernel(h_ref, w_ref, b_ref, idx_ref, gate_ref, rank_ref, cnt_ref, run_ref):
    @pl.when(pl.program_id(0) == 0)
    def _():
        run_ref[...] = jnp.zeros_like(run_ref)

    tm = h_ref.shape[0]
    logits = jnp.dot(h_ref[...], w_ref[...], preferred_element_type=F32,
                     precision=lax.Precision.HIGHEST) + b_ref[...]
    lane = lax.broadcasted_iota(I32, logits.shape, 1)
    lane_f = lane.astype(F32)
    cur = logits
    idxs, vals = [], []
    for _ in range(TOP_K):
        m = jnp.max(cur, axis=-1, keepdims=True)
        ix = jnp.min(jnp.where(cur == m, lane_f, float(LANES)), axis=-1, keepdims=True).astype(I32)
        idxs.append(ix)
        vals.append(m)
        cur = jnp.where(lane == ix, -jnp.inf, cur)
    es = [jnp.exp(val - vals[0]) for val in vals]
    den = es[0]
    for e in es[1:]:
        den = den + e
    onehot = jnp.zeros(logits.shape, F32)
    for ix in idxs:
        onehot = onehot + jnp.where(lane == ix, 1.0, 0.0)
    ri = lax.broadcasted_iota(I32, (tm, tm), 0)
    ci = lax.broadcasted_iota(I32, (tm, tm), 1)
    tri = jnp.where(ri > ci, 1.0, 0.0).astype(BF16)
    before = jnp.dot(tri, onehot.astype(BF16), preferred_element_type=F32) + run_ref[...]
    idx_out = jnp.zeros(logits.shape, I32)
    rank_out = jnp.zeros(logits.shape, I32)
    gate_out = jnp.zeros(logits.shape, F32)
    for kk in range(TOP_K):
        rank = jnp.sum(jnp.where(lane == idxs[kk], before, 0.0), axis=-1, keepdims=True).astype(I32)
        idx_out = jnp.where(lane == kk, idxs[kk], idx_out)
        rank_out = jnp.where(lane == kk, rank, rank_out)
        gate_out = jnp.where(lane == kk, es[kk] / den, gate_out)
    idx_ref[...] = idx_out
    rank_ref[...] = rank_out
    gate_ref[...] = gate_out
    run_ref[...] = run_ref[...] + jnp.sum(onehot, axis=0, keepdims=True)
    cnt_ref[...] = run_ref[...].astype(I32)


def _router(h, w_router, b_router):
    t, d = h.shape
    e = w_router.shape[1]
    tm = _pick(t, (256, 128, 64, 32, 16, 8))
    wp = jnp.zeros((d, LANES), F32).at[:, :e].set(w_router)
    bp = jnp.full((1, LANES), -1e30, F32).at[0, :e].set(b_router)
    tile = pl.BlockSpec((tm, LANES), lambda i: (i, 0))
    return pl.pallas_call(
        _router_kernel,
        grid=(t // tm,),
        in_specs=[pl.BlockSpec((tm, d), lambda i: (i, 0)),
                  pl.BlockSpec((d, LANES), lambda i: (0, 0)),
                  pl.BlockSpec((1, LANES), lambda i: (0, 0))],
        out_specs=[tile, tile, tile, pl.BlockSpec((1, LANES), lambda i: (0, 0))],
        out_shape=[jax.ShapeDtypeStruct((t, LANES), I32), jax.ShapeDtypeStruct((t, LANES), F32),
                   jax.ShapeDtypeStruct((t, LANES), I32), jax.ShapeDtypeStruct((1, LANES), I32)],
        scratch_shapes=[pltpu.VMEM((1, LANES), F32)],
        compiler_params=_params(("arbitrary",)),
        name="router",
    )(h, wp, bp)


def _row_copy(src, s_row, dst, d_row, sem):
    return pltpu.make_async_copy(src.at[pl.ds(s_row, 1)], dst.at[pl.ds(d_row, 1)], sem)


def _dispatch_kernel(pos_ref, x_ref, rows_in_ref, rows_ref, sem):
    del rows_in_ref
    tm = x_ref.shape[0]
    base = pl.program_id(0) * (tm * TOP_K)

    def issue(t, carry):
        for kk in range(TOP_K):
            _row_copy(x_ref, t, rows_ref, pos_ref[base + t * TOP_K + kk], sem).start()
        return carry

    lax.fori_loop(0, tm, issue, 0, unroll=4)
    n = tm * TOP_K
    pltpu.make_async_copy(rows_ref.at[pl.ds(0, n)], rows_ref.at[pl.ds(0, n)], sem).wait()


def _dispatch(h, pos, n_rows):
    t, d = h.shape
    tm = _pick(t, (256, 128, 64, 32, 16, 8))
    rows0 = jnp.zeros((n_rows, d), h.dtype)
    return pl.pallas_call(
        _dispatch_kernel,
        grid_spec=pltpu.PrefetchScalarGridSpec(
            num_scalar_prefetch=1,
            grid=(t // tm,),
            in_specs=[pl.BlockSpec((tm, d), lambda i, pos: (i, 0)),
                      pl.BlockSpec(memory_space=pl.ANY)],
            out_specs=pl.BlockSpec(memory_space=pl.ANY),
            scratch_shapes=[pltpu.SemaphoreType.DMA(())],
        ),
        out_shape=jax.ShapeDtypeStruct((n_rows, d), h.dtype),
        input_output_aliases={2: 0},
        compiler_params=_params(("arbitrary",)),
        name="dispatch",
    )(pos, h, rows0)


def _swiglu_interleaved(gu):
    lane = lax.broadcasted_iota(I32, (gu.shape[0], LANES), 1)
    even = (lane & 1) == 0
    parts = []
    for c in range(gu.shape[1] // LANES):
        g = gu[:, c * LANES:(c + 1) * LANES]
        gt = jnp.minimum(g, SWIGLU_LIMIT)
        glu = gt * jax.nn.sigmoid(SWIGLU_ALPHA * gt)
        up = jnp.clip(g, -SWIGLU_LIMIT, SWIGLU_LIMIT) + 1.0
        parts.append(jnp.where(even, pltpu.roll(up, LANES - 1, 1) * glu, 0.0))
    outs = [parts[2 * m] + pltpu.roll(parts[2 * m + 1], 1, 1) for m in range(len(parts) // 2)]
    return jnp.concatenate(outs, axis=1) if len(outs) > 1 else outs[0]


def _interleave_rows_bf16(w_ref, out_ref):
    half = LANES // 2
    for m in range(w_ref.shape[0] // LANES):
        lo = w_ref[m * LANES:m * LANES + half, :].astype(BF16).astype(F32)
        hi = w_ref[m * LANES + half:(m + 1) * LANES, :].astype(BF16).astype(F32)
        packed = (lax.bitcast_convert_type(lo, jnp.uint32) >> 16) | (
            lax.bitcast_convert_type(hi, jnp.uint32) & jnp.uint32(0xFFFF0000))
        out_ref[m * LANES:(m + 1) * LANES, :] = pltpu.bitcast(packed, BF16)


def _expert_kernel(ie_ref, ist_ref, inb_ref, rows_ref, wgu_ref, bgu_ref, wd_ref, bd_ref, y_ref,
                   xb_ref, acc_ref, stage_ref, wgub_ref, wdb_ref, ost_ref, sem_in, sem_out, *, tm, n_j):
    del ie_ref
    wi = pl.program_id(0)
    j = pl.program_id(1)
    n_items = pl.num_programs(0)
    nblk = inb_ref[wi]
    half = stage_ref.shape[1]

    def block(b):
        return pl.ds(pl.multiple_of(b * tm, tm), tm)

    def hbm_block(item, b):
        return pl.ds(pl.multiple_of(ist_ref[item] + b * tm, tm), tm)

    def rows_copy(item, b):
        return pltpu.make_async_copy(rows_ref.at[hbm_block(item, b)], stage_ref.at[block(b)], sem_in)

    def y_copy(item, b):
        slot = b & 1
        return pltpu.make_async_copy(ost_ref.at[slot], y_ref.at[hbm_block(item, b)], sem_out.at[slot])

    def drain_stores(item):
        n = inb_ref[item]
        for back in (2, 1):
            @pl.when(n >= back)
            def _():
                y_copy(item, n - back).wait()

    def for_blocks(item, fn):
        def body(b, carry):
            fn(item, b)
            return carry
        lax.fori_loop(0, inb_ref[item], body, 0)

    def unpack(item, b):
        lo, hi = _unpack_halves_bf16(stage_ref[block(b), :])
        xb_ref[block(b), :half] = lo
        xb_ref[block(b), half:] = hi

    @pl.when(j == 0)
    def _():
        @pl.when(wi == 0)
        def _():
            for_blocks(0, lambda it, b: rows_copy(it, b).start())
            acc_ref[...] = jnp.zeros_like(acc_ref)

        @pl.when(wi > 0)
        def _():
            drain_stores(wi - 1)

        for_blocks(wi, lambda it, b: rows_copy(it, b).wait())
        for_blocks(wi, unpack)

        @pl.when(wi + 1 < n_items)
        def _():
            for_blocks(wi + 1, lambda it, b: rows_copy(it, b).start())

    @pl.when(nblk > 0)
    def _():
        wgub_ref[...] = wgu_ref[...].astype(BF16)
        _interleave_rows_bf16(wd_ref, wdb_ref)
        bias = bgu_ref[...]

        def mlp(b):
            gu = jnp.dot(xb_ref[block(b), :], wgub_ref[...], preferred_element_type=F32) + bias
            act = _swiglu_interleaved(gu).astype(BF16)
            return jnp.dot(act, wdb_ref[...], preferred_element_type=F32)

        def accumulate(b):
            prev = jnp.where(j == 0, jnp.broadcast_to(bd_ref[...], (tm, acc_ref.shape[1])), acc_ref[block(b), :])
            acc_ref[block(b), :] = prev + mlp(b)

        def finish(b):
            prev = acc_ref[block(b), :] if n_j > 1 else bd_ref[...]
            ost_ref[b & 1] = _pack_halves_bf16(prev + mlp(b))
            y_copy(wi, b).start()

        def run(fn, stores):
            def pair(i, carry):
                if stores:
                    @pl.when(i > 0)
                    def _():
                        y_copy(wi, 2 * i - 2).wait()
                        y_copy(wi, 2 * i - 1).wait()
                fn(2 * i)
                fn(2 * i + 1)
                return carry
            lax.fori_loop(0, nblk // 2, pair, 0)

            @pl.when((nblk & 1) == 1)
            def _():
                if stores:
                    @pl.when(nblk >= 3)
                    def _():
                        y_copy(wi, nblk - 3).wait()
                fn(nblk - 1)

        if n_j > 1:
            @pl.when(j < n_j - 1)
            def _():
                run(accumulate, False)

        @pl.when(j == n_j - 1)
        def _():
            run(finish, True)

    @pl.when((wi == n_items - 1) & (j == n_j - 1))
    def _():
        drain_stores(wi)
        ost_ref[0] = jnp.zeros(ost_ref.shape[1:], U32)

        def fill(b, carry):
            cp = pltpu.make_async_copy(ost_ref.at[0], y_ref.at[pl.ds(pl.multiple_of(b * tm, tm), tm)],
                                       sem_out.at[0])
            cp.start()
            cp.wait()
            return carry
        lax.fori_loop(ist_ref[n_items] // tm, y_ref.shape[0] // tm, fill, 0)


def _experts(rows, item_e, item_start, item_nblk, w_gate_up, b_gate_up, w_down, b_down, *, tm, xmax):
    n_rows = rows.shape[0]
    n_exp, d, f2 = w_gate_up.shape
    assert rows.shape[1] * 2 == d
    f = f2 // 2
    tf = _pick(f, (MOE_TF, LANES))
    n_j = f // tf
    n_items = item_e.shape[0]

    def jeff(j, inb, wi):
        return jnp.where(inb[wi] > 0, j, n_j - 1)

    kern = functools.partial(_expert_kernel, tm=tm, n_j=n_j)
    return pl.pallas_call(
        kern,
        grid_spec=pltpu.PrefetchScalarGridSpec(
            num_scalar_prefetch=3,
            grid=(n_items, n_j),
            in_specs=[
                pl.BlockSpec(memory_space=pl.ANY),
                pl.BlockSpec((None, d, 2 * tf), lambda wi, j, ie, ist, inb: (ie[wi], 0, jeff(j, inb, wi))),
                pl.BlockSpec((None, 1, 2 * tf), lambda wi, j, ie, ist, inb: (ie[wi], 0, jeff(j, inb, wi))),
                pl.BlockSpec((None, tf, d), lambda wi, j, ie, ist, inb: (ie[wi], jeff(j, inb, wi), 0)),
                pl.BlockSpec((None, 1, d), lambda wi, j, ie, ist, inb: (ie[wi], 0, 0)),
            ],
            out_specs=pl.BlockSpec(memory_space=pl.ANY),
            scratch_shapes=[pltpu.VMEM((xmax, d), BF16),
                            pltpu.VMEM((xmax, d), F32),
                            pltpu.VMEM((xmax, d // 2), U32),
                            pltpu.VMEM((d, 2 * tf), BF16),
                            pltpu.VMEM((tf, d), BF16),
                            pltpu.VMEM((2, tm, d // 2), U32),
                            pltpu.SemaphoreType.DMA(()),
                            pltpu.SemaphoreType.DMA((2,))],
        ),
        out_shape=jax.ShapeDtypeStruct((n_rows, d // 2), U32),
        compiler_params=_params(("arbitrary", "arbitrary")),
        name="experts",
    )(item_e, item_start, item_nblk, rows, w_gate_up, b_gate_up.reshape(n_exp, 1, f2),
      w_down, b_down.reshape(n_exp, 1, d))


def _combine_kernel(pos_ref, y_ref, gate_ref, h_ref, g_ref, b_ref, o_ref, ybuf_ref, sem, *, alpha):
    tm = h_ref.shape[0]
    i = pl.program_id(0)
    slot = i & 1

    def gather(tile, to_slot):
        base = tile * (tm * TOP_K)

        def issue(t, carry):
            for kk in range(TOP_K):
                _row_copy(y_ref, pos_ref[base + t * TOP_K + kk], ybuf_ref.at[to_slot], kk * tm + t,
                          sem.at[to_slot]).start()
            return carry
        lax.fori_loop(0, tm, issue, 0, unroll=4)

    @pl.when(i == 0)
    def _():
        gather(0, 0)

    @pl.when(i + 1 < pl.num_programs(0))
    def _():
        gather(i + 1, 1 - slot)

    n = tm * TOP_K
    pltpu.make_async_copy(y_ref.at[pl.ds(0, n)], ybuf_ref.at[slot], sem.at[slot]).wait()

    gate = gate_ref[...]
    half = y_ref.shape[1]
    ffn_lo = jnp.zeros((tm, half), F32)
    ffn_hi = jnp.zeros((tm, half), F32)
    for kk in range(TOP_K):
        p = ybuf_ref[slot, kk * tm:(kk + 1) * tm, :]
        gk = gate[:, kk:kk + 1]
        ffn_lo = ffn_lo + gk * lax.bitcast_convert_type(p << 16, F32)
        ffn_hi = ffn_hi + gk * lax.bitcast_convert_type(p & jnp.uint32(0xFFFF0000), F32)
    ffn = jnp.concatenate([ffn_lo, ffn_hi], axis=1)
    o_ref[...] = _layer_norm(alpha * h_ref[...] + ffn, g_ref[...], b_ref[...])


def _combine(y, pos, gate, h, ln_g, ln_b, alpha):
    t, d = h.shape
    tm = _pick(t, (128, 64, 32, 16, 8))
    assert y.shape[1] * 2 == d
    return pl.pallas_call(
        functools.partial(_combine_kernel, alpha=alpha),
        grid_spec=pltpu.PrefetchScalarGridSpec(
            num_scalar_prefetch=1,
            grid=(t // tm,),
            in_specs=[pl.BlockSpec(memory_space=pl.ANY),
                      pl.BlockSpec((tm, LANES), lambda i, pos: (i, 0)),
                      pl.BlockSpec((tm, d), lambda i, pos: (i, 0)),
                      pl.BlockSpec((1, d), lambda i, pos: (0, 0)),
                      pl.BlockSpec((1, d), lambda i, pos: (0, 0))],
            out_specs=pl.BlockSpec((tm, d), lambda i, pos: (i, 0)),
            scratch_shapes=[pltpu.VMEM((2, TOP_K * tm, d // 2), U32), pltpu.SemaphoreType.DMA((2,))],
        ),
        out_shape=jax.ShapeDtypeStruct((t, d), F32),
        compiler_params=_params(("arbitrary",)),
        name="combine",
    )(pos, y, gate, h, ln_g.reshape(1, d), ln_b.reshape(1, d))


def _moe_plan(idx, rank, counts, *, tm, xmax, n_items):
    n_exp = counts.shape[0]
    padded = (counts + tm - 1) // tm * tm
    pstart = jnp.cumsum(padded) - padded
    pos = (pstart[idx] + rank).reshape(-1).astype(I32)
    per_e = (padded + xmax - 1) // xmax
    cum = jnp.cumsum(per_e)
    total = cum[-1]
    wi = jnp.arange(n_items, dtype=I32)
    valid = wi < total
    e_w = jnp.minimum(jnp.searchsorted(cum, jnp.minimum(wi, total - 1), side="right"), n_exp - 1).astype(I32)
    local = jnp.minimum(wi, total - 1) - (cum - per_e)[e_w]
    start = (pstart[e_w] + local * xmax).astype(I32)
    nblk = jnp.where(valid, jnp.clip(padded[e_w] - local * xmax, 0, xmax) // tm, 0).astype(I32)
    start = jnp.concatenate([start, jnp.sum(padded, keepdims=True).astype(I32)])
    return pos, e_w, start, nblk


def _layer(h0, h0b, p, *, bsz, seq, alpha):
    t, d = h0.shape
    w_in = p["w_in"]
    hv = p["gdn_a_log"].shape[0]
    dk = p["gdn_norm_w"].shape[0]
    qkv_dim = p["gdn_conv_w"].shape[1]
    v_dim = hv * dk
    hq = (qkv_dim - v_dim) // (2 * dk)
    scw = p["sc_conv_w"].shape[1]
    assert 2 * hv <= LANES and hv % hq == 0
    n_a = qkv_dim + v_dim
    off_ba = n_a
    off_b = n_a + 2 * hv
    n_b = 3 * scw + 2 * d
    assert w_in.shape[1] == off_b + n_b

    tm = _pick(t, (1024, 512, 256, 128))
    tn_a = _pick(n_a, (1024, 512, 256, 128))
    (proj_a,) = _matmul(h0b, w_in, col0=0, n_cols=n_a, tm=tm, tn=tn_a, out_dtypes=(BF16,))
    tn_b = _pick(n_b, (512, 256, 128))
    (proj_b,) = _matmul(h0b, w_in, col0=off_b, n_cols=n_b, tm=tm, tn=tn_b, out_dtypes=(BF16,))
    assert off_ba + LANES <= w_in.shape[1]
    (ba,) = _matmul(h0b, w_in, col0=off_ba, n_cols=LANES, tm=tm, tn=LANES, out_dtypes=(F32,))

    gcol, grow = _gdn_gates(ba, p["gdn_a_log"], p["gdn_dt_bias"], hv)
    o_n = _gdn(proj_a, p["gdn_conv_w"], gcol, grow, p["gdn_norm_w"], bsz=bsz, seq=seq, hq=hq, hv=hv, dk=dk)

    u = _short_conv(proj_b, p["sc_conv_w"], bsz=bsz, seq=seq, width=scw)

    tn = _pick(d, (512, 256, 128))
    ga0 = 3 * scw // tn
    gb0 = (3 * scw + d) // tn
    (part_a,) = _matmul(
        o_n, p["w_out_gdn"], col0=0, n_cols=d, tm=tm, tn=tn, out_dtypes=(F32,),
        epilogue=lambda acc, ga: (jax.nn.sigmoid(ga.astype(F32)) * acc,),
        extras=[(proj_b, (tm, tn), lambda j, i: (i, ga0 + j))])
    (merged,) = _matmul(
        u, p["w_out_sc"], col0=0, n_cols=d, tm=tm, tn=tn, out_dtypes=(BF16,),
        epilogue=lambda acc, gb, pa: (pa + jax.nn.sigmoid(gb.astype(F32)) * acc,),
        extras=[(proj_b, (tm, tn), lambda j, i: (i, gb0 + j)),
                (part_a, (tm, tn), lambda j, i: (i, j))])
    tm3 = _pick(t, (256, 128))
    def mix_epilogue(acc, hh, g, b):
        hn = _layer_norm(alpha * hh + acc, g, b)
        return hn, _pack_halves_bf16(hn)

    h1, h1p = _matmul(
        merged, p["w_out"].astype(BF16), col0=0, n_cols=d, tm=tm3, tn=d, out_dtypes=(F32, U32), out_div=(1, 2),
        epilogue=mix_epilogue,
        extras=[(h0, (tm3, d), lambda j, i: (i, 0)),
                (p["ln_mix_g"].reshape(1, d), (1, d), lambda j, i: (0, 0)),
                (p["ln_mix_b"].reshape(1, d), (1, d), lambda j, i: (0, 0))])

    n_exp = p["w_router"].shape[1]
    idx, gate, rank, cnt = _router(h1, p["w_router"], p["b_router"])
    mtm = MOE_TM
    xmax = MOE_XMAX
    n_rows = t * TOP_K + n_exp * mtm
    n_items = n_exp + n_rows // xmax
    pos, item_e, item_start, item_nblk = _moe_plan(
        idx[:, :TOP_K], rank[:, :TOP_K], cnt[0, :n_exp], tm=mtm, xmax=xmax, n_items=n_items)
    rows = _dispatch(h1p, pos, n_rows)
    y = _experts(rows, item_e, item_start, item_nblk, p["w_gate_up"], p["b_gate_up"], p["w_down"], p["b_down"],
                 tm=mtm, xmax=xmax)
    h2 = _combine(y, pos, gate, h1, p["ln_ffn_g"], p["ln_ffn_b"], alpha)
    return h2


_LAYER_PARAMS = ("w_in", "gdn_conv_w", "gdn_a_log", "gdn_dt_bias", "gdn_norm_w", "w_out_gdn", "sc_conv_w",
                 "w_out_sc", "w_out", "ln_mix_g", "ln_mix_b", "w_router", "b_router", "w_gate_up", "b_gate_up",
                 "w_down", "b_down", "ln_ffn_g", "ln_ffn_b")


def kernel(x, ln_in_g, ln_in_b, w_in, gdn_conv_w, gdn_a_log, gdn_dt_bias, gdn_norm_w, w_out_gdn, sc_conv_w,
           w_out_sc, w_out, ln_mix_g, ln_mix_b, w_router, b_router, w_gate_up, b_gate_up, w_down, b_down,
           ln_ffn_g, ln_ffn_b):
    stacked = dict(zip(_LAYER_PARAMS, (w_in, gdn_conv_w, gdn_a_log, gdn_dt_bias, gdn_norm_w, w_out_gdn,
                                       sc_conv_w, w_out_sc, w_out, ln_mix_g, ln_mix_b, w_router, b_router,
                                       w_gate_up, b_gate_up, w_down, b_down, ln_ffn_g, ln_ffn_b)))
    bsz, seq, d = x.shape
    depth = w_in.shape[0]
    alpha = (2 * depth) ** 0.25
    h, hb = _ln_in(x.reshape(bsz * seq, d), ln_in_g, ln_in_b)
    for l in range(depth):
        p = {name: arr[l] for name, arr in stacked.items()}
        h = _layer(h, hb, p, bsz=bsz, seq=seq, alpha=alpha)
        if l + 1 < depth:
            hb = h.astype(BF16)
    return h.reshape(bsz, seq, d)
```

```python
import functools

import jax
import jax.numpy as jnp
from jax import lax
from jax.experimental import pallas as pl
from jax.experimental.pallas import tpu as pltpu

F32 = jnp.float32
BF16 = jnp.bfloat16
I32 = jnp.int32
U32 = jnp.uint32

LANES = 128
SUBLANES = 8
VMEM_LIMIT = 60 << 20

TOP_K = 4
SWIGLU_LIMIT = 7.0
SWIGLU_ALPHA = 1.702
LN_EPS = 1e-5
RMS_EPS = 1e-6
GDN_CHUNK = 64
GDN_STEP = 256
GDN_QK_GROUP = 4
GDN_INV_GROUP = 16
MOE_TM = 256
MOE_XMAX = 1280
MOE_TF = 512


def _params(sem):
    return pltpu.CompilerParams(dimension_semantics=sem, vmem_limit_bytes=VMEM_LIMIT)


def _pick(n, candidates):
    for c in candidates:
        if n % c == 0:
            return c
    raise ValueError(f"no tile for {n} in {candidates}")


def _layer_norm(xf, g, b):
    mu = jnp.mean(xf, axis=-1, keepdims=True)
    xc = xf - mu
    var = jnp.mean(xc * xc, axis=-1, keepdims=True)
    return xc * lax.rsqrt(var + LN_EPS) * g + b


def _pack_halves_bf16(x):
    half = x.shape[1] // 2
    lo = lax.bitcast_convert_type(x[:, :half].astype(BF16).astype(F32), U32) >> 16
    hi = lax.bitcast_convert_type(x[:, half:].astype(BF16).astype(F32), U32) & jnp.uint32(0xFFFF0000)
    return lo | hi


def _unpack_halves_bf16(p):
    lo = lax.bitcast_convert_type(p << 16, F32).astype(BF16)
    hi = lax.bitcast_convert_type(p & jnp.uint32(0xFFFF0000), F32).astype(BF16)
    return lo, hi


def _ln_in_kernel(x_ref, g_ref, b_ref, h_ref, hb_ref):
    h = _layer_norm(x_ref[...], g_ref[...], b_ref[...])
    h_ref[...] = h
    hb_ref[...] = h.astype(BF16)


def _ln_in(x2, g, b):
    t, d = x2.shape
    tm = _pick(t, (512, 256, 128, 64, 32, 16))
    return pl.pallas_call(
        _ln_in_kernel,
        grid=(t // tm,),
        in_specs=[pl.BlockSpec((tm, d), lambda i: (i, 0)),
                  pl.BlockSpec((1, d), lambda i: (0, 0)),
                  pl.BlockSpec((1, d), lambda i: (0, 0))],
        out_specs=[pl.BlockSpec((tm, d), lambda i: (i, 0)),
                   pl.BlockSpec((tm, d), lambda i: (i, 0))],
        out_shape=[jax.ShapeDtypeStruct((t, d), F32), jax.ShapeDtypeStruct((t, d), BF16)],
        compiler_params=_params(("arbitrary",)),
        name="ln_in",
    )(x2, g.reshape(1, d), b.reshape(1, d))


def _mm_kernel(*refs, shift, w_t, n_extra, n_out, epilogue):
    n_w = 2 if shift else 1
    x_ref, w_refs = refs[0], refs[1:1 + n_w]
    extras = refs[1 + n_w:1 + n_w + n_extra]
    o_refs = refs[1 + n_w + n_extra:1 + n_w + n_extra + n_out]
    wb_ref = refs[-1]
    n_axis = 0 if w_t else 1

    @pl.when(pl.program_id(1) == 0)
    def _():
        if shift:
            tn = w_refs[0].shape[n_axis]
            w = jnp.concatenate([lax.slice_in_dim(w_refs[0][...], shift, tn, axis=n_axis),
                                 lax.slice_in_dim(w_refs[1][...], 0, shift, axis=n_axis)], axis=n_axis)
        else:
            w = w_refs[0][...]
        wb_ref[...] = w.astype(BF16)

    acc = lax.dot_general(x_ref[...], wb_ref[...], (((1,), (1 if w_t else 0,)), ((), ())),
                          preferred_element_type=F32)
    outs = epilogue(acc, *[e[...] for e in extras])
    for o_ref, o in zip(o_refs, outs):
        o_ref[...] = o.astype(o_ref.dtype)


def _matmul(x, w, *, col0, n_cols, tm, tn, out_dtypes, epilogue=None, extras=(), out_div=None, w_t=False):
    t, k = x.shape
    k_axis, n_axis = (1, 0) if w_t else (0, 1)
    assert w.shape[k_axis] == k and n_cols % tn == 0 and t % tm == 0 and col0 + n_cols <= w.shape[n_axis]
    j0, shift = divmod(col0, tn)
    if epilogue is None:
        epilogue = lambda acc: (acc,)
    if out_div is None:
        out_div = (1,) * len(out_dtypes)
    kern = functools.partial(_mm_kernel, shift=shift, w_t=w_t, n_extra=len(extras), n_out=len(out_dtypes),
                             epilogue=epilogue)
    w_block = (tn, k) if w_t else (k, tn)
    w_index = lambda jj: (jj, 0) if w_t else (0, jj)
    in_specs = [pl.BlockSpec((tm, k), lambda j, i: (i, 0)),
                pl.BlockSpec(w_block, lambda j, i: w_index(j + j0))]
    ws = [w]
    if shift:
        in_specs.append(pl.BlockSpec(w_block, lambda j, i: w_index(j + j0 + 1)))
        ws.append(w)
    in_specs += [pl.BlockSpec(bs, im) for (_, bs, im) in extras]
    return pl.pallas_call(
        kern,
        grid=(n_cols // tn, t // tm),
        in_specs=in_specs,
        out_specs=[pl.BlockSpec((tm, tn // dv), lambda j, i: (i, j)) for dv in out_div],
        out_shape=[jax.ShapeDtypeStruct((t, n_cols // dv), dt) for dt, dv in zip(out_dtypes, out_div)],
        scratch_shapes=[pltpu.VMEM(w_block, BF16)],
        compiler_params=_params(("arbitrary", "arbitrary")),
        name="matmul",
    )(x, *ws, *[a for (a, _, _) in extras])


def _gates_kernel(ba_ref, alog_ref, dt_ref, col_ref, row_ref, *, hv):
    x = ba_ref[...]
    lane = lax.broadcasted_iota(I32, x.shape, 1)
    beta = jax.nn.sigmoid(x)
    xs = x + dt_ref[...]
    softplus = jnp.maximum(xs, 0.0) + jnp.log(1.0 + jnp.exp(-jnp.abs(xs)))
    g = -jnp.exp(alog_ref[...]) * softplus
    g = jnp.where((lane >= hv) & (lane < 2 * hv), g, 0.0)
    n = x.shape[0]
    row = lax.broadcasted_iota(I32, x.shape, 0) & (GDN_CHUNK - 1)
    cum = g
    rev = g
    sh = 1
    while sh < GDN_CHUNK:
        cum = cum + jnp.where(row >= sh, pltpu.roll(cum, sh, 0), 0.0)
        rev = rev + jnp.where(row < GDN_CHUNK - sh, pltpu.roll(rev, n - sh, 0), 0.0)
        sh *= 2
    e_cum = pltpu.roll(jnp.exp(cum), hv, 1)
    e_rest = pltpu.roll(jnp.exp(rev - g), 2 * hv, 1)
    out = jnp.where(lane < hv, beta, jnp.where(lane < 2 * hv, cum, jnp.where(lane < 3 * hv, e_cum, e_rest)))
    col_ref[...] = out
    row_ref[...] = out.T


def _gdn_gates(ba, a_log, dt_bias, hv):
    assert 4 * hv <= LANES
    t = ba.shape[0]
    tm = _pick(t, (512, 256, 128))
    pad = lambda v: jnp.zeros((1, LANES), F32).at[0, hv:2 * hv].set(v.astype(F32))
    return pl.pallas_call(
        functools.partial(_gates_kernel, hv=hv),
        grid=(t // tm,),
        in_specs=[pl.BlockSpec((tm, LANES), lambda i: (i, 0)),
                  pl.BlockSpec((1, LANES), lambda i: (0, 0)),
                  pl.BlockSpec((1, LANES), lambda i: (0, 0))],
        out_specs=[pl.BlockSpec((tm, LANES), lambda i: (i, 0)),
                   pl.BlockSpec((LANES, tm), lambda i: (0, i))],
        out_shape=[jax.ShapeDtypeStruct((t, LANES), F32), jax.ShapeDtypeStruct((LANES, t), F32)],
        compiler_params=_params(("arbitrary",)),
        name="gdn_gates",
    )(ba, pad(a_log), pad(dt_bias))


def _causal_conv(x, w, ext_ref):
    n = x.shape[0]
    kw = w.shape[0]
    ext_ref[SUBLANES:, :] = x
    acc = x * w[kw - 1:kw]
    for d in range(1, kw):
        acc = acc + ext_ref[SUBLANES - d:SUBLANES - d + n, :] * w[kw - 1 - d:kw - d]
    ext_ref[:SUBLANES, :] = x[n - SUBLANES:]
    return acc


def _silu(x):
    h = 0.5 * x
    return h + h * jnp.tanh(h)


def _unit_lower_inverses(lows, ii, jj):
    c = lows[0].shape[0]
    eye = jnp.where(ii == jj, 1.0, 0.0)
    pair = (ii >> 1) == (jj >> 1)
    xbs = [(eye - jnp.where(pair, low, 0.0)).astype(BF16) for low in lows]
    lbs = [low.astype(BF16) for low in lows]
    zero = jnp.zeros((c, c), BF16)
    s = 1
    while (2 << s) <= c:
        m = ((ii >> (s + 1)) == (jj >> (s + 1))) & ((ii >> s) != (jj >> s))
        ts = [jnp.dot(jnp.where(m, lb, zero), xb, preferred_element_type=F32) for lb, xb in zip(lbs, xbs)]
        xbs = [jnp.dot(xb, (eye - t).astype(BF16), preferred_element_type=F32).astype(BF16)
               for xb, t in zip(xbs, ts)]
        s += 1
    return xbs


def _gdn_kernel(q_ref, k_ref, v_ref, z_ref, cwq_ref, cwk_ref, cwv_ref, gcol_ref, grow_ref, nw_ref,
                o_ref, s_ref, tq_ref, tk_ref, tv_ref, qn_ref, kn_ref, vn_ref, u_ref, wq_ref, attn_ref, kd_ref,
                *, nq, vper, dk):
    @pl.when(pl.program_id(2) == 0)
    def _():
        s_ref[...] = jnp.zeros_like(s_ref)
        for t_ref in (tq_ref, tk_ref, tv_ref):
            t_ref[:SUBLANES, :] = jnp.zeros((SUBLANES, t_ref.shape[1]), F32)

    tb = q_ref.shape[0]
    c = GDN_CHUNK
    nc = tb // c
    nv = nq * vper
    head = lambda h: slice(h * dk, (h + 1) * dk)

    q = _silu(_causal_conv(q_ref[...].astype(F32), cwq_ref[...], tq_ref))
    k = _silu(_causal_conv(k_ref[...].astype(F32), cwk_ref[...], tk_ref))
    for hq in range(nq):
        qh = q[:, head(hq)]
        kh = k[:, head(hq)]
        qn_ref[:, head(hq)] = qh * (lax.rsqrt(jnp.sum(qh * qh, axis=-1, keepdims=True) + RMS_EPS) * (dk ** -0.5))
        kn_ref[:, head(hq)] = kh * lax.rsqrt(jnp.sum(kh * kh, axis=-1, keepdims=True) + RMS_EPS)

    cols = gcol_ref[...]
    grows = grow_ref[...]
    nw = nw_ref[...]
    ii = lax.broadcasted_iota(I32, (c, c), 0)
    jj = lax.broadcasted_iota(I32, (c, c), 1)
    incl = ii >= jj
    strict = ii > jj

    def col(ci, which, h):
        return cols[ci * c:(ci + 1) * c, which * nv + h:which * nv + h + 1]

    def state_step(ci):
        rows = slice(ci * c, (ci + 1) * c)
        ps = [ci * nv + h for h in range(nv)]
        r1 = [jnp.dot(wq_ref[p], s_ref[h].astype(BF16), preferred_element_type=F32) for h, p in enumerate(ps)]
        v_new = [(u_ref[p] - r[:c]).astype(BF16) for p, r in zip(ps, r1)]
        outs = [r[c:] + jnp.dot(attn_ref[p], vn, preferred_element_type=F32) for p, r, vn in zip(ps, r1, v_new)]
        for h, (p, vn) in enumerate(zip(ps, v_new)):
            g_tot = cols[(ci + 1) * c - 1:(ci + 1) * c, 2 * nv + h:2 * nv + h + 1]
            s_ref[h] = s_ref[h] * g_tot + lax.dot_general(
                kd_ref[p], vn, (((0,), (0,)), ((), ())), preferred_element_type=F32)
        for h, o in enumerate(outs):
            zc = z_ref[rows, head(h)].astype(F32)
            var = jnp.mean(o * o, axis=-1, keepdims=True)
            o_ref[rows, head(h)] = (o * lax.rsqrt(var + RMS_EPS) * nw * _silu(zc)).astype(BF16)

    problems = [(ci, h) for ci in range(nc) for h in range(nv)]
    gram = {}
    chunks_done = 0
    for g0 in range(0, len(problems), GDN_INV_GROUP):
        group = problems[g0:g0 + GDN_INV_GROUP]
        lows = []
        for ci, h in group:
            hq = h // vper
            rows = slice(ci * c, (ci + 1) * c)
            if (ci, hq) not in gram:
                kcb = kn_ref[rows, head(hq)].astype(BF16)
                qcb = qn_ref[rows, head(hq)].astype(BF16)
                gram[(ci, hq)] = lax.dot_general(jnp.concatenate([kcb, qcb], axis=0), kcb,
                                                 (((1,), (1,)), ((), ())), preferred_element_type=F32)
            a = gram[(ci, hq)]
            grow = grows[h:h + 1, rows]
            decay = jnp.where(incl, jnp.exp(jnp.minimum(col(ci, 1, h) - grow, 0.0)), 0.0)
            lows.append(jnp.where(strict, col(ci, 0, h) * a[:c] * decay, 0.0))
            attn_ref[ci * nv + h] = jnp.where(incl, a[c:] * decay, 0.0).astype(BF16)
        if g0 == 0:
            vn_ref[...] = _silu(_causal_conv(v_ref[...].astype(F32), cwv_ref[...], tv_ref))
        xs = _unit_lower_inverses(lows, ii, jj)
        for x, (ci, h) in zip(xs, group):
            hq = h // vper
            rows = slice(ci * c, (ci + 1) * c)
            p = ci * nv + h
            kc = kn_ref[rows, head(hq)]
            bcol = col(ci, 0, h)
            eg = col(ci, 2, h)
            rhs = jnp.concatenate([vn_ref[rows, head(h)] * bcol, kc * (bcol * eg)], axis=1).astype(BF16)
            uw = jnp.dot(x, rhs, preferred_element_type=F32)
            u_ref[p] = uw[:, :dk]
            wq_ref[p, :c, :] = uw[:, dk:].astype(BF16)
            wq_ref[p, c:, :] = (qn_ref[rows, head(hq)] * eg).astype(BF16)
            kd_ref[p] = (kc * col(ci, 3, h)).astype(BF16)
        chunks_ready = (g0 + len(group)) // nv
        for ci in range(chunks_done, chunks_ready):
            state_step(ci)
        chunks_done = chunks_ready


def _gdn(proj_a, conv_w, gcol, grow, norm_w, *, bsz, seq, hq, hv, dk):
    t = bsz * seq
    vper = hv // hq
    nq = _pick(hq, (GDN_QK_GROUP, 2, 1))
    nv = nq * vper
    ng = hq // nq
    tb = _pick(seq, (GDN_STEP, 128, 64))
    ns = seq // tb
    nprob = (tb // GDN_CHUNK) * nv
    qw = nq * dk
    vw = nv * dk
    v_blk0 = 2 * hq * dk // vw
    z_blk0 = (2 * hq * dk + hv * dk) // vw
    kw = conv_w.shape[0]
    gcol = gcol[:, :4 * hv].reshape(t, 4, ng, nv).transpose(2, 0, 1, 3).reshape(ng, t, 4 * nv)
    grow = grow[hv:2 * hv].reshape(ng, nv, t)
    rowblk = lambda b, g, s: b * ns + s
    kern = functools.partial(_gdn_kernel, nq=nq, vper=vper, dk=dk)
    return pl.pallas_call(
        kern,
        grid=(bsz, ng, ns),
        in_specs=[
            pl.BlockSpec((tb, qw), lambda b, g, s: (rowblk(b, g, s), g)),
            pl.BlockSpec((tb, qw), lambda b, g, s: (rowblk(b, g, s), ng + g)),
            pl.BlockSpec((tb, vw), lambda b, g, s: (rowblk(b, g, s), v_blk0 + g)),
            pl.BlockSpec((tb, vw), lambda b, g, s: (rowblk(b, g, s), z_blk0 + g)),
            pl.BlockSpec((kw, qw), lambda b, g, s: (0, g)),
            pl.BlockSpec((kw, qw), lambda b, g, s: (0, ng + g)),
            pl.BlockSpec((kw, vw), lambda b, g, s: (0, v_blk0 + g)),
            pl.BlockSpec((None, tb, 4 * nv), lambda b, g, s: (g, rowblk(b, g, s), 0)),
            pl.BlockSpec((None, nv, tb), lambda b, g, s: (g, 0, rowblk(b, g, s))),
            pl.BlockSpec((1, dk), lambda b, g, s: (0, 0)),
        ],
        out_specs=pl.BlockSpec((tb, vw), lambda b, g, s: (rowblk(b, g, s), g)),
        out_shape=jax.ShapeDtypeStruct((t, hv * dk), BF16),
        scratch_shapes=[pltpu.VMEM((nv, dk, dk), F32),
                        pltpu.VMEM((tb + SUBLANES, qw), F32),
                        pltpu.VMEM((tb + SUBLANES, qw), F32),
                        pltpu.VMEM((tb + SUBLANES, vw), F32),
                        pltpu.VMEM((tb, qw), F32),
                        pltpu.VMEM((tb, qw), F32),
                        pltpu.VMEM((tb, vw), F32),
                        pltpu.VMEM((nprob, GDN_CHUNK, dk), F32),
                        pltpu.VMEM((nprob, 2 * GDN_CHUNK, dk), BF16),
                        pltpu.VMEM((nprob, GDN_CHUNK, GDN_CHUNK), BF16),
                        pltpu.VMEM((nprob, GDN_CHUNK, dk), BF16)],
        compiler_params=_params(("arbitrary", "arbitrary", "arbitrary")),
        name="gdn",
    )(proj_a, proj_a, proj_a, proj_a, conv_w, conv_w, conv_w, gcol, grow, norm_w.reshape(1, dk))


def _sc_kernel(b_ref, c_ref, x_ref, w_ref, u_ref, tail_ref):
    @pl.when(pl.program_id(1) == 0)
    def _():
        tail_ref[:SUBLANES, :] = jnp.zeros((SUBLANES, tail_ref.shape[1]), F32)

    p = c_ref[...].astype(F32) * x_ref[...].astype(F32)
    u_ref[...] = (b_ref[...].astype(F32) * _causal_conv(p, w_ref[...], tail_ref)).astype(BF16)


def _short_conv(proj_b, conv_w, *, bsz, seq, width):
    t = bsz * seq
    ts = _pick(seq, (512, 256, 128, 64))
    ns = seq // ts
    kw = conv_w.shape[0]
    return pl.pallas_call(
        _sc_kernel,
        grid=(bsz, ns),
        in_specs=[pl.BlockSpec((ts, width), lambda b, s: (b * ns + s, 0)),
                  pl.BlockSpec((ts, width), lambda b, s: (b * ns + s, 1)),
                  pl.BlockSpec((ts, width), lambda b, s: (b * ns + s, 2)),
                  pl.BlockSpec((kw, width), lambda b, s: (0, 0))],
        out_specs=pl.BlockSpec((ts, width), lambda b, s: (b * ns + s, 0)),
        out_shape=jax.ShapeDtypeStruct((t, width), BF16),
        scratch_shapes=[pltpu.VMEM((ts + SUBLANES, width), F32)],
        compiler_params=_params(("arbitrary", "arbitrary")),
        name="short_conv",
    )(proj_b, proj_b, proj_b, conv_w)


def _router_kernel(h_ref, w_ref, b_ref, idx_ref, gate_ref, rank_ref, cnt_ref, run_ref):
    @pl.when(pl.program_id(0) == 0)
    def _():
        run_ref[...] = jnp.zeros_like(run_ref)

    tm = h_ref.shape[0]
    logits = jnp.dot(h_ref[...], w_ref[...], preferred_element_type=F32,
                     precision=lax.Precision.HIGHEST) + b_ref[...]
    lane = lax.broadcasted_iota(I32, logits.shape, 1)
    lane_f = lane.astype(F32)
    cur = logits
    idxs, vals = [], []
    for _ in range(TOP_K):
        m = jnp.max(cur, axis=-1, keepdims=True)
        ix = jnp.min(jnp.where(cur == m, lane_f, float(LANES)), axis=-1, keepdims=True).astype(I32)
        idxs.append(ix)
        vals.append(m)
        cur = jnp.where(lane == ix, -jnp.inf, cur)
    es = [jnp.exp(val - vals[0]) for val in vals]
    den = es[0]
    for e in es[1:]:
        den = den + e
    onehot = jnp.zeros(logits.shape, F32)
    for ix in idxs:
        onehot = onehot + jnp.where(lane == ix, 1.0, 0.0)
    ri = lax.broadcasted_iota(I32, (tm, tm), 0)
    ci = lax.broadcasted_iota(I32, (tm, tm), 1)
    tri = jnp.where(ri > ci, 1.0, 0.0).astype(BF16)
    before = jnp.dot(tri, onehot.astype(BF16), preferred_element_type=F32) + run_ref[...]
    idx_out = jnp.zeros(logits.shape, I32)
    rank_out = jnp.zeros(logits.shape, I32)
    gate_out = jnp.zeros(logits.shape, F32)
    for kk in range(TOP_K):
        rank = jnp.sum(jnp.where(lane == idxs[kk], before, 0.0), axis=-1, keepdims=True).astype(I32)
        idx_out = jnp.where(lane == kk, idxs[kk], idx_out)
        rank_out = jnp.where(lane == kk, rank, rank_out)
        gate_out = jnp.where(lane == kk, es[kk] / den, gate_out)
    idx_ref[...] = idx_out
    rank_ref[...] = rank_out
    gate_ref[...] = gate_out
    run_ref[...] = run_ref[...] + jnp.sum(onehot, axis=0, keepdims=True)
    cnt_ref[...] = run_ref[...].astype(I32)


def _router(h, w_router, b_router):
    t, d = h.shape
    e = w_router.shape[1]
    tm = _pick(t, (256, 128, 64, 32, 16, 8))
    wp = jnp.zeros((d, LANES), F32).at[:, :e].set(w_router)
    bp = jnp.full((1, LANES), -1e30, F32).at[0, :e].set(b_router)
    tile = pl.BlockSpec((tm, LANES), lambda i: (i, 0))
    return pl.pallas_call(
        _router_kernel,
        grid=(t // tm,),
        in_specs=[pl.BlockSpec((tm, d), lambda i: (i, 0)),
                  pl.BlockSpec((d, LANES), lambda i: (0, 0)),
                  pl.BlockSpec((1, LANES), lambda i: (0, 0))],
        out_specs=[tile, tile, tile, pl.BlockSpec((1, LANES), lambda i: (0, 0))],
        out_shape=[jax.ShapeDtypeStruct((t, LANES), I32), jax.ShapeDtypeStruct((t, LANES), F32),
                   jax.ShapeDtypeStruct((t, LANES), I32), jax.ShapeDtypeStruct((1, LANES), I32)],
        scratch_shapes=[pltpu.VMEM((1, LANES), F32)],
        compiler_params=_params(("arbitrary",)),
        name="router",
    )(h, wp, bp)


def _row_copy(src, s_row, dst, d_row, sem):
    return pltpu.make_async_copy(src.at[pl.ds(s_row, 1)], dst.at[pl.ds(d_row, 1)], sem)


def _dispatch_kernel(pos_ref, x_ref, rows_in_ref, rows_ref, sem):
    del rows_in_ref
    tm = x_ref.shape[0]
    base = pl.program_id(0) * (tm * TOP_K)

    for t in range(tm):
        for kk in range(TOP_K):
            _row_copy(x_ref, t, rows_ref, pos_ref[base + (t * TOP_K + kk)], sem).start()
    n = tm * TOP_K
    pltpu.make_async_copy(rows_ref.at[pl.ds(0, n)], rows_ref.at[pl.ds(0, n)], sem).wait()


def _dispatch(h, pos, n_rows):
    t, d = h.shape
    tm = _pick(t, (128, 64, 32, 16, 8))
    rows0 = jnp.zeros((n_rows, d), h.dtype)
    return pl.pallas_call(
        _dispatch_kernel,
        grid_spec=pltpu.PrefetchScalarGridSpec(
            num_scalar_prefetch=1,
            grid=(t // tm,),
            in_specs=[pl.BlockSpec((tm, d), lambda i, pos: (i, 0)),
                      pl.BlockSpec(memory_space=pl.ANY)],
            out_specs=pl.BlockSpec(memory_space=pl.ANY),
            scratch_shapes=[pltpu.SemaphoreType.DMA(())],
        ),
        out_shape=jax.ShapeDtypeStruct((n_rows, d), h.dtype),
        input_output_aliases={2: 0},
        compiler_params=_params(("arbitrary",)),
        name="dispatch",
    )(pos, h, rows0)


def _swiglu_interleaved(gu):
    lane = lax.broadcasted_iota(I32, (gu.shape[0], LANES), 1)
    even = (lane & 1) == 0
    parts = []
    for c in range(gu.shape[1] // LANES):
        g = gu[:, c * LANES:(c + 1) * LANES]
        gt = jnp.minimum(g, SWIGLU_LIMIT)
        glu = gt * jax.nn.sigmoid(SWIGLU_ALPHA * gt)
        up = jnp.clip(g, -SWIGLU_LIMIT, SWIGLU_LIMIT) + 1.0
        parts.append(jnp.where(even, pltpu.roll(up, LANES - 1, 1) * glu, 0.0))
    outs = [parts[2 * m] + pltpu.roll(parts[2 * m + 1], 1, 1) for m in range(len(parts) // 2)]
    return jnp.concatenate(outs, axis=1) if len(outs) > 1 else outs[0]


def _interleave_rows_bf16(w_ref, out_ref):
    half = LANES // 2
    for m in range(w_ref.shape[0] // LANES):
        lo = w_ref[m * LANES:m * LANES + half, :].astype(BF16).astype(F32)
        hi = w_ref[m * LANES + half:(m + 1) * LANES, :].astype(BF16).astype(F32)
        packed = (lax.bitcast_convert_type(lo, jnp.uint32) >> 16) | (
            lax.bitcast_convert_type(hi, jnp.uint32) & jnp.uint32(0xFFFF0000))
        out_ref[m * LANES:(m + 1) * LANES, :] = pltpu.bitcast(packed, BF16)


def _expert_kernel(ie_ref, ist_ref, inb_ref, rows_ref, wgu_ref, bgu_ref, wd_ref, bd_ref, y_ref,
                   xb_ref, acc_ref, stage_ref, wgub_ref, wdb_ref, ost_ref, sem_in, sem_out, *, tm, n_j):
    del ie_ref
    wi = pl.program_id(0)
    j = pl.program_id(1)
    n_items = pl.num_programs(0)
    nblk = inb_ref[wi]
    half = stage_ref.shape[1]

    def block(b):
        return pl.ds(pl.multiple_of(b * tm, tm), tm)

    def hbm_block(item, b):
        return pl.ds(pl.multiple_of(ist_ref[item] + b * tm, tm), tm)

    def rows_copy(item, b):
        return pltpu.make_async_copy(rows_ref.at[hbm_block(item, b)], stage_ref.at[block(b)], sem_in)

    def y_copy(item, b):
        slot = b & 1
        return pltpu.make_async_copy(ost_ref.at[slot], y_ref.at[hbm_block(item, b)], sem_out.at[slot])

    def drain_stores(item):
        n = inb_ref[item]
        for back in (2, 1):
            @pl.when(n >= back)
            def _():
                y_copy(item, n - back).wait()

    def for_blocks(item, fn):
        def body(b, carry):
            fn(item, b)
            return carry
        lax.fori_loop(0, inb_ref[item], body, 0)

    def unpack(item, b):
        lo, hi = _unpack_halves_bf16(stage_ref[block(b), :])
        xb_ref[block(b), :half] = lo
        xb_ref[block(b), half:] = hi

    @pl.when(j == 0)
    def _():
        @pl.when(wi == 0)
        def _():
            for_blocks(0, lambda it, b: rows_copy(it, b).start())
            acc_ref[...] = jnp.zeros_like(acc_ref)

        @pl.when(wi > 0)
        def _():
            drain_stores(wi - 1)

        for_blocks(wi, lambda it, b: rows_copy(it, b).wait())
        for_blocks(wi, unpack)

        @pl.when(wi + 1 < n_items)
        def _():
            for_blocks(wi + 1, lambda it, b: rows_copy(it, b).start())

    @pl.when(nblk > 0)
    def _():
        wgub_ref[...] = wgu_ref[...].astype(BF16)
        _interleave_rows_bf16(wd_ref, wdb_ref)
        bias = bgu_ref[...]

        def mlp(b):
            gu = jnp.dot(xb_ref[block(b), :], wgub_ref[...], preferred_element_type=F32) + bias
            act = _swiglu_interleaved(gu).astype(BF16)
            return jnp.dot(act, wdb_ref[...], preferred_element_type=F32)

        def accumulate(b):
            prev = jnp.where(j == 0, jnp.broadcast_to(bd_ref[...], (tm, acc_ref.shape[1])), acc_ref[block(b), :])
            acc_ref[block(b), :] = prev + mlp(b)

        def finish(b):
            prev = acc_ref[block(b), :] if n_j > 1 else bd_ref[...]
            ost_ref[b & 1] = _pack_halves_bf16(prev + mlp(b))
            y_copy(wi, b).start()

        def run(fn, stores):
            def pair(i, carry):
                if stores:
                    @pl.when(i > 0)
                    def _():
                        y_copy(wi, 2 * i - 2).wait()
                        y_copy(wi, 2 * i - 1).wait()
                fn(2 * i)
                fn(2 * i + 1)
                return carry
            lax.fori_loop(0, nblk // 2, pair, 0)

            @pl.when((nblk & 1) == 1)
            def _():
                if stores:
                    @pl.when(nblk >= 3)
                    def _():
                        y_copy(wi, nblk - 3).wait()
                fn(nblk - 1)

        if n_j > 1:
            @pl.when(j < n_j - 1)
            def _():
                run(accumulate, False)

        @pl.when(j == n_j - 1)
        def _():
            run(finish, True)

    @pl.when((wi == n_items - 1) & (j == n_j - 1))
    def _():
        drain_stores(wi)
        ost_ref[0] = jnp.zeros(ost_ref.shape[1:], U32)

        def fill(b, carry):
            cp = pltpu.make_async_copy(ost_ref.at[0], y_ref.at[pl.ds(pl.multiple_of(b * tm, tm), tm)],
                                       sem_out.at[0])
            cp.start()
            cp.wait()
            return carry
        lax.fori_loop(ist_ref[n_items] // tm, y_ref.shape[0] // tm, fill, 0)


def _experts(rows, item_e, item_start, item_nblk, w_gate_up, b_gate_up, w_down, b_down, *, tm, xmax):
    n_rows = rows.shape[0]
    n_exp, d, f2 = w_gate_up.shape
    assert rows.shape[1] * 2 == d
    f = f2 // 2
    tf = _pick(f, (MOE_TF, LANES))
    n_j = f // tf
    n_items = item_e.shape[0]

    def jeff(j, inb, wi):
        return jnp.where(inb[wi] > 0, j, n_j - 1)

    kern = functools.partial(_expert_kernel, tm=tm, n_j=n_j)
    return pl.pallas_call(
        kern,
        grid_spec=pltpu.PrefetchScalarGridSpec(
            num_scalar_prefetch=3,
            grid=(n_items, n_j),
            in_specs=[
                pl.BlockSpec(memory_space=pl.ANY),
                pl.BlockSpec((None, d, 2 * tf), lambda wi, j, ie, ist, inb: (ie[wi], 0, jeff(j, inb, wi))),
                pl.BlockSpec((None, 1, 2 * tf), lambda wi, j, ie, ist, inb: (ie[wi], 0, jeff(j, inb, wi))),
                pl.BlockSpec((None, tf, d), lambda wi, j, ie, ist, inb: (ie[wi], jeff(j, inb, wi), 0)),
                pl.BlockSpec((None, 1, d), lambda wi, j, ie, ist, inb: (ie[wi], 0, 0)),
            ],
            out_specs=pl.BlockSpec(memory_space=pl.ANY),
            scratch_shapes=[pltpu.VMEM((xmax, d), BF16),
                            pltpu.VMEM((xmax, d), F32),
                            pltpu.VMEM((xmax, d // 2), U32),
                            pltpu.VMEM((d, 2 * tf), BF16),
                            pltpu.VMEM((tf, d), BF16),
                            pltpu.VMEM((2, tm, d // 2), U32),
                            pltpu.SemaphoreType.DMA(()),
                            pltpu.SemaphoreType.DMA((2,))],
        ),
        out_shape=jax.ShapeDtypeStruct((n_rows, d // 2), U32),
        compiler_params=_params(("arbitrary", "arbitrary")),
        name="experts",
    )(item_e, item_start, item_nblk, rows, w_gate_up, b_gate_up.reshape(n_exp, 1, f2),
      w_down, b_down.reshape(n_exp, 1, d))


def _combine_kernel(pos_ref, y_ref, gate_ref, h_ref, g_ref, b_ref, o_ref, ybuf0_ref, ybuf1_ref, sem,
                    *, alpha):
    tm = h_ref.shape[0]
    i = pl.program_id(0)
    n_tiles = pl.num_programs(0)
    n = tm * TOP_K
    bufs = (ybuf0_ref, ybuf1_ref)

    def gather(tile, s):
        base = tile * n
        for t in range(tm):
            for kk in range(TOP_K):
                _row_copy(y_ref, pos_ref[base + (t * TOP_K + kk)], bufs[s], kk * tm + t, sem.at[s]).start()

    def wait_buf(s):
        pltpu.make_async_copy(y_ref.at[pl.ds(0, n)], bufs[s], sem.at[s]).wait()

    def step(s):
        wait_buf(s)
        gather(jnp.minimum(i + 1, n_tiles - 1), 1 - s)
        gate = gate_ref[...]
        half = y_ref.shape[1]
        ffn_lo = jnp.zeros((tm, half), F32)
        ffn_hi = jnp.zeros((tm, half), F32)
        for kk in range(TOP_K):
            p = bufs[s][kk * tm:(kk + 1) * tm, :]
            gk = gate[:, kk:kk + 1]
            ffn_lo = ffn_lo + gk * lax.bitcast_convert_type(p << 16, F32)
            ffn_hi = ffn_hi + gk * lax.bitcast_convert_type(p & jnp.uint32(0xFFFF0000), F32)
        ffn = jnp.concatenate([ffn_lo, ffn_hi], axis=1)
        o_ref[...] = _layer_norm(alpha * h_ref[...] + ffn, g_ref[...], b_ref[...])

        @pl.when(i == n_tiles - 1)
        def _():
            wait_buf(1 - s)

    @pl.when(i == 0)
    def _():
        gather(0, 0)

    for s in range(2):
        @pl.when((i & 1) == s)
        def _():
            step(s)


def _combine(y, pos, gate, h, ln_g, ln_b, alpha):
    t, d = h.shape
    tm = _pick(t, (128, 64, 32, 16, 8))
    assert y.shape[1] * 2 == d
    return pl.pallas_call(
        functools.partial(_combine_kernel, alpha=alpha),
        grid_spec=pltpu.PrefetchScalarGridSpec(
            num_scalar_prefetch=1,
            grid=(t // tm,),
            in_specs=[pl.BlockSpec(memory_space=pl.ANY),
                      pl.BlockSpec((tm, LANES), lambda i, pos: (i, 0)),
                      pl.BlockSpec((tm, d), lambda i, pos: (i, 0)),
                      pl.BlockSpec((1, d), lambda i, pos: (0, 0)),
                      pl.BlockSpec((1, d), lambda i, pos: (0, 0))],
            out_specs=pl.BlockSpec((tm, d), lambda i, pos: (i, 0)),
            scratch_shapes=[pltpu.VMEM((TOP_K * tm, d // 2), U32), pltpu.VMEM((TOP_K * tm, d // 2), U32),
                            pltpu.SemaphoreType.DMA((2,))],
        ),
        out_shape=jax.ShapeDtypeStruct((t, d), F32),
        compiler_params=_params(("arbitrary",)),
        name="combine",
    )(pos, y, gate, h, ln_g.reshape(1, d), ln_b.reshape(1, d))


def _moe_plan(idx, rank, counts, *, tm, xmax, n_items):
    n_exp = counts.shape[0]
    padded = (counts + tm - 1) // tm * tm
    pstart = jnp.cumsum(padded) - padded
    pos = (pstart[idx] + rank).reshape(-1).astype(I32)
    per_e = (padded + xmax - 1) // xmax
    cum = jnp.cumsum(per_e)
    total = cum[-1]
    wi = jnp.arange(n_items, dtype=I32)
    valid = wi < total
    e_w = jnp.minimum(jnp.searchsorted(cum, jnp.minimum(wi, total - 1), side="right"), n_exp - 1).astype(I32)
    local = jnp.minimum(wi, total - 1) - (cum - per_e)[e_w]
    start = (pstart[e_w] + local * xmax).astype(I32)
    nblk = jnp.where(valid, jnp.clip(padded[e_w] - local * xmax, 0, xmax) // tm, 0).astype(I32)
    start = jnp.concatenate([start, jnp.sum(padded, keepdims=True).astype(I32)])
    return pos, e_w, start, nblk


def _layer(h0, h0b, p, *, bsz, seq, alpha):
    t, d = h0.shape
    w_in = p["w_in"]
    hv = p["gdn_a_log"].shape[0]
    dk = p["gdn_norm_w"].shape[0]
    qkv_dim = p["gdn_conv_w"].shape[1]
    v_dim = hv * dk
    hq = (qkv_dim - v_dim) // (2 * dk)
    scw = p["sc_conv_w"].shape[1]
    assert 2 * hv <= LANES and hv % hq == 0
    n_a = qkv_dim + v_dim
    off_ba = n_a
    off_b = n_a + 2 * hv
    n_b = 3 * scw + 2 * d
    assert w_in.shape[1] == off_b + n_b

    w_in_t = w_in.T
    tm = _pick(t, (1024, 512, 256, 128))
    tn_a = _pick(n_a, (1024, 512, 256, 128))
    (proj_a,) = _matmul(h0b, w_in_t, col0=0, n_cols=n_a, tm=tm, tn=tn_a, out_dtypes=(BF16,), w_t=True)
    tn_b = _pick(n_b, (512, 256, 128))
    (proj_b,) = _matmul(h0b, w_in_t, col0=off_b, n_cols=n_b, tm=tm, tn=tn_b, out_dtypes=(BF16,), w_t=True)
    assert off_ba + LANES <= w_in.shape[1]
    (ba,) = _matmul(h0b, w_in_t, col0=off_ba, n_cols=LANES, tm=tm, tn=LANES, out_dtypes=(F32,), w_t=True)

    gcol, grow = _gdn_gates(ba, p["gdn_a_log"], p["gdn_dt_bias"], hv)
    o_n = _gdn(proj_a, p["gdn_conv_w"], gcol, grow, p["gdn_norm_w"], bsz=bsz, seq=seq, hq=hq, hv=hv, dk=dk)

    u = _short_conv(proj_b, p["sc_conv_w"], bsz=bsz, seq=seq, width=scw)

    tn = _pick(d, (512, 256, 128))
    ga0 = 3 * scw // tn
    gb0 = (3 * scw + d) // tn
    (part_a,) = _matmul(
        o_n, p["w_out_gdn"], col0=0, n_cols=d, tm=tm, tn=tn, out_dtypes=(F32,),
        epilogue=lambda acc, ga: (jax.nn.sigmoid(ga.astype(F32)) * acc,),
        extras=[(proj_b, (tm, tn), lambda j, i: (i, ga0 + j))])
    (merged,) = _matmul(
        u, p["w_out_sc"], col0=0, n_cols=d, tm=tm, tn=tn, out_dtypes=(BF16,),
        epilogue=lambda acc, gb, pa: (pa + jax.nn.sigmoid(gb.astype(F32)) * acc,),
        extras=[(proj_b, (tm, tn), lambda j, i: (i, gb0 + j)),
                (part_a, (tm, tn), lambda j, i: (i, j))])
    tm3 = _pick(t, (256, 128))
    def mix_epilogue(acc, hh, g, b):
        hn = _layer_norm(alpha * hh + acc, g, b)
        return hn, _pack_halves_bf16(hn)

    h1, h1p = _matmul(
        merged, p["w_out"].astype(BF16), col0=0, n_cols=d, tm=tm3, tn=d, out_dtypes=(F32, U32), out_div=(1, 2),
        epilogue=mix_epilogue,
        extras=[(h0, (tm3, d), lambda j, i: (i, 0)),
                (p["ln_mix_g"].reshape(1, d), (1, d), lambda j, i: (0, 0)),
                (p["ln_mix_b"].reshape(1, d), (1, d), lambda j, i: (0, 0))])

    n_exp = p["w_router"].shape[1]
    idx, gate, rank, cnt = _router(h1, p["w_router"], p["b_router"])
    mtm = MOE_TM
    xmax = MOE_XMAX
    n_rows = t * TOP_K + n_exp * mtm
    n_items = n_exp + n_rows // xmax
    pos, item_e, item_start, item_nblk = _moe_plan(
        idx[:, :TOP_K], rank[:, :TOP_K], cnt[0, :n_exp], tm=mtm, xmax=xmax, n_items=n_items)
    rows = _dispatch(h1p, pos, n_rows)
    y = _experts(rows, item_e, item_start, item_nblk, p["w_gate_up"], p["b_gate_up"], p["w_down"], p["b_down"],
                 tm=mtm, xmax=xmax)
    h2 = _combine(y, pos, gate, h1, p["ln_ffn_g"], p["ln_ffn_b"], alpha)
    return h2


_LAYER_PARAMS = ("w_in", "gdn_conv_w", "gdn_a_log", "gdn_dt_bias", "gdn_norm_w", "w_out_gdn", "sc_conv_w",
                 "w_out_sc", "w_out", "ln_mix_g", "ln_mix_b", "w_router", "b_router", "w_gate_up", "b_gate_up",
                 "w_down", "b_down", "ln_ffn_g", "ln_ffn_b")


def kernel(x, ln_in_g, ln_in_b, w_in, gdn_conv_w, gdn_a_log, gdn_dt_bias, gdn_norm_w, w_out_gdn, sc_conv_w,
           w_out_sc, w_out, ln_mix_g, ln_mix_b, w_router, b_router, w_gate_up, b_gate_up, w_down, b_down,
           ln_ffn_g, ln_ffn_b):
    stacked = dict(zip(_LAYER_PARAMS, (w_in, gdn_conv_w, gdn_a_log, gdn_dt_bias, gdn_norm_w, w_out_gdn,
                                       sc_conv_w, w_out_sc, w_out, ln_mix_g, ln_mix_b, w_router, b_router,
                                       w_gate_up, b_gate_up, w_down, b_down, ln_ffn_g, ln_ffn_b)))
    bsz, seq, d = x.shape
    depth = w_in.shape[0]
    alpha = (2 * depth) ** 0.25
    h, hb = _ln_in(x.reshape(bsz * seq, d), ln_in_g, ln_in_b)
    for l in range(depth):
        p = {name: arr[l] for name, arr in stacked.items()}
        h = _layer(h, hb, p, bsz=bsz, seq=seq, alpha=alpha)
        if l + 1 < depth:
            hb = h.astype(BF16)
    return h.reshape(bsz, seq, d)
```

```python
import functools

import jax
import jax.numpy as jnp
from jax import lax
from jax.experimental import pallas as pl
from jax.experimental.pallas import tpu as pltpu

F32 = jnp.float32
BF16 = jnp.bfloat16
I32 = jnp.int32
U32 = jnp.uint32

LANES = 128
SUBLANES = 8
MXU_N = 256
VMEM_LIMIT = 60 << 20

TOP_K = 4
SWIGLU_LIMIT = 7.0
SWIGLU_ALPHA = 1.702
LN_EPS = 1e-5
RMS_EPS = 1e-6
GDN_CHUNK = 64
GDN_STEP = 256
GDN_QK_GROUP = 4
GDN_INV_GROUP = 16
MOE_TM = 256
MOE_XMAX = 1280
MOE_TF = 512


def _params(sem):
    return pltpu.CompilerParams(dimension_semantics=sem, vmem_limit_bytes=VMEM_LIMIT)


def _pick(n, candidates):
    for c in candidates:
        if n % c == 0:
            return c
    raise ValueError(f"no tile for {n} in {candidates}")


def _layer_norm(xf, g, b):
    mu = jnp.mean(xf, axis=-1, keepdims=True)
    xc = xf - mu
    var = jnp.mean(xc * xc, axis=-1, keepdims=True)
    return xc * lax.rsqrt(var + LN_EPS) * g + b


def _pack_halves_bf16(x):
    half = x.shape[1] // 2
    lo = lax.bitcast_convert_type(x[:, :half].astype(BF16).astype(F32), U32) >> 16
    hi = lax.bitcast_convert_type(x[:, half:].astype(BF16).astype(F32), U32) & jnp.uint32(0xFFFF0000)
    return lo | hi


def _unpack_halves_bf16(p):
    lo = lax.bitcast_convert_type(p << 16, F32).astype(BF16)
    hi = lax.bitcast_convert_type(p & jnp.uint32(0xFFFF0000), F32).astype(BF16)
    return lo, hi


def _ln_in_kernel(x_ref, g_ref, b_ref, h_ref, hb_ref):
    h = _layer_norm(x_ref[...], g_ref[...], b_ref[...])
    h_ref[...] = h
    hb_ref[...] = h.astype(BF16)


def _ln_in(x2, g, b):
    t, d = x2.shape
    tm = _pick(t, (512, 256, 128, 64, 32, 16))
    return pl.pallas_call(
        _ln_in_kernel,
        grid=(t // tm,),
        in_specs=[pl.BlockSpec((tm, d), lambda i: (i, 0)),
                  pl.BlockSpec((1, d), lambda i: (0, 0)),
                  pl.BlockSpec((1, d), lambda i: (0, 0))],
        out_specs=[pl.BlockSpec((tm, d), lambda i: (i, 0)),
                   pl.BlockSpec((tm, d), lambda i: (i, 0))],
        out_shape=[jax.ShapeDtypeStruct((t, d), F32), jax.ShapeDtypeStruct((t, d), BF16)],
        compiler_params=_params(("arbitrary",)),
        name="ln_in",
    )(x2, g.reshape(1, d), b.reshape(1, d))


def _mm_kernel(*refs, shift, w_t, n_extra, n_out, epilogue):
    n_w = 2 if shift else 1
    x_ref, w_refs = refs[0], refs[1:1 + n_w]
    extras = refs[1 + n_w:1 + n_w + n_extra]
    o_refs = refs[1 + n_w + n_extra:1 + n_w + n_extra + n_out]
    wb_ref = refs[-1]
    n_axis = 0 if w_t else 1

    @pl.when(pl.program_id(1) == 0)
    def _():
        if shift:
            tn = w_refs[0].shape[n_axis]
            w = jnp.concatenate([lax.slice_in_dim(w_refs[0][...], shift, tn, axis=n_axis),
                                 lax.slice_in_dim(w_refs[1][...], 0, shift, axis=n_axis)], axis=n_axis)
        else:
            w = w_refs[0][...]
        wb_ref[...] = w.astype(BF16)

    acc = lax.dot_general(x_ref[...], wb_ref[...], (((1,), (1 if w_t else 0,)), ((), ())),
                          preferred_element_type=F32)
    outs = epilogue(acc, *[e[...] for e in extras])
    for o_ref, o in zip(o_refs, outs):
        o_ref[...] = o.astype(o_ref.dtype)


def _matmul(x, w, *, col0, n_cols, tm, tn, out_dtypes, epilogue=None, extras=(), out_div=None, w_t=False):
    t, k = x.shape
    k_axis, n_axis = (1, 0) if w_t else (0, 1)
    assert w.shape[k_axis] == k and n_cols % tn == 0 and t % tm == 0 and col0 + n_cols <= w.shape[n_axis]
    j0, shift = divmod(col0, tn)
    if epilogue is None:
        epilogue = lambda acc: (acc,)
    if out_div is None:
        out_div = (1,) * len(out_dtypes)
    kern = functools.partial(_mm_kernel, shift=shift, w_t=w_t, n_extra=len(extras), n_out=len(out_dtypes),
                             epilogue=epilogue)
    w_block = (tn, k) if w_t else (k, tn)
    w_index = lambda jj: (jj, 0) if w_t else (0, jj)
    in_specs = [pl.BlockSpec((tm, k), lambda j, i: (i, 0)),
                pl.BlockSpec(w_block, lambda j, i: w_index(j + j0))]
    ws = [w]
    if shift:
        in_specs.append(pl.BlockSpec(w_block, lambda j, i: w_index(j + j0 + 1)))
        ws.append(w)
    in_specs += [pl.BlockSpec(bs, im) for (_, bs, im) in extras]
    return pl.pallas_call(
        kern,
        grid=(n_cols // tn, t // tm),
        in_specs=in_specs,
        out_specs=[pl.BlockSpec((tm, tn // dv), lambda j, i: (i, j)) for dv in out_div],
        out_shape=[jax.ShapeDtypeStruct((t, n_cols // dv), dt) for dt, dv in zip(out_dtypes, out_div)],
        scratch_shapes=[pltpu.VMEM(w_block, BF16)],
        compiler_params=_params(("arbitrary", "arbitrary")),
        name="matmul",
    )(x, *ws, *[a for (a, _, _) in extras])


def _gates_kernel(ba_ref, alog_ref, dt_ref, col_ref, row_ref, *, hv):
    x = ba_ref[...]
    lane = lax.broadcasted_iota(I32, x.shape, 1)
    beta = jax.nn.sigmoid(x)
    xs = x + dt_ref[...]
    softplus = jnp.maximum(xs, 0.0) + jnp.log(1.0 + jnp.exp(-jnp.abs(xs)))
    g = -jnp.exp(alog_ref[...]) * softplus
    g = jnp.where((lane >= hv) & (lane < 2 * hv), g, 0.0)
    n = x.shape[0]
    row = lax.broadcasted_iota(I32, x.shape, 0) & (GDN_CHUNK - 1)
    cum = g
    rev = g
    sh = 1
    while sh < GDN_CHUNK:
        cum = cum + jnp.where(row >= sh, pltpu.roll(cum, sh, 0), 0.0)
        rev = rev + jnp.where(row < GDN_CHUNK - sh, pltpu.roll(rev, n - sh, 0), 0.0)
        sh *= 2
    e_cum = pltpu.roll(jnp.exp(cum), hv, 1)
    e_rest = pltpu.roll(jnp.exp(rev - g), 2 * hv, 1)
    out = jnp.where(lane < hv, beta, jnp.where(lane < 2 * hv, cum, jnp.where(lane < 3 * hv, e_cum, e_rest)))
    col_ref[...] = out
    row_ref[...] = out.T


def _gdn_gates(ba, a_log, dt_bias, hv):
    assert 4 * hv <= LANES
    t = ba.shape[0]
    tm = _pick(t, (512, 256, 128))
    pad = lambda v: jnp.zeros((1, LANES), F32).at[0, hv:2 * hv].set(v.astype(F32))
    return pl.pallas_call(
        functools.partial(_gates_kernel, hv=hv),
        grid=(t // tm,),
        in_specs=[pl.BlockSpec((tm, LANES), lambda i: (i, 0)),
                  pl.BlockSpec((1, LANES), lambda i: (0, 0)),
                  pl.BlockSpec((1, LANES), lambda i: (0, 0))],
        out_specs=[pl.BlockSpec((tm, LANES), lambda i: (i, 0)),
                   pl.BlockSpec((LANES, tm), lambda i: (0, i))],
        out_shape=[jax.ShapeDtypeStruct((t, LANES), F32), jax.ShapeDtypeStruct((LANES, t), F32)],
        compiler_params=_params(("arbitrary",)),
        name="gdn_gates",
    )(ba, pad(a_log), pad(dt_bias))


def _causal_conv(x, w, ext_ref):
    n = x.shape[0]
    kw = w.shape[0]
    ext_ref[SUBLANES:, :] = x
    acc = x * w[kw - 1:kw]
    for d in range(1, kw):
        acc = acc + ext_ref[SUBLANES - d:SUBLANES - d + n, :] * w[kw - 1 - d:kw - d]
    ext_ref[:SUBLANES, :] = x[n - SUBLANES:]
    return acc


def _silu(x):
    h = 0.5 * x
    return h + h * jnp.tanh(h)


def _unit_lower_inverses(lows, ii, jj):
    c = lows[0].shape[0]
    eye = jnp.where(ii == jj, 1.0, 0.0)
    pair = (ii >> 1) == (jj >> 1)
    xbs = [(eye - jnp.where(pair, low, 0.0)).astype(BF16) for low in lows]
    lbs = [low.astype(BF16) for low in lows]
    zero = jnp.zeros((c, c), BF16)
    s = 1
    while (2 << s) <= c:
        m = ((ii >> (s + 1)) == (jj >> (s + 1))) & ((ii >> s) != (jj >> s))
        ts = [jnp.dot(jnp.where(m, lb, zero), xb, preferred_element_type=F32) for lb, xb in zip(lbs, xbs)]
        xbs = [jnp.dot(xb, (eye - t).astype(BF16), preferred_element_type=F32).astype(BF16)
               for xb, t in zip(xbs, ts)]
        s += 1
    return xbs


def _gdn_kernel(q_ref, k_ref, v_ref, z_ref, cwq_ref, cwk_ref, cwv_ref, gcol_ref, grow_ref, nw_ref,
                o_ref, s_ref, tq_ref, tk_ref, tv_ref, qn_ref, kn_ref, vn_ref, u_ref, wq_ref, attn_ref, kd_ref,
                *, nq, vper, dk):
    @pl.when(pl.program_id(2) == 0)
    def _():
        s_ref[...] = jnp.zeros_like(s_ref)
        for t_ref in (tq_ref, tk_ref, tv_ref):
            t_ref[:SUBLANES, :] = jnp.zeros((SUBLANES, t_ref.shape[1]), F32)

    tb = q_ref.shape[0]
    c = GDN_CHUNK
    nc = tb // c
    nv = nq * vper
    head = lambda h: slice(h * dk, (h + 1) * dk)

    q = _silu(_causal_conv(q_ref[...].astype(F32), cwq_ref[...], tq_ref))
    k = _silu(_causal_conv(k_ref[...].astype(F32), cwk_ref[...], tk_ref))
    for hq in range(nq):
        qh = q[:, head(hq)]
        kh = k[:, head(hq)]
        qn_ref[:, head(hq)] = qh * (lax.rsqrt(jnp.sum(qh * qh, axis=-1, keepdims=True) + RMS_EPS) * (dk ** -0.5))
        kn_ref[:, head(hq)] = kh * lax.rsqrt(jnp.sum(kh * kh, axis=-1, keepdims=True) + RMS_EPS)

    cols = gcol_ref[...]
    grows = grow_ref[...]
    nw = nw_ref[...]
    ii = lax.broadcasted_iota(I32, (c, c), 0)
    jj = lax.broadcasted_iota(I32, (c, c), 1)
    incl = ii >= jj
    strict = ii > jj

    def col(ci, which, h):
        return cols[ci * c:(ci + 1) * c, which * nv + h:which * nv + h + 1]

    def state_step(ci):
        rows = slice(ci * c, (ci + 1) * c)
        ps = [ci * nv + h for h in range(nv)]
        r1 = [jnp.dot(wq_ref[p], s_ref[h].astype(BF16), preferred_element_type=F32) for h, p in enumerate(ps)]
        v_new = [(u_ref[p] - r[:c]).astype(BF16) for p, r in zip(ps, r1)]
        outs = [r[c:] + jnp.dot(attn_ref[p], vn, preferred_element_type=F32) for p, r, vn in zip(ps, r1, v_new)]
        for h, (p, vn) in enumerate(zip(ps, v_new)):
            g_tot = cols[(ci + 1) * c - 1:(ci + 1) * c, 2 * nv + h:2 * nv + h + 1]
            s_ref[h] = s_ref[h] * g_tot + lax.dot_general(
                kd_ref[p], vn, (((0,), (0,)), ((), ())), preferred_element_type=F32)
        for h, o in enumerate(outs):
            zc = z_ref[rows, head(h)].astype(F32)
            var = jnp.mean(o * o, axis=-1, keepdims=True)
            o_ref[rows, head(h)] = (o * lax.rsqrt(var + RMS_EPS) * nw * _silu(zc)).astype(BF16)

    problems = [(ci, h) for ci in range(nc) for h in range(nv)]
    gram = {}
    chunks_done = 0
    for g0 in range(0, len(problems), GDN_INV_GROUP):
        group = problems[g0:g0 + GDN_INV_GROUP]
        lows = []
        for ci, h in group:
            hq = h // vper
            rows = slice(ci * c, (ci + 1) * c)
            if (ci, hq) not in gram:
                kcb = kn_ref[rows, head(hq)].astype(BF16)
                qcb = qn_ref[rows, head(hq)].astype(BF16)
                gram[(ci, hq)] = lax.dot_general(jnp.concatenate([kcb, qcb], axis=0), kcb,
                                                 (((1,), (1,)), ((), ())), preferred_element_type=F32)
            a = gram[(ci, hq)]
            grow = grows[h:h + 1, rows]
            decay = jnp.where(incl, jnp.exp(jnp.minimum(col(ci, 1, h) - grow, 0.0)), 0.0)
            lows.append(jnp.where(strict, col(ci, 0, h) * a[:c] * decay, 0.0))
            attn_ref[ci * nv + h] = jnp.where(incl, a[c:] * decay, 0.0).astype(BF16)
        if g0 == 0:
            vn_ref[...] = _silu(_causal_conv(v_ref[...].astype(F32), cwv_ref[...], tv_ref))
        xs = _unit_lower_inverses(lows, ii, jj)
        for x, (ci, h) in zip(xs, group):
            hq = h // vper
            rows = slice(ci * c, (ci + 1) * c)
            p = ci * nv + h
            kc = kn_ref[rows, head(hq)]
            bcol = col(ci, 0, h)
            eg = col(ci, 2, h)
            rhs = jnp.concatenate([vn_ref[rows, head(h)] * bcol, kc * (bcol * eg)], axis=1).astype(BF16)
            uw = jnp.dot(x, rhs, preferred_element_type=F32)
            u_ref[p] = uw[:, :dk]
            wq_ref[p, :c, :] = uw[:, dk:].astype(BF16)
            wq_ref[p, c:, :] = (qn_ref[rows, head(hq)] * eg).astype(BF16)
            kd_ref[p] = (kc * col(ci, 3, h)).astype(BF16)
        chunks_ready = (g0 + len(group)) // nv
        for ci in range(chunks_done, chunks_ready):
            state_step(ci)
        chunks_done = chunks_ready


def _gdn(proj_a, conv_w, gcol, grow, norm_w, *, bsz, seq, hq, hv, dk):
    t = bsz * seq
    vper = hv // hq
    nq = _pick(hq, (GDN_QK_GROUP, 2, 1))
    nv = nq * vper
    ng = hq // nq
    tb = _pick(seq, (GDN_STEP, 128, 64))
    ns = seq // tb
    nprob = (tb // GDN_CHUNK) * nv
    qw = nq * dk
    vw = nv * dk
    v_blk0 = 2 * hq * dk // vw
    z_blk0 = (2 * hq * dk + hv * dk) // vw
    kw = conv_w.shape[0]
    gcol = gcol[:, :4 * hv].reshape(t, 4, ng, nv).transpose(2, 0, 1, 3).reshape(ng, t, 4 * nv)
    grow = grow[hv:2 * hv].reshape(ng, nv, t)
    rowblk = lambda b, g, s: b * ns + s
    kern = functools.partial(_gdn_kernel, nq=nq, vper=vper, dk=dk)
    return pl.pallas_call(
        kern,
        grid=(bsz, ng, ns),
        in_specs=[
            pl.BlockSpec((tb, qw), lambda b, g, s: (rowblk(b, g, s), g)),
            pl.BlockSpec((tb, qw), lambda b, g, s: (rowblk(b, g, s), ng + g)),
            pl.BlockSpec((tb, vw), lambda b, g, s: (rowblk(b, g, s), v_blk0 + g)),
            pl.BlockSpec((tb, vw), lambda b, g, s: (rowblk(b, g, s), z_blk0 + g)),
            pl.BlockSpec((kw, qw), lambda b, g, s: (0, g)),
            pl.BlockSpec((kw, qw), lambda b, g, s: (0, ng + g)),
            pl.BlockSpec((kw, vw), lambda b, g, s: (0, v_blk0 + g)),
            pl.BlockSpec((None, tb, 4 * nv), lambda b, g, s: (g, rowblk(b, g, s), 0)),
            pl.BlockSpec((None, nv, tb), lambda b, g, s: (g, 0, rowblk(b, g, s))),
            pl.BlockSpec((1, dk), lambda b, g, s: (0, 0)),
        ],
        out_specs=pl.BlockSpec((tb, vw), lambda b, g, s: (rowblk(b, g, s), g)),
        out_shape=jax.ShapeDtypeStruct((t, hv * dk), BF16),
        scratch_shapes=[pltpu.VMEM((nv, dk, dk), F32),
                        pltpu.VMEM((tb + SUBLANES, qw), F32),
                        pltpu.VMEM((tb + SUBLANES, qw), F32),
                        pltpu.VMEM((tb + SUBLANES, vw), F32),
                        pltpu.VMEM((tb, qw), F32),
                        pltpu.VMEM((tb, qw), F32),
                        pltpu.VMEM((tb, vw), F32),
                        pltpu.VMEM((nprob, GDN_CHUNK, dk), F32),
                        pltpu.VMEM((nprob, 2 * GDN_CHUNK, dk), BF16),
                        pltpu.VMEM((nprob, GDN_CHUNK, GDN_CHUNK), BF16),
                        pltpu.VMEM((nprob, GDN_CHUNK, dk), BF16)],
        compiler_params=_params(("arbitrary", "arbitrary", "arbitrary")),
        name="gdn",
    )(proj_a, proj_a, proj_a, proj_a, conv_w, conv_w, conv_w, gcol, grow, norm_w.reshape(1, dk))


def _sc_kernel(b_ref, c_ref, x_ref, w_ref, u_ref, tail_ref):
    @pl.when(pl.program_id(1) == 0)
    def _():
        tail_ref[:SUBLANES, :] = jnp.zeros((SUBLANES, tail_ref.shape[1]), F32)

    p = c_ref[...].astype(F32) * x_ref[...].astype(F32)
    u_ref[...] = (b_ref[...].astype(F32) * _causal_conv(p, w_ref[...], tail_ref)).astype(BF16)


def _short_conv(proj_b, conv_w, *, bsz, seq, width):
    t = bsz * seq
    ts = _pick(seq, (512, 256, 128, 64))
    ns = seq // ts
    kw = conv_w.shape[0]
    return pl.pallas_call(
        _sc_kernel,
        grid=(bsz, ns),
        in_specs=[pl.BlockSpec((ts, width), lambda b, s: (b * ns + s, 0)),
                  pl.BlockSpec((ts, width), lambda b, s: (b * ns + s, 1)),
                  pl.BlockSpec((ts, width), lambda b, s: (b * ns + s, 2)),
                  pl.BlockSpec((kw, width), lambda b, s: (0, 0))],
        out_specs=pl.BlockSpec((ts, width), lambda b, s: (b * ns + s, 0)),
        out_shape=jax.ShapeDtypeStruct((t, width), BF16),
        scratch_shapes=[pltpu.VMEM((ts + SUBLANES, width), F32)],
        compiler_params=_params(("arbitrary", "arbitrary")),
        name="short_conv",
    )(proj_b, proj_b, proj_b, conv_w)


def _router_kernel(h_ref, w_ref, b_ref, idx_ref, gate_ref, rank_ref, cnt_ref, run_ref):
    @pl.when(pl.program_id(0) == 0)
    def _():
        run_ref[...] = jnp.zeros_like(run_ref)

    tm = h_ref.shape[0]
    logits = jnp.dot(h_ref[...], w_ref[...], preferred_element_type=F32,
                     precision=lax.Precision.HIGHEST) + b_ref[...]
    lane = lax.broadcasted_iota(I32, logits.shape, 1)
    lane_f = lane.astype(F32)
    cur = logits
    idxs, vals = [], []
    for _ in range(TOP_K):
        m = jnp.max(cur, axis=-1, keepdims=True)
        ix = jnp.min(jnp.where(cur == m, lane_f, float(LANES)), axis=-1, keepdims=True).astype(I32)
        idxs.append(ix)
        vals.append(m)
        cur = jnp.where(lane == ix, -jnp.inf, cur)
    es = [jnp.exp(val - vals[0]) for val in vals]
    den = es[0]
    for e in es[1:]:
        den = den + e
    onehot = jnp.zeros(logits.shape, F32)
    for ix in idxs:
        onehot = onehot + jnp.where(lane == ix, 1.0, 0.0)
    ri = lax.broadcasted_iota(I32, (tm, tm), 0)
    ci = lax.broadcasted_iota(I32, (tm, tm), 1)
    tri = jnp.where(ri > ci, 1.0, 0.0).astype(BF16)
    before = jnp.dot(tri, onehot.astype(BF16), preferred_element_type=F32) + run_ref[...]
    idx_out = jnp.zeros(logits.shape, I32)
    rank_out = jnp.zeros(logits.shape, I32)
    gate_out = jnp.zeros(logits.shape, F32)
    for kk in range(TOP_K):
        rank = jnp.sum(jnp.where(lane == idxs[kk], before, 0.0), axis=-1, keepdims=True).astype(I32)
        idx_out = jnp.where(lane == kk, idxs[kk], idx_out)
        rank_out = jnp.where(lane == kk, rank, rank_out)
        gate_out = jnp.where(lane == kk, es[kk] / den, gate_out)
    idx_ref[...] = idx_out
    rank_ref[...] = rank_out
    gate_ref[...] = gate_out
    run_ref[...] = run_ref[...] + jnp.sum(onehot, axis=0, keepdims=True)
    cnt_ref[...] = run_ref[...].astype(I32)


def _router(h, w_router, b_router):
    t, d = h.shape
    e = w_router.shape[1]
    tm = _pick(t, (256, 128, 64, 32, 16, 8))
    wp = jnp.zeros((d, LANES), F32).at[:, :e].set(w_router)
    bp = jnp.full((1, LANES), -1e30, F32).at[0, :e].set(b_router)
    tile = pl.BlockSpec((tm, LANES), lambda i: (i, 0))
    return pl.pallas_call(
        _router_kernel,
        grid=(t // tm,),
        in_specs=[pl.BlockSpec((tm, d), lambda i: (i, 0)),
                  pl.BlockSpec((d, LANES), lambda i: (0, 0)),
                  pl.BlockSpec((1, LANES), lambda i: (0, 0))],
        out_specs=[tile, tile, tile, pl.BlockSpec((1, LANES), lambda i: (0, 0))],
        out_shape=[jax.ShapeDtypeStruct((t, LANES), I32), jax.ShapeDtypeStruct((t, LANES), F32),
                   jax.ShapeDtypeStruct((t, LANES), I32), jax.ShapeDtypeStruct((1, LANES), I32)],
        scratch_shapes=[pltpu.VMEM((1, LANES), F32)],
        compiler_params=_params(("arbitrary",)),
        name="router",
    )(h, wp, bp)


def _row_copy(src, s_row, dst, d_row, sem):
    return pltpu.make_async_copy(src.at[pl.ds(s_row, 1)], dst.at[pl.ds(d_row, 1)], sem)


def _dispatch_kernel(pos_ref, x_ref, rows_in_ref, rows_ref, sem):
    del rows_in_ref
    tm = x_ref.shape[0]
    base = pl.program_id(0) * (tm * TOP_K)

    for t in range(tm):
        for kk in range(TOP_K):
            _row_copy(x_ref, t, rows_ref, pos_ref[base + (t * TOP_K + kk)], sem).start(priority=kk & 1)
    n = tm * TOP_K
    pltpu.make_async_copy(rows_ref.at[pl.ds(0, n)], rows_ref.at[pl.ds(0, n)], sem).wait()


def _dispatch(h, pos, n_rows):
    t, d = h.shape
    tm = _pick(t, (128, 64, 32, 16, 8))
    rows0 = jnp.zeros((n_rows, d), h.dtype)
    return pl.pallas_call(
        _dispatch_kernel,
        grid_spec=pltpu.PrefetchScalarGridSpec(
            num_scalar_prefetch=1,
            grid=(t // tm,),
            in_specs=[pl.BlockSpec((tm, d), lambda i, pos: (i, 0)),
                      pl.BlockSpec(memory_space=pl.ANY)],
            out_specs=pl.BlockSpec(memory_space=pl.ANY),
            scratch_shapes=[pltpu.SemaphoreType.DMA(())],
        ),
        out_shape=jax.ShapeDtypeStruct((n_rows, d), h.dtype),
        input_output_aliases={2: 0},
        compiler_params=_params(("arbitrary",)),
        name="dispatch",
    )(pos, h, rows0)


def _swiglu_interleaved(gu):
    lane = lax.broadcasted_iota(I32, (gu.shape[0], LANES), 1)
    even = (lane & 1) == 0
    parts = []
    for c in range(gu.shape[1] // LANES):
        g = gu[:, c * LANES:(c + 1) * LANES]
        gt = jnp.minimum(g, SWIGLU_LIMIT)
        glu = gt * jax.nn.sigmoid(SWIGLU_ALPHA * gt)
        up = jnp.clip(g, -SWIGLU_LIMIT, SWIGLU_LIMIT) + 1.0
        parts.append(jnp.where(even, pltpu.roll(up, LANES - 1, 1) * glu, 0.0))
    outs = [parts[2 * m] + pltpu.roll(parts[2 * m + 1], 1, 1) for m in range(len(parts) // 2)]
    return jnp.concatenate(outs, axis=1) if len(outs) > 1 else outs[0]


def _interleave_rows_bf16(w_ref, out_ref):
    half = LANES // 2
    for m in range(w_ref.shape[0] // LANES):
        lo = w_ref[m * LANES:m * LANES + half, :].astype(BF16).astype(F32)
        hi = w_ref[m * LANES + half:(m + 1) * LANES, :].astype(BF16).astype(F32)
        packed = (lax.bitcast_convert_type(lo, jnp.uint32) >> 16) | (
            lax.bitcast_convert_type(hi, jnp.uint32) & jnp.uint32(0xFFFF0000))
        out_ref[m * LANES:(m + 1) * LANES, :] = pltpu.bitcast(packed, BF16)


def _expert_kernel(ie_ref, ist_ref, inb_ref, rows_ref, wgu_ref, bgu_ref, wd_ref, bd_ref, y_ref,
                   xb_ref, acc_ref, stage_ref, wgub_ref, wdb_ref, ost_ref, sem_in, sem_out, *, tm, n_j):
    del ie_ref
    wi = pl.program_id(0)
    j = pl.program_id(1)
    n_items = pl.num_programs(0)
    nblk = inb_ref[wi]
    half = stage_ref.shape[1]

    def block(b):
        return pl.ds(pl.multiple_of(b * tm, tm), tm)

    def hbm_block(item, b):
        return pl.ds(pl.multiple_of(ist_ref[item] + b * tm, tm), tm)

    def rows_copy(item, b):
        return pltpu.make_async_copy(rows_ref.at[hbm_block(item, b)], stage_ref.at[block(b)], sem_in)

    def y_copy(item, b):
        slot = b & 1
        return pltpu.make_async_copy(ost_ref.at[slot], y_ref.at[hbm_block(item, b)], sem_out.at[slot])

    def drain_stores(item):
        n = inb_ref[item]
        for back in (2, 1):
            @pl.when(n >= back)
            def _():
                y_copy(item, n - back).wait()

    def for_blocks(item, fn):
        def body(b, carry):
            fn(item, b)
            return carry
        lax.fori_loop(0, inb_ref[item], body, 0)

    def unpack(item, b):
        lo, hi = _unpack_halves_bf16(stage_ref[block(b), :])
        xb_ref[block(b), :half] = lo
        xb_ref[block(b), half:] = hi

    @pl.when(j == 0)
    def _():
        @pl.when(wi == 0)
        def _():
            for_blocks(0, lambda it, b: rows_copy(it, b).start())
            acc_ref[...] = jnp.zeros_like(acc_ref)

        @pl.when(wi > 0)
        def _():
            drain_stores(wi - 1)

        for_blocks(wi, lambda it, b: rows_copy(it, b).wait())
        for_blocks(wi, unpack)

        @pl.when(wi + 1 < n_items)
        def _():
            for_blocks(wi + 1, lambda it, b: rows_copy(it, b).start())

    @pl.when(nblk > 0)
    def _():
        bias = bgu_ref[...]
        n_ct, _, ct = wgub_ref.shape

        def convert_weights():
            for c in range(n_ct):
                wgub_ref[c] = wgu_ref[:, c * ct:(c + 1) * ct].astype(BF16)
            _interleave_rows_bf16(wd_ref, wdb_ref)

        def mlp(b):
            x = xb_ref[block(b), :]
            gu = jnp.concatenate([jnp.dot(x, wgub_ref[c], preferred_element_type=F32) for c in range(n_ct)],
                                 axis=1) + bias
            act = _swiglu_interleaved(gu).astype(BF16)
            return jnp.dot(act, wdb_ref[...], preferred_element_type=F32)

        def accumulate(b):
            prev = jnp.where(j == 0, jnp.broadcast_to(bd_ref[...], (tm, acc_ref.shape[1])), acc_ref[block(b), :])
            acc_ref[block(b), :] = prev + mlp(b)

        def finish(b):
            prev = acc_ref[block(b), :] if n_j > 1 else bd_ref[...]
            ost_ref[b & 1] = _pack_halves_bf16(prev + mlp(b))
            y_copy(wi, b).start()

        def run(fn, stores):
            @pl.when(nblk >= 2)
            def _():
                convert_weights()
                fn(0)
                fn(1)

            @pl.when(nblk == 1)
            def _():
                convert_weights()
                fn(0)

            def pair(i, carry):
                if stores:
                    y_copy(wi, 2 * i - 2).wait()
                    y_copy(wi, 2 * i - 1).wait()
                fn(2 * i)
                fn(2 * i + 1)
                return carry
            lax.fori_loop(1, nblk // 2, pair, 0)

            @pl.when(((nblk & 1) == 1) & (nblk >= 3))
            def _():
                if stores:
                    y_copy(wi, nblk - 3).wait()
                fn(nblk - 1)

        if n_j > 1:
            @pl.when(j < n_j - 1)
            def _():
                run(accumulate, False)

        @pl.when(j == n_j - 1)
        def _():
            run(finish, True)

    @pl.when((wi == n_items - 1) & (j == n_j - 1))
    def _():
        drain_stores(wi)
        ost_ref[0] = jnp.zeros(ost_ref.shape[1:], U32)

        def fill(b, carry):
            cp = pltpu.make_async_copy(ost_ref.at[0], y_ref.at[pl.ds(pl.multiple_of(b * tm, tm), tm)],
                                       sem_out.at[0])
            cp.start()
            cp.wait()
            return carry
        lax.fori_loop(ist_ref[n_items] // tm, y_ref.shape[0] // tm, fill, 0)


def _experts(rows, item_e, item_start, item_nblk, w_gate_up, b_gate_up, w_down, b_down, *, tm, xmax):
    n_rows = rows.shape[0]
    n_exp, d, f2 = w_gate_up.shape
    assert rows.shape[1] * 2 == d
    f = f2 // 2
    tf = _pick(f, (MOE_TF, LANES))
    n_j = f // tf
    ct = min(2 * tf, MXU_N)
    n_items = item_e.shape[0]

    def jeff(j, inb, wi):
        return jnp.where(inb[wi] > 0, j, n_j - 1)

    kern = functools.partial(_expert_kernel, tm=tm, n_j=n_j)
    return pl.pallas_call(
        kern,
        grid_spec=pltpu.PrefetchScalarGridSpec(
            num_scalar_prefetch=3,
            grid=(n_items, n_j),
            in_specs=[
                pl.BlockSpec(memory_space=pl.ANY),
                pl.BlockSpec((None, d, 2 * tf), lambda wi, j, ie, ist, inb: (ie[wi], 0, jeff(j, inb, wi))),
                pl.BlockSpec((None, 1, 2 * tf), lambda wi, j, ie, ist, inb: (ie[wi], 0, jeff(j, inb, wi))),
                pl.BlockSpec((None, tf, d), lambda wi, j, ie, ist, inb: (ie[wi], jeff(j, inb, wi), 0)),
                pl.BlockSpec((None, 1, d), lambda wi, j, ie, ist, inb: (ie[wi], 0, 0)),
            ],
            out_specs=pl.BlockSpec(memory_space=pl.ANY),
            scratch_shapes=[pltpu.VMEM((xmax, d), BF16),
                            pltpu.VMEM((xmax, d), F32),
                            pltpu.VMEM((xmax, d // 2), U32),
                            pltpu.VMEM((2 * tf // ct, d, ct), BF16),
                            pltpu.VMEM((tf, d), BF16),
                            pltpu.VMEM((2, tm, d // 2), U32),
                            pltpu.SemaphoreType.DMA(()),
                            pltpu.SemaphoreType.DMA((2,))],
        ),
        out_shape=jax.ShapeDtypeStruct((n_rows, d // 2), U32),
        compiler_params=_params(("arbitrary", "arbitrary")),
        name="experts",
    )(item_e, item_start, item_nblk, rows, w_gate_up, b_gate_up.reshape(n_exp, 1, f2),
      w_down, b_down.reshape(n_exp, 1, d))


def _combine_kernel(pos_ref, y_ref, gate_ref, h_ref, g_ref, b_ref, o_ref, ybuf0_ref, ybuf1_ref, sem,
                    *, alpha):
    tm = h_ref.shape[0]
    i = pl.program_id(0)
    n_tiles = pl.num_programs(0)
    n = tm * TOP_K
    bufs = (ybuf0_ref, ybuf1_ref)

    def gather(tile, s):
        base = tile * n
        for t in range(tm):
            for kk in range(TOP_K):
                _row_copy(y_ref, pos_ref[base + (t * TOP_K + kk)], bufs[s], kk * tm + t,
                          sem.at[s]).start(priority=kk & 1)

    def wait_buf(s):
        pltpu.make_async_copy(y_ref.at[pl.ds(0, n)], bufs[s], sem.at[s]).wait()

    def step(s):
        wait_buf(s)
        gather(jnp.minimum(i + 1, n_tiles - 1), 1 - s)
        gate = gate_ref[...]
        half = y_ref.shape[1]
        ffn_lo = jnp.zeros((tm, half), F32)
        ffn_hi = jnp.zeros((tm, half), F32)
        for kk in range(TOP_K):
            p = bufs[s][kk * tm:(kk + 1) * tm, :]
            gk = gate[:, kk:kk + 1]
            ffn_lo = ffn_lo + gk * lax.bitcast_convert_type(p << 16, F32)
            ffn_hi = ffn_hi + gk * lax.bitcast_convert_type(p & jnp.uint32(0xFFFF0000), F32)
        ffn = jnp.concatenate([ffn_lo, ffn_hi], axis=1)
        o_ref[...] = _layer_norm(alpha * h_ref[...] + ffn, g_ref[...], b_ref[...])

        @pl.when(i == n_tiles - 1)
        def _():
            wait_buf(1 - s)

    @pl.when(i == 0)
    def _():
        gather(0, 0)

    for s in range(2):
        @pl.when((i & 1) == s)
        def _():
            step(s)


def _combine(y, pos, gate, h, ln_g, ln_b, alpha):
    t, d = h.shape
    tm = _pick(t, (128, 64, 32, 16, 8))
    assert y.shape[1] * 2 == d
    return pl.pallas_call(
        functools.partial(_combine_kernel, alpha=alpha),
        grid_spec=pltpu.PrefetchScalarGridSpec(
            num_scalar_prefetch=1,
            grid=(t // tm,),
            in_specs=[pl.BlockSpec(memory_space=pl.ANY),
                      pl.BlockSpec((tm, LANES), lambda i, pos: (i, 0)),
                      pl.BlockSpec((tm, d), lambda i, pos: (i, 0)),
                      pl.BlockSpec((1, d), lambda i, pos: (0, 0)),
                      pl.BlockSpec((1, d), lambda i, pos: (0, 0))],
            out_specs=pl.BlockSpec((tm, d), lambda i, pos: (i, 0)),
            scratch_shapes=[pltpu.VMEM((TOP_K * tm, d // 2), U32), pltpu.VMEM((TOP_K * tm, d // 2), U32),
                            pltpu.SemaphoreType.DMA((2,))],
        ),
        out_shape=jax.ShapeDtypeStruct((t, d), F32),
        compiler_params=_params(("arbitrary",)),
        name="combine",
    )(pos, y, gate, h, ln_g.reshape(1, d), ln_b.reshape(1, d))


def _moe_plan(idx, rank, counts, *, tm, xmax, n_items):
    n_exp = counts.shape[0]
    padded = (counts + tm - 1) // tm * tm
    pstart = jnp.cumsum(padded) - padded
    pos = (pstart[idx] + rank).reshape(-1).astype(I32)
    per_e = (padded + xmax - 1) // xmax
    cum = jnp.cumsum(per_e)
    total = cum[-1]
    wi = jnp.arange(n_items, dtype=I32)
    valid = wi < total
    e_w = jnp.minimum(jnp.searchsorted(cum, jnp.minimum(wi, total - 1), side="right"), n_exp - 1).astype(I32)
    local = jnp.minimum(wi, total - 1) - (cum - per_e)[e_w]
    start = (pstart[e_w] + local * xmax).astype(I32)
    nblk = jnp.where(valid, jnp.clip(padded[e_w] - local * xmax, 0, xmax) // tm, 0).astype(I32)
    start = jnp.concatenate([start, jnp.sum(padded, keepdims=True).astype(I32)])
    return pos, e_w, start, nblk


def _layer(h0, h0b, p, *, bsz, seq, alpha):
    t, d = h0.shape
    w_in = p["w_in"]
    hv = p["gdn_a_log"].shape[0]
    dk = p["gdn_norm_w"].shape[0]
    qkv_dim = p["gdn_conv_w"].shape[1]
    v_dim = hv * dk
    hq = (qkv_dim - v_dim) // (2 * dk)
    scw = p["sc_conv_w"].shape[1]
    assert 2 * hv <= LANES and hv % hq == 0
    n_a = qkv_dim + v_dim
    off_ba = n_a
    off_b = n_a + 2 * hv
    n_b = 3 * scw + 2 * d
    assert w_in.shape[1] == off_b + n_b

    w_in_t = w_in.T
    tm = _pick(t, (1024, 512, 256, 128))
    tn_a = _pick(n_a, (1024, 512, 256, 128))
    (proj_a,) = _matmul(h0b, w_in_t, col0=0, n_cols=n_a, tm=tm, tn=tn_a, out_dtypes=(BF16,), w_t=True)
    tn_b = _pick(n_b, (512, 256, 128))
    (proj_b,) = _matmul(h0b, w_in_t, col0=off_b, n_cols=n_b, tm=tm, tn=tn_b, out_dtypes=(BF16,), w_t=True)
    assert off_ba + LANES <= w_in.shape[1]
    (ba,) = _matmul(h0b, w_in_t, col0=off_ba, n_cols=LANES, tm=tm, tn=LANES, out_dtypes=(F32,), w_t=True)

    gcol, grow = _gdn_gates(ba, p["gdn_a_log"], p["gdn_dt_bias"], hv)
    o_n = _gdn(proj_a, p["gdn_conv_w"], gcol, grow, p["gdn_norm_w"], bsz=bsz, seq=seq, hq=hq, hv=hv, dk=dk)

    u = _short_conv(proj_b, p["sc_conv_w"], bsz=bsz, seq=seq, width=scw)

    tn = _pick(d, (512, 256, 128))
    ga0 = 3 * scw // tn
    gb0 = (3 * scw + d) // tn
    (part_a,) = _matmul(
        o_n, p["w_out_gdn"], col0=0, n_cols=d, tm=tm, tn=tn, out_dtypes=(F32,),
        epilogue=lambda acc, ga: (jax.nn.sigmoid(ga.astype(F32)) * acc,),
        extras=[(proj_b, (tm, tn), lambda j, i: (i, ga0 + j))])
    (merged,) = _matmul(
        u, p["w_out_sc"], col0=0, n_cols=d, tm=tm, tn=tn, out_dtypes=(BF16,),
        epilogue=lambda acc, gb, pa: (pa + jax.nn.sigmoid(gb.astype(F32)) * acc,),
        extras=[(proj_b, (tm, tn), lambda j, i: (i, gb0 + j)),
                (part_a, (tm, tn), lambda j, i: (i, j))])
    tm3 = _pick(t, (256, 128))
    def mix_epilogue(acc, hh, g, b):
        hn = _layer_norm(alpha * hh + acc, g, b)
        return hn, _pack_halves_bf16(hn)

    h1, h1p = _matmul(
        merged, p["w_out"].astype(BF16), col0=0, n_cols=d, tm=tm3, tn=d, out_dtypes=(F32, U32), out_div=(1, 2),
        epilogue=mix_epilogue,
        extras=[(h0, (tm3, d), lambda j, i: (i, 0)),
                (p["ln_mix_g"].reshape(1, d), (1, d), lambda j, i: (0, 0)),
                (p["ln_mix_b"].reshape(1, d), (1, d), lambda j, i: (0, 0))])

    n_exp = p["w_router"].shape[1]
    idx, gate, rank, cnt = _router(h1, p["w_router"], p["b_router"])
    mtm = MOE_TM
    xmax = MOE_XMAX
    n_rows = t * TOP_K + n_exp * mtm
    n_items = n_exp + n_rows // xmax
    pos, item_e, item_start, item_nblk = _moe_plan(
        idx[:, :TOP_K], rank[:, :TOP_K], cnt[0, :n_exp], tm=mtm, xmax=xmax, n_items=n_items)
    rows = _dispatch(h1p, pos, n_rows)
    y = _experts(rows, item_e, item_start, item_nblk, p["w_gate_up"], p["b_gate_up"], p["w_down"], p["b_down"],
                 tm=mtm, xmax=xmax)
    h2 = _combine(y, pos, gate, h1, p["ln_ffn_g"], p["ln_ffn_b"], alpha)
    return h2


_LAYER_PARAMS = ("w_in", "gdn_conv_w", "gdn_a_log", "gdn_dt_bias", "gdn_norm_w", "w_out_gdn", "sc_conv_w",
                 "w_out_sc", "w_out", "ln_mix_g", "ln_mix_b", "w_router", "b_router", "w_gate_up", "b_gate_up",
                 "w_down", "b_down", "ln_ffn_g", "ln_ffn_b")


def kernel(x, ln_in_g, ln_in_b, w_in, gdn_conv_w, gdn_a_log, gdn_dt_bias, gdn_norm_w, w_out_gdn, sc_conv_w,
           w_out_sc, w_out, ln_mix_g, ln_mix_b, w_router, b_router, w_gate_up, b_gate_up, w_down, b_down,
           ln_ffn_g, ln_ffn_b):
    stacked = dict(zip(_LAYER_PARAMS, (w_in, gdn_conv_w, gdn_a_log, gdn_dt_bias, gdn_norm_w, w_out_gdn,
                                       sc_conv_w, w_out_sc, w_out, ln_mix_g, ln_mix_b, w_router, b_router,
                                       w_gate_up, b_gate_up, w_down, b_down, ln_ffn_g, ln_ffn_b)))
    bsz, seq, d = x.shape
    depth = w_in.shape[0]
    alpha = (2 * depth) ** 0.25
    h, hb = _ln_in(x.reshape(bsz * seq, d), ln_in_g, ln_in_b)
    for l in range(depth):
        p = {name: arr[l] for name, arr in stacked.items()}
        h = _layer(h, hb, p, bsz=bsz, seq=seq, alpha=alpha)
        if l + 1 < depth:
            hb = h.astype(BF16)
    return h.reshape(bsz, seq, d)
```

```python
import functools

import jax
import jax.numpy as jnp
from jax import lax
from jax.experimental import pallas as pl
from jax.experimental.pallas import tpu as pltpu

F32 = jnp.float32
BF16 = jnp.bfloat16
I32 = jnp.int32
U32 = jnp.uint32

LANES = 128
SUBLANES = 8
MXU_N = 256
VMEM_LIMIT = 60 << 20

TOP_K = 4
SWIGLU_LIMIT = 7.0
SWIGLU_ALPHA = 1.702
LN_EPS = 1e-5
RMS_EPS = 1e-6
GDN_CHUNK = 64
GDN_STEP = 512
GDN_QK_GROUP = 4
GDN_INV_GROUP = 16
MOE_TM = 256
MOE_XMAX = 1280
MOE_TF = 512


def _params(sem):
    return pltpu.CompilerParams(dimension_semantics=sem, vmem_limit_bytes=VMEM_LIMIT)


def _pick(n, candidates):
    for c in candidates:
        if n % c == 0:
            return c
    raise ValueError(f"no tile for {n} in {candidates}")


def _layer_norm(xf, g, b):
    mu = jnp.mean(xf, axis=-1, keepdims=True)
    xc = xf - mu
    var = jnp.mean(xc * xc, axis=-1, keepdims=True)
    return xc * lax.rsqrt(var + LN_EPS) * g + b


def _pack_halves_bf16(x):
    half = x.shape[1] // 2
    lo = lax.bitcast_convert_type(x[:, :half].astype(BF16).astype(F32), U32) >> 16
    hi = lax.bitcast_convert_type(x[:, half:].astype(BF16).astype(F32), U32) & jnp.uint32(0xFFFF0000)
    return lo | hi


def _unpack_halves_bf16(p):
    lo = lax.bitcast_convert_type(p << 16, F32).astype(BF16)
    hi = lax.bitcast_convert_type(p & jnp.uint32(0xFFFF0000), F32).astype(BF16)
    return lo, hi


def _ln_in_kernel(x_ref, g_ref, b_ref, h_ref, hb_ref):
    h = _layer_norm(x_ref[...], g_ref[...], b_ref[...])
    h_ref[...] = h
    hb_ref[...] = h.astype(BF16)


def _ln_in(x2, g, b):
    t, d = x2.shape
    tm = _pick(t, (512, 256, 128, 64, 32, 16))
    return pl.pallas_call(
        _ln_in_kernel,
        grid=(t // tm,),
        in_specs=[pl.BlockSpec((tm, d), lambda i: (i, 0)),
                  pl.BlockSpec((1, d), lambda i: (0, 0)),
                  pl.BlockSpec((1, d), lambda i: (0, 0))],
        out_specs=[pl.BlockSpec((tm, d), lambda i: (i, 0)),
                   pl.BlockSpec((tm, d), lambda i: (i, 0))],
        out_shape=[jax.ShapeDtypeStruct((t, d), F32), jax.ShapeDtypeStruct((t, d), BF16)],
        compiler_params=_params(("arbitrary",)),
        name="ln_in",
    )(x2, g.reshape(1, d), b.reshape(1, d))


def _mm_kernel(*refs, shift, w_t, n_extra, n_out, epilogue):
    n_w = 2 if shift else 1
    x_ref, w_refs = refs[0], refs[1:1 + n_w]
    extras = refs[1 + n_w:1 + n_w + n_extra]
    o_refs = refs[1 + n_w + n_extra:1 + n_w + n_extra + n_out]
    wb_ref = refs[-1]
    n_axis = 0 if w_t else 1

    @pl.when(pl.program_id(1) == 0)
    def _():
        if shift:
            tn = w_refs[0].shape[n_axis]
            w = jnp.concatenate([lax.slice_in_dim(w_refs[0][...], shift, tn, axis=n_axis),
                                 lax.slice_in_dim(w_refs[1][...], 0, shift, axis=n_axis)], axis=n_axis)
        else:
            w = w_refs[0][...]
        wb_ref[...] = w.astype(BF16)

    acc = lax.dot_general(x_ref[...], wb_ref[...], (((1,), (1 if w_t else 0,)), ((), ())),
                          preferred_element_type=F32)
    outs = epilogue(acc, *[e[...] for e in extras])
    for o_ref, o in zip(o_refs, outs):
        o_ref[...] = o.astype(o_ref.dtype)


def _matmul(x, w, *, col0, n_cols, tm, tn, out_dtypes, epilogue=None, extras=(), out_div=None, w_t=False):
    t, k = x.shape
    k_axis, n_axis = (1, 0) if w_t else (0, 1)
    assert w.shape[k_axis] == k and n_cols % tn == 0 and t % tm == 0 and col0 + n_cols <= w.shape[n_axis]
    j0, shift = divmod(col0, tn)
    if epilogue is None:
        epilogue = lambda acc: (acc,)
    if out_div is None:
        out_div = (1,) * len(out_dtypes)
    kern = functools.partial(_mm_kernel, shift=shift, w_t=w_t, n_extra=len(extras), n_out=len(out_dtypes),
                             epilogue=epilogue)
    w_block = (tn, k) if w_t else (k, tn)
    w_index = lambda jj: (jj, 0) if w_t else (0, jj)
    in_specs = [pl.BlockSpec((tm, k), lambda j, i: (i, 0)),
                pl.BlockSpec(w_block, lambda j, i: w_index(j + j0))]
    ws = [w]
    if shift:
        in_specs.append(pl.BlockSpec(w_block, lambda j, i: w_index(j + j0 + 1)))
        ws.append(w)
    in_specs += [pl.BlockSpec(bs, im) for (_, bs, im) in extras]
    return pl.pallas_call(
        kern,
        grid=(n_cols // tn, t // tm),
        in_specs=in_specs,
        out_specs=[pl.BlockSpec((tm, tn // dv), lambda j, i: (i, j)) for dv in out_div],
        out_shape=[jax.ShapeDtypeStruct((t, n_cols // dv), dt) for dt, dv in zip(out_dtypes, out_div)],
        scratch_shapes=[pltpu.VMEM(w_block, BF16)],
        compiler_params=_params(("arbitrary", "arbitrary")),
        name="matmul",
    )(x, *ws, *[a for (a, _, _) in extras])


def _gates_kernel(ba_ref, alog_ref, dt_ref, col_ref, row_ref, *, hv):
    x = ba_ref[...]
    lane = lax.broadcasted_iota(I32, x.shape, 1)
    beta = jax.nn.sigmoid(x)
    xs = x + dt_ref[...]
    softplus = jnp.maximum(xs, 0.0) + jnp.log(1.0 + jnp.exp(-jnp.abs(xs)))
    g = -jnp.exp(alog_ref[...]) * softplus
    g = jnp.where((lane >= hv) & (lane < 2 * hv), g, 0.0)
    n = x.shape[0]
    row = lax.broadcasted_iota(I32, x.shape, 0) & (GDN_CHUNK - 1)
    cum = g
    rev = g
    sh = 1
    while sh < GDN_CHUNK:
        cum = cum + jnp.where(row >= sh, pltpu.roll(cum, sh, 0), 0.0)
        rev = rev + jnp.where(row < GDN_CHUNK - sh, pltpu.roll(rev, n - sh, 0), 0.0)
        sh *= 2
    e_cum = pltpu.roll(jnp.exp(cum), hv, 1)
    e_rest = pltpu.roll(jnp.exp(rev - g), 2 * hv, 1)
    out = jnp.where(lane < hv, beta, jnp.where(lane < 2 * hv, cum, jnp.where(lane < 3 * hv, e_cum, e_rest)))
    col_ref[...] = out
    row_ref[...] = out.T


def _gdn_gates(ba, a_log, dt_bias, hv):
    assert 4 * hv <= LANES
    t = ba.shape[0]
    tm = _pick(t, (512, 256, 128))
    pad = lambda v: jnp.zeros((1, LANES), F32).at[0, hv:2 * hv].set(v.astype(F32))
    return pl.pallas_call(
        functools.partial(_gates_kernel, hv=hv),
        grid=(t // tm,),
        in_specs=[pl.BlockSpec((tm, LANES), lambda i: (i, 0)),
                  pl.BlockSpec((1, LANES), lambda i: (0, 0)),
                  pl.BlockSpec((1, LANES), lambda i: (0, 0))],
        out_specs=[pl.BlockSpec((tm, LANES), lambda i: (i, 0)),
                   pl.BlockSpec((LANES, tm), lambda i: (0, i))],
        out_shape=[jax.ShapeDtypeStruct((t, LANES), F32), jax.ShapeDtypeStruct((LANES, t), F32)],
        compiler_params=_params(("arbitrary",)),
        name="gdn_gates",
    )(ba, pad(a_log), pad(dt_bias))


def _causal_conv(x, w, ext_ref):
    n = x.shape[0]
    kw = w.shape[0]
    ext_ref[SUBLANES:, :] = x
    acc = x * w[kw - 1:kw]
    for d in range(1, kw):
        acc = acc + ext_ref[SUBLANES - d:SUBLANES - d + n, :] * w[kw - 1 - d:kw - d]
    ext_ref[:SUBLANES, :] = x[n - SUBLANES:]
    return acc


def _silu(x):
    h = 0.5 * x
    return h + h * jnp.tanh(h)


def _unit_lower_inverses(lows, ii, jj):
    c = lows[0].shape[0]
    eye = jnp.where(ii == jj, 1.0, 0.0)
    pair = (ii >> 1) == (jj >> 1)
    xbs = [(eye - jnp.where(pair, low, 0.0)).astype(BF16) for low in lows]
    lbs = [low.astype(BF16) for low in lows]
    zero = jnp.zeros((c, c), BF16)
    s = 1
    while (2 << s) <= c:
        m = ((ii >> (s + 1)) == (jj >> (s + 1))) & ((ii >> s) != (jj >> s))
        ts = [jnp.dot(jnp.where(m, lb, zero), xb, preferred_element_type=F32) for lb, xb in zip(lbs, xbs)]
        xbs = [jnp.dot(xb, (eye - t).astype(BF16), preferred_element_type=F32).astype(BF16)
               for xb, t in zip(xbs, ts)]
        s += 1
    return xbs


def _gdn_kernel(q_ref, k_ref, v_ref, z_ref, cwq_ref, cwk_ref, cwv_ref, gcol_ref, grow_ref, nw_ref,
                o_ref, s_ref, tq_ref, tk_ref, tv_ref, qn_ref, kn_ref, vn_ref, u_ref, wq_ref, attn_ref, kd_ref,
                *, nq, vper, dk):
    @pl.when(pl.program_id(2) == 0)
    def _():
        s_ref[...] = jnp.zeros_like(s_ref)
        for t_ref in (tq_ref, tk_ref, tv_ref):
            t_ref[:SUBLANES, :] = jnp.zeros((SUBLANES, t_ref.shape[1]), F32)

    tb = q_ref.shape[0]
    c = GDN_CHUNK
    nc = tb // c
    nv = nq * vper
    head = lambda h: slice(h * dk, (h + 1) * dk)

    q = _silu(_causal_conv(q_ref[...].astype(F32), cwq_ref[...], tq_ref))
    k = _silu(_causal_conv(k_ref[...].astype(F32), cwk_ref[...], tk_ref))
    for hq in range(nq):
        qh = q[:, head(hq)]
        kh = k[:, head(hq)]
        qn_ref[:, head(hq)] = qh * (lax.rsqrt(jnp.sum(qh * qh, axis=-1, keepdims=True) + RMS_EPS) * (dk ** -0.5))
        kn_ref[:, head(hq)] = kh * lax.rsqrt(jnp.sum(kh * kh, axis=-1, keepdims=True) + RMS_EPS)

    cols = gcol_ref[...]
    grows = grow_ref[...]
    nw = nw_ref[...]
    ii = lax.broadcasted_iota(I32, (c, c), 0)
    jj = lax.broadcasted_iota(I32, (c, c), 1)
    incl = ii >= jj
    strict = ii > jj

    def col(ci, which, h):
        return cols[ci * c:(ci + 1) * c, which * nv + h:which * nv + h + 1]

    def state_step(ci):
        rows = slice(ci * c, (ci + 1) * c)
        ps = [ci * nv + h for h in range(nv)]
        r1 = [jnp.dot(wq_ref[p], s_ref[h].astype(BF16), preferred_element_type=F32) for h, p in enumerate(ps)]
        v_new = [(u_ref[p] - r[:c]).astype(BF16) for p, r in zip(ps, r1)]
        outs = [r[c:] + jnp.dot(attn_ref[p], vn, preferred_element_type=F32) for p, r, vn in zip(ps, r1, v_new)]
        for h, (p, vn) in enumerate(zip(ps, v_new)):
            g_tot = cols[(ci + 1) * c - 1:(ci + 1) * c, 2 * nv + h:2 * nv + h + 1]
            s_ref[h] = s_ref[h] * g_tot + lax.dot_general(
                kd_ref[p], vn, (((0,), (0,)), ((), ())), preferred_element_type=F32)
        for h, o in enumerate(outs):
            zc = z_ref[rows, head(h)].astype(F32)
            var = jnp.mean(o * o, axis=-1, keepdims=True)
            o_ref[rows, head(h)] = (o * lax.rsqrt(var + RMS_EPS) * nw * _silu(zc)).astype(BF16)

    problems = [(ci, h) for ci in range(nc) for h in range(nv)]
    gram = {}
    chunks_done = 0
    for g0 in range(0, len(problems), GDN_INV_GROUP):
        group = problems[g0:g0 + GDN_INV_GROUP]
        lows = []
        for ci, h in group:
            hq = h // vper
            rows = slice(ci * c, (ci + 1) * c)
            if (ci, hq) not in gram:
                kcb = kn_ref[rows, head(hq)].astype(BF16)
                qcb = qn_ref[rows, head(hq)].astype(BF16)
                gram[(ci, hq)] = lax.dot_general(jnp.concatenate([kcb, qcb], axis=0), kcb,
                                                 (((1,), (1,)), ((), ())), preferred_element_type=F32)
            a = gram[(ci, hq)]
            grow = grows[h:h + 1, rows]
            decay = jnp.where(incl, jnp.exp(jnp.minimum(col(ci, 1, h) - grow, 0.0)), 0.0)
            lows.append(jnp.where(strict, col(ci, 0, h) * a[:c] * decay, 0.0))
            attn_ref[ci * nv + h] = jnp.where(incl, a[c:] * decay, 0.0).astype(BF16)
        if g0 == 0:
            vn_ref[...] = _silu(_causal_conv(v_ref[...].astype(F32), cwv_ref[...], tv_ref))
        xs = _unit_lower_inverses(lows, ii, jj)
        for x, (ci, h) in zip(xs, group):
            hq = h // vper
            rows = slice(ci * c, (ci + 1) * c)
            p = ci * nv + h
            kc = kn_ref[rows, head(hq)]
            bcol = col(ci, 0, h)
            eg = col(ci, 2, h)
            rhs = jnp.concatenate([vn_ref[rows, head(h)] * bcol, kc * (bcol * eg)], axis=1).astype(BF16)
            uw = jnp.dot(x, rhs, preferred_element_type=F32)
            u_ref[p] = uw[:, :dk]
            wq_ref[p, :c, :] = uw[:, dk:].astype(BF16)
            wq_ref[p, c:, :] = (qn_ref[rows, head(hq)] * eg).astype(BF16)
            kd_ref[p] = (kc * col(ci, 3, h)).astype(BF16)
        chunks_ready = (g0 + len(group)) // nv
        for ci in range(chunks_done, chunks_ready):
            state_step(ci)
        chunks_done = chunks_ready


def _gdn(proj_a, conv_w, gcol, grow, norm_w, *, bsz, seq, hq, hv, dk):
    t = bsz * seq
    vper = hv // hq
    nq = _pick(hq, (GDN_QK_GROUP, 2, 1))
    nv = nq * vper
    ng = hq // nq
    tb = _pick(seq, (GDN_STEP, 128, 64))
    ns = seq // tb
    nprob = (tb // GDN_CHUNK) * nv
    qw = nq * dk
    vw = nv * dk
    v_blk0 = 2 * hq * dk // vw
    z_blk0 = (2 * hq * dk + hv * dk) // vw
    kw = conv_w.shape[0]
    gcol = gcol[:, :4 * hv].reshape(t, 4, ng, nv).transpose(2, 0, 1, 3).reshape(ng, t, 4 * nv)
    grow = grow[hv:2 * hv].reshape(ng, nv, t)
    rowblk = lambda b, g, s: b * ns + s
    kern = functools.partial(_gdn_kernel, nq=nq, vper=vper, dk=dk)
    return pl.pallas_call(
        kern,
        grid=(bsz, ng, ns),
        in_specs=[
            pl.BlockSpec((tb, qw), lambda b, g, s: (rowblk(b, g, s), g)),
            pl.BlockSpec((tb, qw), lambda b, g, s: (rowblk(b, g, s), ng + g)),
            pl.BlockSpec((tb, vw), lambda b, g, s: (rowblk(b, g, s), v_blk0 + g)),
            pl.BlockSpec((tb, vw), lambda b, g, s: (rowblk(b, g, s), z_blk0 + g)),
            pl.BlockSpec((kw, qw), lambda b, g, s: (0, g)),
            pl.BlockSpec((kw, qw), lambda b, g, s: (0, ng + g)),
            pl.BlockSpec((kw, vw), lambda b, g, s: (0, v_blk0 + g)),
            pl.BlockSpec((None, tb, 4 * nv), lambda b, g, s: (g, rowblk(b, g, s), 0)),
            pl.BlockSpec((None, nv, tb), lambda b, g, s: (g, 0, rowblk(b, g, s))),
            pl.BlockSpec((1, dk), lambda b, g, s: (0, 0)),
        ],
        out_specs=pl.BlockSpec((tb, vw), lambda b, g, s: (rowblk(b, g, s), g)),
        out_shape=jax.ShapeDtypeStruct((t, hv * dk), BF16),
        scratch_shapes=[pltpu.VMEM((nv, dk, dk), F32),
                        pltpu.VMEM((tb + SUBLANES, qw), F32),
                        pltpu.VMEM((tb + SUBLANES, qw), F32),
                        pltpu.VMEM((tb + SUBLANES, vw), F32),
                        pltpu.VMEM((tb, qw), F32),
                        pltpu.VMEM((tb, qw), F32),
                        pltpu.VMEM((tb, vw), F32),
                        pltpu.VMEM((nprob, GDN_CHUNK, dk), F32),
                        pltpu.VMEM((nprob, 2 * GDN_CHUNK, dk), BF16),
                        pltpu.VMEM((nprob, GDN_CHUNK, GDN_CHUNK), BF16),
                        pltpu.VMEM((nprob, GDN_CHUNK, dk), BF16)],
        compiler_params=_params(("arbitrary", "arbitrary", "arbitrary")),
        name="gdn",
    )(proj_a, proj_a, proj_a, proj_a, conv_w, conv_w, conv_w, gcol, grow, norm_w.reshape(1, dk))


def _sc_kernel(b_ref, c_ref, x_ref, w_ref, u_ref, tail_ref):
    @pl.when(pl.program_id(1) == 0)
    def _():
        tail_ref[:SUBLANES, :] = jnp.zeros((SUBLANES, tail_ref.shape[1]), F32)

    p = c_ref[...].astype(F32) * x_ref[...].astype(F32)
    u_ref[...] = (b_ref[...].astype(F32) * _causal_conv(p, w_ref[...], tail_ref)).astype(BF16)


def _short_conv(proj_b, conv_w, *, bsz, seq, width):
    t = bsz * seq
    ts = _pick(seq, (512, 256, 128, 64))
    ns = seq // ts
    kw = conv_w.shape[0]
    return pl.pallas_call(
        _sc_kernel,
        grid=(bsz, ns),
        in_specs=[pl.BlockSpec((ts, width), lambda b, s: (b * ns + s, 0)),
                  pl.BlockSpec((ts, width), lambda b, s: (b * ns + s, 1)),
                  pl.BlockSpec((ts, width), lambda b, s: (b * ns + s, 2)),
                  pl.BlockSpec((kw, width), lambda b, s: (0, 0))],
        out_specs=pl.BlockSpec((ts, width), lambda b, s: (b * ns + s, 0)),
        out_shape=jax.ShapeDtypeStruct((t, width), BF16),
        scratch_shapes=[pltpu.VMEM((ts + SUBLANES, width), F32)],
        compiler_params=_params(("arbitrary", "arbitrary")),
        name="short_conv",
    )(proj_b, proj_b, proj_b, conv_w)


def _router_kernel(h_ref, w_ref, b_ref, idx_ref, gate_ref, rank_ref, cnt_ref, run_ref):
    @pl.when(pl.program_id(0) == 0)
    def _():
        run_ref[...] = jnp.zeros_like(run_ref)

    tm = h_ref.shape[0]
    h = h_ref[...]
    h_hi = h.astype(BF16)
    h_lo = (h - h_hi.astype(F32)).astype(BF16)
    w2 = w_ref[...]
    first = jnp.dot(h_hi, w2, preferred_element_type=F32)
    logits = (first[:, :LANES] + first[:, LANES:]
              + jnp.dot(h_lo, w2[:, :LANES], preferred_element_type=F32) + b_ref[...])
    lane = lax.broadcasted_iota(I32, logits.shape, 1)
    lane_f = lane.astype(F32)
    cur = logits
    idxs, vals = [], []
    for _ in range(TOP_K):
        m = jnp.max(cur, axis=-1, keepdims=True)
        ix = jnp.min(jnp.where(cur == m, lane_f, float(LANES)), axis=-1, keepdims=True).astype(I32)
        idxs.append(ix)
        vals.append(m)
        cur = jnp.where(lane == ix, -jnp.inf, cur)
    es = [jnp.exp(val - vals[0]) for val in vals]
    den = es[0]
    for e in es[1:]:
        den = den + e
    onehot = jnp.zeros(logits.shape, F32)
    for ix in idxs:
        onehot = onehot + jnp.where(lane == ix, 1.0, 0.0)
    ri = lax.broadcasted_iota(I32, (tm, tm), 0)
    ci = lax.broadcasted_iota(I32, (tm, tm), 1)
    tri = jnp.where(ri > ci, 1.0, 0.0).astype(BF16)
    before = jnp.dot(tri, onehot.astype(BF16), preferred_element_type=F32) + run_ref[...]
    idx_out = jnp.zeros(logits.shape, I32)
    rank_out = jnp.zeros(logits.shape, I32)
    gate_out = jnp.zeros(logits.shape, F32)
    for kk in range(TOP_K):
        rank = jnp.sum(jnp.where(lane == idxs[kk], before, 0.0), axis=-1, keepdims=True).astype(I32)
        idx_out = jnp.where(lane == kk, idxs[kk], idx_out)
        rank_out = jnp.where(lane == kk, rank, rank_out)
        gate_out = jnp.where(lane == kk, es[kk] / den, gate_out)
    idx_ref[...] = idx_out
    rank_ref[...] = rank_out
    gate_ref[...] = gate_out
    run_ref[...] = run_ref[...] + jnp.sum(onehot, axis=0, keepdims=True)
    cnt_ref[...] = run_ref[...].astype(I32)


def _router(h, w_router, b_router):
    t, d = h.shape
    e = w_router.shape[1]
    tm = _pick(t, (256, 128, 64, 32, 16, 8))
    wp = jnp.zeros((d, LANES), F32).at[:, :e].set(w_router)
    w_hi = wp.astype(BF16)
    w_lo = (wp - w_hi.astype(F32)).astype(BF16)
    w2 = jnp.concatenate([w_hi, w_lo], axis=1)
    bp = jnp.full((1, LANES), -1e30, F32).at[0, :e].set(b_router)
    tile = pl.BlockSpec((tm, LANES), lambda i: (i, 0))
    return pl.pallas_call(
        _router_kernel,
        grid=(t // tm,),
        in_specs=[pl.BlockSpec((tm, d), lambda i: (i, 0)),
                  pl.BlockSpec((d, 2 * LANES), lambda i: (0, 0)),
                  pl.BlockSpec((1, LANES), lambda i: (0, 0))],
        out_specs=[tile, tile, tile, pl.BlockSpec((1, LANES), lambda i: (0, 0))],
        out_shape=[jax.ShapeDtypeStruct((t, LANES), I32), jax.ShapeDtypeStruct((t, LANES), F32),
                   jax.ShapeDtypeStruct((t, LANES), I32), jax.ShapeDtypeStruct((1, LANES), I32)],
        scratch_shapes=[pltpu.VMEM((1, LANES), F32)],
        compiler_params=_params(("arbitrary",)),
        name="router",
    )(h, w2, bp)


def _row_copy(src, s_row, dst, d_row, sem):
    return pltpu.make_async_copy(src.at[pl.ds(s_row, 1)], dst.at[pl.ds(d_row, 1)], sem)


def _dispatch_kernel(pos_ref, x_ref, rows_in_ref, rows_ref, sem):
    del rows_in_ref
    tm = x_ref.shape[0]
    base = pl.program_id(0) * (tm * TOP_K)

    for t in range(tm):
        for kk in range(TOP_K):
            _row_copy(x_ref, t, rows_ref, pos_ref[base + (t * TOP_K + kk)], sem).start(priority=kk & 1)
    n = tm * TOP_K
    pltpu.make_async_copy(rows_ref.at[pl.ds(0, n)], rows_ref.at[pl.ds(0, n)], sem).wait()


def _dispatch(h, pos, n_rows):
    t, d = h.shape
    tm = _pick(t, (128, 64, 32, 16, 8))
    rows0 = jnp.zeros((n_rows, d), h.dtype)
    return pl.pallas_call(
        _dispatch_kernel,
        grid_spec=pltpu.PrefetchScalarGridSpec(
            num_scalar_prefetch=1,
            grid=(t // tm,),
            in_specs=[pl.BlockSpec((tm, d), lambda i, pos: (i, 0)),
                      pl.BlockSpec(memory_space=pl.ANY)],
            out_specs=pl.BlockSpec(memory_space=pl.ANY),
            scratch_shapes=[pltpu.SemaphoreType.DMA(())],
        ),
        out_shape=jax.ShapeDtypeStruct((n_rows, d), h.dtype),
        input_output_aliases={2: 0},
        compiler_params=_params(("arbitrary",)),
        name="dispatch",
    )(pos, h, rows0)


def _swiglu_interleaved(gu):
    lane = lax.broadcasted_iota(I32, (gu.shape[0], LANES), 1)
    even = (lane & 1) == 0
    parts = []
    for c in range(gu.shape[1] // LANES):
        g = gu[:, c * LANES:(c + 1) * LANES]
        gt = jnp.minimum(g, SWIGLU_LIMIT)
        glu = gt * jax.nn.sigmoid(SWIGLU_ALPHA * gt)
        up = jnp.clip(g, -SWIGLU_LIMIT, SWIGLU_LIMIT) + 1.0
        parts.append(jnp.where(even, pltpu.roll(up, LANES - 1, 1) * glu, 0.0))
    outs = [parts[2 * m] + pltpu.roll(parts[2 * m + 1], 1, 1) for m in range(len(parts) // 2)]
    return jnp.concatenate(outs, axis=1) if len(outs) > 1 else outs[0]


def _interleave_rows_bf16(w_ref, out_ref):
    half = LANES // 2
    for m in range(w_ref.shape[0] // LANES):
        lo = w_ref[m * LANES:m * LANES + half, :].astype(BF16).astype(F32)
        hi = w_ref[m * LANES + half:(m + 1) * LANES, :].astype(BF16).astype(F32)
        packed = (lax.bitcast_convert_type(lo, jnp.uint32) >> 16) | (
            lax.bitcast_convert_type(hi, jnp.uint32) & jnp.uint32(0xFFFF0000))
        out_ref[m * LANES:(m + 1) * LANES, :] = pltpu.bitcast(packed, BF16)


def _expert_kernel(ie_ref, ist_ref, inb_ref, rows_ref, wgu_ref, bgu_ref, wd_ref, bd_ref, y_ref,
                   xb_ref, acc_ref, stage_ref, wgub_ref, wdb_ref, ost_ref, sem_in, sem_out, *, tm, n_j):
    del ie_ref
    wi = pl.program_id(0)
    j = pl.program_id(1)
    n_items = pl.num_programs(0)
    nblk = inb_ref[wi]
    half = stage_ref.shape[1]

    def block(b):
        return pl.ds(pl.multiple_of(b * tm, tm), tm)

    def hbm_block(item, b):
        return pl.ds(pl.multiple_of(ist_ref[item] + b * tm, tm), tm)

    def rows_copy(item, b):
        return pltpu.make_async_copy(rows_ref.at[hbm_block(item, b)], stage_ref.at[block(b)], sem_in)

    def y_copy(item, b):
        slot = b & 1
        return pltpu.make_async_copy(ost_ref.at[slot], y_ref.at[hbm_block(item, b)], sem_out.at[slot])

    def drain_stores(item):
        n = inb_ref[item]
        for back in (2, 1):
            @pl.when(n >= back)
            def _():
                y_copy(item, n - back).wait()

    def for_blocks(item, fn):
        def body(b, carry):
            fn(item, b)
            return carry
        lax.fori_loop(0, inb_ref[item], body, 0)

    def unpack(item, b):
        lo, hi = _unpack_halves_bf16(stage_ref[block(b), :])
        xb_ref[block(b), :half] = lo
        xb_ref[block(b), half:] = hi

    @pl.when(j == 0)
    def _():
        @pl.when(wi == 0)
        def _():
            for_blocks(0, lambda it, b: rows_copy(it, b).start())
            acc_ref[...] = jnp.zeros_like(acc_ref)

        @pl.when(wi > 0)
        def _():
            drain_stores(wi - 1)

        for_blocks(wi, lambda it, b: rows_copy(it, b).wait())
        for_blocks(wi, unpack)

        @pl.when(wi + 1 < n_items)
        def _():
            for_blocks(wi + 1, lambda it, b: rows_copy(it, b).start())

    @pl.when(nblk > 0)
    def _():
        bias = bgu_ref[...]
        n_ct, _, ct = wgub_ref.shape

        def convert_weights():
            for c in range(n_ct):
                wgub_ref[c] = wgu_ref[:, c * ct:(c + 1) * ct].astype(BF16)
            _interleave_rows_bf16(wd_ref, wdb_ref)

        def mlp(b):
            x = xb_ref[block(b), :]
            gu = jnp.concatenate([jnp.dot(x, wgub_ref[c], preferred_element_type=F32) for c in range(n_ct)],
                                 axis=1) + bias
            act = _swiglu_interleaved(gu).astype(BF16)
            return jnp.dot(act, wdb_ref[...], preferred_element_type=F32)

        def accumulate(b):
            prev = jnp.where(j == 0, jnp.broadcast_to(bd_ref[...], (tm, acc_ref.shape[1])), acc_ref[block(b), :])
            acc_ref[block(b), :] = prev + mlp(b)

        def finish(b):
            prev = acc_ref[block(b), :] if n_j > 1 else bd_ref[...]
            ost_ref[b & 1] = _pack_halves_bf16(prev + mlp(b))
            y_copy(wi, b).start()

        def run(fn, stores):
            @pl.when(nblk >= 2)
            def _():
                convert_weights()
                fn(0)
                fn(1)

            @pl.when(nblk == 1)
            def _():
                convert_weights()
                fn(0)

            def pair(i, carry):
                if stores:
                    y_copy(wi, 2 * i - 2).wait()
                    y_copy(wi, 2 * i - 1).wait()
                fn(2 * i)
                fn(2 * i + 1)
                return carry
            lax.fori_loop(1, nblk // 2, pair, 0)

            @pl.when(((nblk & 1) == 1) & (nblk >= 3))
            def _():
                if stores:
                    y_copy(wi, nblk - 3).wait()
                fn(nblk - 1)

        if n_j > 1:
            @pl.when(j < n_j - 1)
            def _():
                run(accumulate, False)

        @pl.when(j == n_j - 1)
        def _():
            run(finish, True)

    @pl.when((wi == n_items - 1) & (j == n_j - 1))
    def _():
        drain_stores(wi)
        ost_ref[0] = jnp.zeros(ost_ref.shape[1:], U32)

        def fill(b, carry):
            cp = pltpu.make_async_copy(ost_ref.at[0], y_ref.at[pl.ds(pl.multiple_of(b * tm, tm), tm)],
                                       sem_out.at[0])
            cp.start()
            cp.wait()
            return carry
        lax.fori_loop(ist_ref[n_items] // tm, y_ref.shape[0] // tm, fill, 0)


def _experts(rows, item_e, item_start, item_nblk, w_gate_up, b_gate_up, w_down, b_down, *, tm, xmax):
    n_rows = rows.shape[0]
    n_exp, d, f2 = w_gate_up.shape
    assert rows.shape[1] * 2 == d
    f = f2 // 2
    tf = _pick(f, (MOE_TF, LANES))
    n_j = f // tf
    ct = min(2 * tf, MXU_N)
    n_items = item_e.shape[0]

    def jeff(j, inb, wi):
        return jnp.where(inb[wi] > 0, j, n_j - 1)

    kern = functools.partial(_expert_kernel, tm=tm, n_j=n_j)
    return pl.pallas_call(
        kern,
        grid_spec=pltpu.PrefetchScalarGridSpec(
            num_scalar_prefetch=3,
            grid=(n_items, n_j),
            in_specs=[
                pl.BlockSpec(memory_space=pl.ANY),
                pl.BlockSpec((None, d, 2 * tf), lambda wi, j, ie, ist, inb: (ie[wi], 0, jeff(j, inb, wi))),
                pl.BlockSpec((None, 1, 2 * tf), lambda wi, j, ie, ist, inb: (ie[wi], 0, jeff(j, inb, wi))),
                pl.BlockSpec((None, tf, d), lambda wi, j, ie, ist, inb: (ie[wi], jeff(j, inb, wi), 0)),
                pl.BlockSpec((None, 1, d), lambda wi, j, ie, ist, inb: (ie[wi], 0, 0)),
            ],
            out_specs=pl.BlockSpec(memory_space=pl.ANY),
            scratch_shapes=[pltpu.VMEM((xmax, d), BF16),
                            pltpu.VMEM((xmax, d), F32),
                            pltpu.VMEM((xmax, d // 2), U32),
                            pltpu.VMEM((2 * tf // ct, d, ct), BF16),
                            pltpu.VMEM((tf, d), BF16),
                            pltpu.VMEM((2, tm, d // 2), U32),
                            pltpu.SemaphoreType.DMA(()),
                            pltpu.SemaphoreType.DMA((2,))],
        ),
        out_shape=jax.ShapeDtypeStruct((n_rows, d // 2), U32),
        compiler_params=_params(("arbitrary", "arbitrary")),
        name="experts",
    )(item_e, item_start, item_nblk, rows, w_gate_up, b_gate_up.reshape(n_exp, 1, f2),
      w_down, b_down.reshape(n_exp, 1, d))


def _combine_kernel(pos_ref, y_ref, gate_ref, h_ref, g_ref, b_ref, o_ref, ybuf0_ref, ybuf1_ref, sem,
                    *, alpha):
    tm = h_ref.shape[0]
    i = pl.program_id(0)
    n_tiles = pl.num_programs(0)
    n = tm * TOP_K
    bufs = (ybuf0_ref, ybuf1_ref)

    def gather(tile, s):
        base = tile * n
        for t in range(tm):
            for kk in range(TOP_K):
                _row_copy(y_ref, pos_ref[base + (t * TOP_K + kk)], bufs[s], kk * tm + t,
                          sem.at[s]).start(priority=kk & 1)

    def wait_buf(s):
        pltpu.make_async_copy(y_ref.at[pl.ds(0, n)], bufs[s], sem.at[s]).wait()

    def step(s):
        wait_buf(s)
        gather(jnp.minimum(i + 1, n_tiles - 1), 1 - s)
        gate = gate_ref[...]
        half = y_ref.shape[1]
        ffn_lo = jnp.zeros((tm, half), F32)
        ffn_hi = jnp.zeros((tm, half), F32)
        for kk in range(TOP_K):
            p = bufs[s][kk * tm:(kk + 1) * tm, :]
            gk = gate[:, kk:kk + 1]
            ffn_lo = ffn_lo + gk * lax.bitcast_convert_type(p << 16, F32)
            ffn_hi = ffn_hi + gk * lax.bitcast_convert_type(p & jnp.uint32(0xFFFF0000), F32)
        ffn = jnp.concatenate([ffn_lo, ffn_hi], axis=1)
        o_ref[...] = _layer_norm(alpha * h_ref[...] + ffn, g_ref[...], b_ref[...])

        @pl.when(i == n_tiles - 1)
        def _():
            wait_buf(1 - s)

    @pl.when(i == 0)
    def _():
        gather(0, 0)

    for s in range(2):
        @pl.when((i & 1) == s)
        def _():
            step(s)


def _combine(y, pos, gate, h, ln_g, ln_b, alpha):
    t, d = h.shape
    tm = _pick(t, (128, 64, 32, 16, 8))
    assert y.shape[1] * 2 == d
    return pl.pallas_call(
        functools.partial(_combine_kernel, alpha=alpha),
        grid_spec=pltpu.PrefetchScalarGridSpec(
            num_scalar_prefetch=1,
            grid=(t // tm,),
            in_specs=[pl.BlockSpec(memory_space=pl.ANY),
                      pl.BlockSpec((tm, LANES), lambda i, pos: (i, 0)),
                      pl.BlockSpec((tm, d), lambda i, pos: (i, 0)),
                      pl.BlockSpec((1, d), lambda i, pos: (0, 0)),
                      pl.BlockSpec((1, d), lambda i, pos: (0, 0))],
            out_specs=pl.BlockSpec((tm, d), lambda i, pos: (i, 0)),
            scratch_shapes=[pltpu.VMEM((TOP_K * tm, d // 2), U32), pltpu.VMEM((TOP_K * tm, d // 2), U32),
                            pltpu.SemaphoreType.DMA((2,))],
        ),
        out_shape=jax.ShapeDtypeStruct((t, d), F32),
        compiler_params=_params(("arbitrary",)),
        name="combine",
    )(pos, y, gate, h, ln_g.reshape(1, d), ln_b.reshape(1, d))


def _moe_plan(idx, rank, counts, *, tm, xmax, n_items):
    n_exp = counts.shape[0]
    padded = (counts + tm - 1) // tm * tm
    pstart = jnp.cumsum(padded) - padded
    pos = (pstart[idx] + rank).reshape(-1).astype(I32)
    per_e = (padded + xmax - 1) // xmax
    cum = jnp.cumsum(per_e)
    total = cum[-1]
    wi = jnp.arange(n_items, dtype=I32)
    valid = wi < total
    e_w = jnp.minimum(jnp.searchsorted(cum, jnp.minimum(wi, total - 1), side="right"), n_exp - 1).astype(I32)
    local = jnp.minimum(wi, total - 1) - (cum - per_e)[e_w]
    start = (pstart[e_w] + local * xmax).astype(I32)
    nblk = jnp.where(valid, jnp.clip(padded[e_w] - local * xmax, 0, xmax) // tm, 0).astype(I32)
    start = jnp.concatenate([start, jnp.sum(padded, keepdims=True).astype(I32)])
    return pos, e_w, start, nblk


def _layer(h0, h0b, p, *, bsz, seq, alpha):
    t, d = h0.shape
    w_in = p["w_in"]
    hv = p["gdn_a_log"].shape[0]
    dk = p["gdn_norm_w"].shape[0]
    qkv_dim = p["gdn_conv_w"].shape[1]
    v_dim = hv * dk
    hq = (qkv_dim - v_dim) // (2 * dk)
    scw = p["sc_conv_w"].shape[1]
    assert 2 * hv <= LANES and hv % hq == 0
    n_a = qkv_dim + v_dim
    off_ba = n_a
    off_b = n_a + 2 * hv
    n_b = 3 * scw + 2 * d
    assert w_in.shape[1] == off_b + n_b

    w_in_t = w_in.T
    tm = _pick(t, (1024, 512, 256, 128))
    tn_a = _pick(n_a, (1024, 512, 256, 128))
    (proj_a,) = _matmul(h0b, w_in_t, col0=0, n_cols=n_a, tm=tm, tn=tn_a, out_dtypes=(BF16,), w_t=True)
    tn_b = _pick(n_b, (512, 256, 128))
    (proj_b,) = _matmul(h0b, w_in_t, col0=off_b, n_cols=n_b, tm=tm, tn=tn_b, out_dtypes=(BF16,), w_t=True)
    assert off_ba + LANES <= w_in.shape[1]
    (ba,) = _matmul(h0b, w_in_t, col0=off_ba, n_cols=LANES, tm=tm, tn=LANES, out_dtypes=(F32,), w_t=True)

    gcol, grow = _gdn_gates(ba, p["gdn_a_log"], p["gdn_dt_bias"], hv)
    o_n = _gdn(proj_a, p["gdn_conv_w"], gcol, grow, p["gdn_norm_w"], bsz=bsz, seq=seq, hq=hq, hv=hv, dk=dk)

    u = _short_conv(proj_b, p["sc_conv_w"], bsz=bsz, seq=seq, width=scw)

    tn = _pick(d, (512, 256, 128))
    ga0 = 3 * scw // tn
    gb0 = (3 * scw + d) // tn
    (part_a,) = _matmul(
        o_n, p["w_out_gdn"], col0=0, n_cols=d, tm=tm, tn=tn, out_dtypes=(F32,),
        epilogue=lambda acc, ga: (jax.nn.sigmoid(ga.astype(F32)) * acc,),
        extras=[(proj_b, (tm, tn), lambda j, i: (i, ga0 + j))])
    (merged,) = _matmul(
        u, p["w_out_sc"], col0=0, n_cols=d, tm=tm, tn=tn, out_dtypes=(BF16,),
        epilogue=lambda acc, gb, pa: (pa + jax.nn.sigmoid(gb.astype(F32)) * acc,),
        extras=[(proj_b, (tm, tn), lambda j, i: (i, gb0 + j)),
                (part_a, (tm, tn), lambda j, i: (i, j))])
    tm3 = _pick(t, (256, 128))
    def mix_epilogue(acc, hh, g, b):
        hn = _layer_norm(alpha * hh + acc, g, b)
        return hn, _pack_halves_bf16(hn)

    h1, h1p = _matmul(
        merged, p["w_out"].astype(BF16), col0=0, n_cols=d, tm=tm3, tn=d, out_dtypes=(F32, U32), out_div=(1, 2),
        epilogue=mix_epilogue,
        extras=[(h0, (tm3, d), lambda j, i: (i, 0)),
                (p["ln_mix_g"].reshape(1, d), (1, d), lambda j, i: (0, 0)),
                (p["ln_mix_b"].reshape(1, d), (1, d), lambda j, i: (0, 0))])

    n_exp = p["w_router"].shape[1]
    idx, gate, rank, cnt = _router(h1, p["w_router"], p["b_router"])
    mtm = MOE_TM
    xmax = MOE_XMAX
    n_rows = t * TOP_K + n_exp * mtm
    n_items = n_exp + n_rows // xmax
    pos, item_e, item_start, item_nblk = _moe_plan(
        idx[:, :TOP_K], rank[:, :TOP_K], cnt[0, :n_exp], tm=mtm, xmax=xmax, n_items=n_items)
    rows = _dispatch(h1p, pos, n_rows)
    y = _experts(rows, item_e, item_start, item_nblk, p["w_gate_up"], p["b_gate_up"], p["w_down"], p["b_down"],
                 tm=mtm, xmax=xmax)
    h2 = _combine(y, pos, gate, h1, p["ln_ffn_g"], p["ln_ffn_b"], alpha)
    return h2


_LAYER_PARAMS = ("w_in", "gdn_conv_w", "gdn_a_log", "gdn_dt_bias", "gdn_norm_w", "w_out_gdn", "sc_conv_w",
                 "w_out_sc", "w_out", "ln_mix_g", "ln_mix_b", "w_router", "b_router", "w_gate_up", "b_gate_up",
                 "w_down", "b_down", "ln_ffn_g", "ln_ffn_b")


def kernel(x, ln_in_g, ln_in_b, w_in, gdn_conv_w, gdn_a_log, gdn_dt_bias, gdn_norm_w, w_out_gdn, sc_conv_w,
           w_out_sc, w_out, ln_mix_g, ln_mix_b, w_router, b_router, w_gate_up, b_gate_up, w_down, b_down,
           ln_ffn_g, ln_ffn_b):
    stacked = dict(zip(_LAYER_PARAMS, (w_in, gdn_conv_w, gdn_a_log, gdn_dt_bias, gdn_norm_w, w_out_gdn,
                                       sc_conv_w, w_out_sc, w_out, ln_mix_g, ln_mix_b, w_router, b_router,
                                       w_gate_up, b_gate_up, w_down, b_down, ln_ffn_g, ln_ffn_b)))
    bsz, seq, d = x.shape
    depth = w_in.shape[0]
    alpha = (2 * depth) ** 0.25
    h, hb = _ln_in(x.reshape(bsz * seq, d), ln_in_g, ln_in_b)
    for l in range(depth):
        p = {name: arr[l] for name, arr in stacked.items()}
        h = _layer(h, hb, p, bsz=bsz, seq=seq, alpha=alpha)
        if l + 1 < depth:
            hb = h.astype(BF16)
    return h.reshape(bsz, seq, d)
```

```python
import functools

import jax
import jax.numpy as jnp
from jax import lax
from jax.experimental import pallas as pl
from jax.experimental.pallas import tpu as pltpu

F32 = jnp.float32
BF16 = jnp.bfloat16
I32 = jnp.int32
U32 = jnp.uint32

LANES = 128
SUBLANES = 8
MXU_N = 256
VMEM_LIMIT = 60 << 20

TOP_K = 4
SWIGLU_LIMIT = 7.0
SWIGLU_ALPHA = 1.702
LN_EPS = 1e-5
RMS_EPS = 1e-6
GDN_CHUNK = 64
GDN_STEP = 512
GDN_QK_GROUP = 4
GDN_INV_GROUP = 16
MOE_TM = 256
MOE_XMAX = 1280
MOE_TF = 512


def _params(sem):
    return pltpu.CompilerParams(dimension_semantics=sem, vmem_limit_bytes=VMEM_LIMIT)


def _pick(n, candidates):
    for c in candidates:
        if n % c == 0:
            return c
    raise ValueError(f"no tile for {n} in {candidates}")


def _sigmoid(x):
    return 0.5 * jnp.tanh(0.5 * x) + 0.5


def _layer_norm(xf, g, b):
    mu = jnp.mean(xf, axis=-1, keepdims=True)
    xc = xf - mu
    var = jnp.mean(xc * xc, axis=-1, keepdims=True)
    return xc * lax.rsqrt(var + LN_EPS) * g + b


def _pack_halves_bf16(x):
    half = x.shape[1] // 2
    lo = lax.bitcast_convert_type(x[:, :half].astype(BF16).astype(F32), U32) >> 16
    hi = lax.bitcast_convert_type(x[:, half:].astype(BF16).astype(F32), U32) & jnp.uint32(0xFFFF0000)
    return lo | hi


def _unpack_halves_bf16(p):
    lo = lax.bitcast_convert_type(p << 16, F32).astype(BF16)
    hi = lax.bitcast_convert_type(p & jnp.uint32(0xFFFF0000), F32).astype(BF16)
    return lo, hi


def _ln_in_kernel(x_ref, g_ref, b_ref, h_ref, hb_ref):
    h = _layer_norm(x_ref[...], g_ref[...], b_ref[...])
    h_ref[...] = h
    hb_ref[...] = h.astype(BF16)


def _ln_in(x2, g, b):
    t, d = x2.shape
    tm = _pick(t, (512, 256, 128, 64, 32, 16))
    return pl.pallas_call(
        _ln_in_kernel,
        grid=(t // tm,),
        in_specs=[pl.BlockSpec((tm, d), lambda i: (i, 0)),
                  pl.BlockSpec((1, d), lambda i: (0, 0)),
                  pl.BlockSpec((1, d), lambda i: (0, 0))],
        out_specs=[pl.BlockSpec((tm, d), lambda i: (i, 0)),
                   pl.BlockSpec((tm, d), lambda i: (i, 0))],
        out_shape=[jax.ShapeDtypeStruct((t, d), F32), jax.ShapeDtypeStruct((t, d), BF16)],
        compiler_params=_params(("arbitrary",)),
        name="ln_in",
    )(x2, g.reshape(1, d), b.reshape(1, d))


def _mm_kernel(*refs, shift, w_t, row_split, n_extra, n_out, epilogue):
    n_w = 2 if shift else 1
    x_ref, w_refs = refs[0], refs[1:1 + n_w]
    extras = refs[1 + n_w:1 + n_w + n_extra]
    o_refs = refs[1 + n_w + n_extra:1 + n_w + n_extra + n_out]
    wb_ref = refs[-1]
    n_axis = 0 if w_t else 1

    @pl.when(pl.program_id(1) == 0)
    def _():
        if shift:
            tn = w_refs[0].shape[n_axis]
            w = jnp.concatenate([lax.slice_in_dim(w_refs[0][...], shift, tn, axis=n_axis),
                                 lax.slice_in_dim(w_refs[1][...], 0, shift, axis=n_axis)], axis=n_axis)
        else:
            w = w_refs[0][...]
        wb_ref[...] = w.astype(BF16)

    tm = x_ref.shape[0]
    sub = tm // row_split
    for r in range(row_split):
        rows = slice(r * sub, (r + 1) * sub)
        acc = lax.dot_general(x_ref[rows, :], wb_ref[...], (((1,), (1 if w_t else 0,)), ((), ())),
                              preferred_element_type=F32)
        outs = epilogue(acc, *[e[rows, :] if e.shape[0] == tm else e[...] for e in extras])
        for o_ref, o in zip(o_refs, outs):
            o_ref[rows, :] = o.astype(o_ref.dtype)


def _matmul(x, w, *, col0, n_cols, tm, tn, out_dtypes, epilogue=None, extras=(), out_div=None, w_t=False,
            row_split=1):
    t, k = x.shape
    k_axis, n_axis = (1, 0) if w_t else (0, 1)
    assert w.shape[k_axis] == k and n_cols % tn == 0 and t % tm == 0 and col0 + n_cols <= w.shape[n_axis]
    j0, shift = divmod(col0, tn)
    if epilogue is None:
        epilogue = lambda acc: (acc,)
    if out_div is None:
        out_div = (1,) * len(out_dtypes)
    assert tm % (row_split * 16) == 0
    kern = functools.partial(_mm_kernel, shift=shift, w_t=w_t, row_split=row_split, n_extra=len(extras),
                             n_out=len(out_dtypes), epilogue=epilogue)
    w_block = (tn, k) if w_t else (k, tn)
    w_index = lambda jj: (jj, 0) if w_t else (0, jj)
    in_specs = [pl.BlockSpec((tm, k), lambda j, i: (i, 0)),
                pl.BlockSpec(w_block, lambda j, i: w_index(j + j0))]
    ws = [w]
    if shift:
        in_specs.append(pl.BlockSpec(w_block, lambda j, i: w_index(j + j0 + 1)))
        ws.append(w)
    in_specs += [pl.BlockSpec(bs, im) for (_, bs, im) in extras]
    return pl.pallas_call(
        kern,
        grid=(n_cols // tn, t // tm),
        in_specs=in_specs,
        out_specs=[pl.BlockSpec((tm, tn // dv), lambda j, i: (i, j)) for dv in out_div],
        out_shape=[jax.ShapeDtypeStruct((t, n_cols // dv), dt) for dt, dv in zip(out_dtypes, out_div)],
        scratch_shapes=[pltpu.VMEM(w_block, BF16)],
        compiler_params=_params(("arbitrary", "arbitrary")),
        name="matmul",
    )(x, *ws, *[a for (a, _, _) in extras])


def _gates_kernel(ba_ref, alog_ref, dt_ref, col_ref, row_ref, *, hv):
    x = ba_ref[...]
    lane = lax.broadcasted_iota(I32, x.shape, 1)
    beta = _sigmoid(x)
    xs = x + dt_ref[...]
    softplus = jnp.maximum(xs, 0.0) + jnp.log(1.0 + jnp.exp(-jnp.abs(xs)))
    g = -jnp.exp(alog_ref[...]) * softplus
    g = jnp.where((lane >= hv) & (lane < 2 * hv), g, 0.0)
    n = x.shape[0]
    row = lax.broadcasted_iota(I32, x.shape, 0) & (GDN_CHUNK - 1)
    cum = g
    rev = g
    sh = 1
    while sh < GDN_CHUNK:
        cum = cum + jnp.where(row >= sh, pltpu.roll(cum, sh, 0), 0.0)
        rev = rev + jnp.where(row < GDN_CHUNK - sh, pltpu.roll(rev, n - sh, 0), 0.0)
        sh *= 2
    e_cum = pltpu.roll(jnp.exp(cum), hv, 1)
    e_rest = pltpu.roll(jnp.exp(rev - g), 2 * hv, 1)
    out = jnp.where(lane < hv, beta, jnp.where(lane < 2 * hv, cum, jnp.where(lane < 3 * hv, e_cum, e_rest)))
    col_ref[...] = out
    row_ref[...] = out.T


def _gdn_gates(ba, a_log, dt_bias, hv):
    assert 4 * hv <= LANES
    t = ba.shape[0]
    tm = _pick(t, (512, 256, 128))
    pad = lambda v: jnp.zeros((1, LANES), F32).at[0, hv:2 * hv].set(v.astype(F32))
    return pl.pallas_call(
        functools.partial(_gates_kernel, hv=hv),
        grid=(t // tm,),
        in_specs=[pl.BlockSpec((tm, LANES), lambda i: (i, 0)),
                  pl.BlockSpec((1, LANES), lambda i: (0, 0)),
                  pl.BlockSpec((1, LANES), lambda i: (0, 0))],
        out_specs=[pl.BlockSpec((tm, LANES), lambda i: (i, 0)),
                   pl.BlockSpec((LANES, tm), lambda i: (0, i))],
        out_shape=[jax.ShapeDtypeStruct((t, LANES), F32), jax.ShapeDtypeStruct((LANES, t), F32)],
        compiler_params=_params(("arbitrary",)),
        name="gdn_gates",
    )(ba, pad(a_log), pad(dt_bias))


def _causal_conv(x, w, ext_ref):
    n = x.shape[0]
    kw = w.shape[0]
    ext_ref[SUBLANES:, :] = x
    acc = x * w[kw - 1:kw]
    for d in range(1, kw):
        acc = acc + ext_ref[SUBLANES - d:SUBLANES - d + n, :] * w[kw - 1 - d:kw - d]
    ext_ref[:SUBLANES, :] = x[n - SUBLANES:]
    return acc


def _silu(x):
    h = 0.5 * x
    return h + h * jnp.tanh(h)


def _unit_lower_inverses(lows, ii, jj):
    c = lows[0].shape[0]
    eye = jnp.where(ii == jj, 1.0, 0.0)
    pair = (ii >> 1) == (jj >> 1)
    xbs = [(eye - jnp.where(pair, low, 0.0)).astype(BF16) for low in lows]
    lbs = [low.astype(BF16) for low in lows]
    zero = jnp.zeros((c, c), BF16)
    s = 1
    while (2 << s) <= c:
        m = ((ii >> (s + 1)) == (jj >> (s + 1))) & ((ii >> s) != (jj >> s))
        ts = [jnp.dot(jnp.where(m, lb, zero), xb, preferred_element_type=F32) for lb, xb in zip(lbs, xbs)]
        xbs = [jnp.dot(xb, (eye - t).astype(BF16), preferred_element_type=F32).astype(BF16)
               for xb, t in zip(xbs, ts)]
        s += 1
    return xbs


def _gdn_kernel(q_ref, k_ref, v_ref, z_ref, cwq_ref, cwk_ref, cwv_ref, gcol_ref, grow_ref, nw_ref,
                o_ref, s_ref, tq_ref, tk_ref, tv_ref, qn_ref, kn_ref, vn_ref, u_ref, wq_ref, attn_ref, kd_ref,
                *, nq, vper, dk):
    @pl.when(pl.program_id(2) == 0)
    def _():
        s_ref[...] = jnp.zeros_like(s_ref)
        for t_ref in (tq_ref, tk_ref, tv_ref):
            t_ref[:SUBLANES, :] = jnp.zeros((SUBLANES, t_ref.shape[1]), F32)

    tb = q_ref.shape[0]
    c = GDN_CHUNK
    nc = tb // c
    nv = nq * vper
    head = lambda h: slice(h * dk, (h + 1) * dk)

    q = _silu(_causal_conv(q_ref[...].astype(F32), cwq_ref[...], tq_ref))
    k = _silu(_causal_conv(k_ref[...].astype(F32), cwk_ref[...], tk_ref))
    for hq in range(nq):
        qh = q[:, head(hq)]
        kh = k[:, head(hq)]
        qn_ref[:, head(hq)] = qh * (lax.rsqrt(jnp.sum(qh * qh, axis=-1, keepdims=True) + RMS_EPS) * (dk ** -0.5))
        kn_ref[:, head(hq)] = kh * lax.rsqrt(jnp.sum(kh * kh, axis=-1, keepdims=True) + RMS_EPS)

    cols = gcol_ref[...]
    grows = grow_ref[...]
    nw = nw_ref[...]
    ii = lax.broadcasted_iota(I32, (c, c), 0)
    jj = lax.broadcasted_iota(I32, (c, c), 1)
    incl = ii >= jj
    strict = ii > jj

    def col(ci, which, h):
        return cols[ci * c:(ci + 1) * c, which * nv + h:which * nv + h + 1]

    def state_step(ci):
        rows = slice(ci * c, (ci + 1) * c)
        ps = [ci * nv + h for h in range(nv)]
        r1 = [jnp.dot(wq_ref[p], s_ref[h].astype(BF16), preferred_element_type=F32) for h, p in enumerate(ps)]
        v_new = [(u_ref[p] - r[:c]).astype(BF16) for p, r in zip(ps, r1)]
        outs = [r[c:] + jnp.dot(attn_ref[p], vn, preferred_element_type=F32) for p, r, vn in zip(ps, r1, v_new)]
        for h, (p, vn) in enumerate(zip(ps, v_new)):
            g_tot = cols[(ci + 1) * c - 1:(ci + 1) * c, 2 * nv + h:2 * nv + h + 1]
            s_ref[h] = s_ref[h] * g_tot + lax.dot_general(
                kd_ref[p], vn, (((0,), (0,)), ((), ())), preferred_element_type=F32)
        for h, o in enumerate(outs):
            zc = z_ref[rows, head(h)].astype(F32)
            var = jnp.mean(o * o, axis=-1, keepdims=True)
            o_ref[rows, head(h)] = (o * lax.rsqrt(var + RMS_EPS) * nw * _silu(zc)).astype(BF16)

    problems = [(ci, h) for ci in range(nc) for h in range(nv)]
    gram = {}
    chunks_done = 0
    for g0 in range(0, len(problems), GDN_INV_GROUP):
        group = problems[g0:g0 + GDN_INV_GROUP]
        lows = []
        for ci, h in group:
            hq = h // vper
            rows = slice(ci * c, (ci + 1) * c)
            if (ci, hq) not in gram:
                kcb = kn_ref[rows, head(hq)].astype(BF16)
                qcb = qn_ref[rows, head(hq)].astype(BF16)
                gram[(ci, hq)] = lax.dot_general(jnp.concatenate([kcb, qcb], axis=0), kcb,
                                                 (((1,), (1,)), ((), ())), preferred_element_type=F32)
            a = gram[(ci, hq)]
            grow = grows[h:h + 1, rows]
            decay = jnp.where(incl, jnp.exp(jnp.minimum(col(ci, 1, h) - grow, 0.0)), 0.0)
            lows.append(jnp.where(strict, col(ci, 0, h) * a[:c] * decay, 0.0))
            attn_ref[ci * nv + h] = jnp.where(incl, a[c:] * decay, 0.0).astype(BF16)
        if g0 == 0:
            vn_ref[...] = _silu(_causal_conv(v_ref[...].astype(F32), cwv_ref[...], tv_ref))
        xs = _unit_lower_inverses(lows, ii, jj)
        for x, (ci, h) in zip(xs, group):
            hq = h // vper
            rows = slice(ci * c, (ci + 1) * c)
            p = ci * nv + h
            kc = kn_ref[rows, head(hq)]
            bcol = col(ci, 0, h)
            eg = col(ci, 2, h)
            rhs = jnp.concatenate([vn_ref[rows, head(h)] * bcol, kc * (bcol * eg)], axis=1).astype(BF16)
            uw = jnp.dot(x, rhs, preferred_element_type=F32)
            u_ref[p] = uw[:, :dk]
            wq_ref[p, :c, :] = uw[:, dk:].astype(BF16)
            wq_ref[p, c:, :] = (qn_ref[rows, head(hq)] * eg).astype(BF16)
            kd_ref[p] = (kc * col(ci, 3, h)).astype(BF16)
        chunks_ready = (g0 + len(group)) // nv
        for ci in range(chunks_done, chunks_ready):
            state_step(ci)
        chunks_done = chunks_ready


def _gdn(proj_a, conv_w, gcol, grow, norm_w, *, bsz, seq, hq, hv, dk):
    t = bsz * seq
    vper = hv // hq
    nq = _pick(hq, (GDN_QK_GROUP, 2, 1))
    nv = nq * vper
    ng = hq // nq
    tb = _pick(seq, (GDN_STEP, 128, 64))
    ns = seq // tb
    nprob = (tb // GDN_CHUNK) * nv
    qw = nq * dk
    vw = nv * dk
    v_blk0 = 2 * hq * dk // vw
    z_blk0 = (2 * hq * dk + hv * dk) // vw
    kw = conv_w.shape[0]
    gcol = gcol[:, :4 * hv].reshape(t, 4, ng, nv).transpose(2, 0, 1, 3).reshape(ng, t, 4 * nv)
    grow = grow[hv:2 * hv].reshape(ng, nv, t)
    rowblk = lambda b, g, s: b * ns + s
    kern = functools.partial(_gdn_kernel, nq=nq, vper=vper, dk=dk)
    return pl.pallas_call(
        kern,
        grid=(bsz, ng, ns),
        in_specs=[
            pl.BlockSpec((tb, qw), lambda b, g, s: (rowblk(b, g, s), g)),
            pl.BlockSpec((tb, qw), lambda b, g, s: (rowblk(b, g, s), ng + g)),
            pl.BlockSpec((tb, vw), lambda b, g, s: (rowblk(b, g, s), v_blk0 + g)),
            pl.BlockSpec((tb, vw), lambda b, g, s: (rowblk(b, g, s), z_blk0 + g)),
            pl.BlockSpec((kw, qw), lambda b, g, s: (0, g)),
            pl.BlockSpec((kw, qw), lambda b, g, s: (0, ng + g)),
            pl.BlockSpec((kw, vw), lambda b, g, s: (0, v_blk0 + g)),
            pl.BlockSpec((None, tb, 4 * nv), lambda b, g, s: (g, rowblk(b, g, s), 0)),
            pl.BlockSpec((None, nv, tb), lambda b, g, s: (g, 0, rowblk(b, g, s))),
            pl.BlockSpec((1, dk), lambda b, g, s: (0, 0)),
        ],
        out_specs=pl.BlockSpec((tb, vw), lambda b, g, s: (rowblk(b, g, s), g)),
        out_shape=jax.ShapeDtypeStruct((t, hv * dk), BF16),
        scratch_shapes=[pltpu.VMEM((nv, dk, dk), F32),
                        pltpu.VMEM((tb + SUBLANES, qw), F32),
                        pltpu.VMEM((tb + SUBLANES, qw), F32),
                        pltpu.VMEM((tb + SUBLANES, vw), F32),
                        pltpu.VMEM((tb, qw), F32),
                        pltpu.VMEM((tb, qw), F32),
                        pltpu.VMEM((tb, vw), F32),
                        pltpu.VMEM((nprob, GDN_CHUNK, dk), F32),
                        pltpu.VMEM((nprob, 2 * GDN_CHUNK, dk), BF16),
                        pltpu.VMEM((nprob, GDN_CHUNK, GDN_CHUNK), BF16),
                        pltpu.VMEM((nprob, GDN_CHUNK, dk), BF16)],
        compiler_params=_params(("arbitrary", "arbitrary", "arbitrary")),
        name="gdn",
    )(proj_a, proj_a, proj_a, proj_a, conv_w, conv_w, conv_w, gcol, grow, norm_w.reshape(1, dk))


def _sc_kernel(b_ref, c_ref, x_ref, w_ref, u_ref, tail_ref):
    @pl.when(pl.program_id(1) == 0)
    def _():
        tail_ref[:SUBLANES, :] = jnp.zeros((SUBLANES, tail_ref.shape[1]), F32)

    p = c_ref[...].astype(F32) * x_ref[...].astype(F32)
    u_ref[...] = (b_ref[...].astype(F32) * _causal_conv(p, w_ref[...], tail_ref)).astype(BF16)


def _short_conv(proj_b, conv_w, *, bsz, seq, width):
    t = bsz * seq
    ts = _pick(seq, (512, 256, 128, 64))
    ns = seq // ts
    kw = conv_w.shape[0]
    return pl.pallas_call(
        _sc_kernel,
        grid=(bsz, ns),
        in_specs=[pl.BlockSpec((ts, width), lambda b, s: (b * ns + s, 0)),
                  pl.BlockSpec((ts, width), lambda b, s: (b * ns + s, 1)),
                  pl.BlockSpec((ts, width), lambda b, s: (b * ns + s, 2)),
                  pl.BlockSpec((kw, width), lambda b, s: (0, 0))],
        out_specs=pl.BlockSpec((ts, width), lambda b, s: (b * ns + s, 0)),
        out_shape=jax.ShapeDtypeStruct((t, width), BF16),
        scratch_shapes=[pltpu.VMEM((ts + SUBLANES, width), F32)],
        compiler_params=_params(("arbitrary", "arbitrary")),
        name="short_conv",
    )(proj_b, proj_b, proj_b, conv_w)


def _router_kernel(h_ref, w_ref, b_ref, idx_ref, gate_ref, rank_ref, cnt_ref, run_ref):
    @pl.when(pl.program_id(0) == 0)
    def _():
        run_ref[...] = jnp.zeros_like(run_ref)

    tm = h_ref.shape[0]
    h = h_ref[...]
    h_hi = h.astype(BF16)
    h_lo = (h - h_hi.astype(F32)).astype(BF16)
    w2 = w_ref[...]
    first = jnp.dot(h_hi, w2, preferred_element_type=F32)
    logits = (first[:, :LANES] + first[:, LANES:]
              + jnp.dot(h_lo, w2[:, :LANES], preferred_element_type=F32) + b_ref[...])
    lane = lax.broadcasted_iota(I32, logits.shape, 1)
    lane_f = lane.astype(F32)
    cur = logits
    idxs, vals = [], []
    for _ in range(TOP_K):
        m = jnp.max(cur, axis=-1, keepdims=True)
        ix = jnp.min(jnp.where(cur == m, lane_f, float(LANES)), axis=-1, keepdims=True).astype(I32)
        idxs.append(ix)
        vals.append(m)
        cur = jnp.where(lane == ix, -jnp.inf, cur)
    es = [jnp.exp(val - vals[0]) for val in vals]
    den = es[0]
    for e in es[1:]:
        den = den + e
    onehot = jnp.zeros(logits.shape, F32)
    for ix in idxs:
        onehot = onehot + jnp.where(lane == ix, 1.0, 0.0)
    ri = lax.broadcasted_iota(I32, (tm, tm), 0)
    ci = lax.broadcasted_iota(I32, (tm, tm), 1)
    tri = jnp.where(ri > ci, 1.0, 0.0).astype(BF16)
    before = jnp.dot(tri, onehot.astype(BF16), preferred_element_type=F32) + run_ref[...]
    idx_out = jnp.zeros(logits.shape, I32)
    rank_out = jnp.zeros(logits.shape, I32)
    gate_out = jnp.zeros(logits.shape, F32)
    for kk in range(TOP_K):
        rank = jnp.sum(jnp.where(lane == idxs[kk], before, 0.0), axis=-1, keepdims=True).astype(I32)
        idx_out = jnp.where(lane == kk, idxs[kk], idx_out)
        rank_out = jnp.where(lane == kk, rank, rank_out)
        gate_out = jnp.where(lane == kk, es[kk] / den, gate_out)
    idx_ref[...] = idx_out
    rank_ref[...] = rank_out
    gate_ref[...] = gate_out
    run_ref[...] = run_ref[...] + jnp.sum(onehot, axis=0, keepdims=True)
    cnt_ref[...] = run_ref[...].astype(I32)


def _router(h, w_router, b_router):
    t, d = h.shape
    e = w_router.shape[1]
    tm = _pick(t, (256, 128, 64, 32, 16, 8))
    wp = jnp.zeros((d, LANES), F32).at[:, :e].set(w_router)
    w_hi = wp.astype(BF16)
    w_lo = (wp - w_hi.astype(F32)).astype(BF16)
    w2 = jnp.concatenate([w_hi, w_lo], axis=1)
    bp = jnp.full((1, LANES), -1e30, F32).at[0, :e].set(b_router)
    tile = pl.BlockSpec((tm, LANES), lambda i: (i, 0))
    return pl.pallas_call(
        _router_kernel,
        grid=(t // tm,),
        in_specs=[pl.BlockSpec((tm, d), lambda i: (i, 0)),
                  pl.BlockSpec((d, 2 * LANES), lambda i: (0, 0)),
                  pl.BlockSpec((1, LANES), lambda i: (0, 0))],
        out_specs=[tile, tile, tile, pl.BlockSpec((1, LANES), lambda i: (0, 0))],
        out_shape=[jax.ShapeDtypeStruct((t, LANES), I32), jax.ShapeDtypeStruct((t, LANES), F32),
                   jax.ShapeDtypeStruct((t, LANES), I32), jax.ShapeDtypeStruct((1, LANES), I32)],
        scratch_shapes=[pltpu.VMEM((1, LANES), F32)],
        compiler_params=_params(("arbitrary",)),
        name="router",
    )(h, w2, bp)


def _row_copy(src, s_row, dst, d_row, sem):
    return pltpu.make_async_copy(src.at[pl.ds(s_row, 1)], dst.at[pl.ds(d_row, 1)], sem)


def _dispatch_kernel(pos_ref, x_ref, rows_in_ref, rows_ref, sem):
    del rows_in_ref
    tm = x_ref.shape[0]
    base = pl.program_id(0) * (tm * TOP_K)

    for t in range(tm):
        for kk in range(TOP_K):
            _row_copy(x_ref, t, rows_ref, pos_ref[base + (t * TOP_K + kk)], sem).start(priority=kk & 1)
    n = tm * TOP_K
    pltpu.make_async_copy(rows_ref.at[pl.ds(0, n)], rows_ref.at[pl.ds(0, n)], sem).wait()


def _dispatch(h, pos, n_rows):
    t, d = h.shape
    tm = _pick(t, (128, 64, 32, 16, 8))
    rows0 = jnp.zeros((n_rows, d), h.dtype)
    return pl.pallas_call(
        _dispatch_kernel,
        grid_spec=pltpu.PrefetchScalarGridSpec(
            num_scalar_prefetch=1,
            grid=(t // tm,),
            in_specs=[pl.BlockSpec((tm, d), lambda i, pos: (i, 0)),
                      pl.BlockSpec(memory_space=pl.ANY)],
            out_specs=pl.BlockSpec(memory_space=pl.ANY),
            scratch_shapes=[pltpu.SemaphoreType.DMA(())],
        ),
        out_shape=jax.ShapeDtypeStruct((n_rows, d), h.dtype),
        input_output_aliases={2: 0},
        compiler_params=_params(("arbitrary",)),
        name="dispatch",
    )(pos, h, rows0)


def _swiglu_interleaved(gu):
    lane = lax.broadcasted_iota(I32, (gu.shape[0], LANES), 1)
    even = (lane & 1) == 0
    parts = []
    for c in range(gu.shape[1] // LANES):
        g = gu[:, c * LANES:(c + 1) * LANES]
        gt = jnp.minimum(g, SWIGLU_LIMIT)
        glu = gt * _sigmoid(SWIGLU_ALPHA * gt)
        up = jnp.clip(g, -SWIGLU_LIMIT, SWIGLU_LIMIT) + 1.0
        parts.append(jnp.where(even, pltpu.roll(up, LANES - 1, 1) * glu, 0.0))
    outs = [parts[2 * m] + pltpu.roll(parts[2 * m + 1], 1, 1) for m in range(len(parts) // 2)]
    return jnp.concatenate(outs, axis=1) if len(outs) > 1 else outs[0]


def _interleave_rows_bf16(w_ref, out_ref):
    half = LANES // 2
    for m in range(w_ref.shape[0] // LANES):
        lo = w_ref[m * LANES:m * LANES + half, :].astype(BF16).astype(F32)
        hi = w_ref[m * LANES + half:(m + 1) * LANES, :].astype(BF16).astype(F32)
        packed = (lax.bitcast_convert_type(lo, jnp.uint32) >> 16) | (
            lax.bitcast_convert_type(hi, jnp.uint32) & jnp.uint32(0xFFFF0000))
        out_ref[m * LANES:(m + 1) * LANES, :] = pltpu.bitcast(packed, BF16)


def _expert_kernel(ie_ref, ist_ref, inb_ref, rows_ref, wgu_ref, bgu_ref, wd_ref, bd_ref, y_ref,
                   xb_ref, acc_ref, stage_ref, wgub_ref, wdb_ref, ost_ref, sem_in, sem_out, *, tm, n_j):
    del ie_ref
    wi = pl.program_id(0)
    j = pl.program_id(1)
    n_items = pl.num_programs(0)
    nblk = inb_ref[wi]
    half = stage_ref.shape[1]

    def block(b):
        return pl.ds(pl.multiple_of(b * tm, tm), tm)

    def hbm_block(item, b):
        return pl.ds(pl.multiple_of(ist_ref[item] + b * tm, tm), tm)

    def rows_copy(item, b):
        return pltpu.make_async_copy(rows_ref.at[hbm_block(item, b)], stage_ref.at[block(b)], sem_in)

    def y_copy(item, b):
        slot = b & 1
        return pltpu.make_async_copy(ost_ref.at[slot], y_ref.at[hbm_block(item, b)], sem_out.at[slot])

    def drain_stores(item):
        n = inb_ref[item]
        for back in (2, 1):
            @pl.when(n >= back)
            def _():
                y_copy(item, n - back).wait()

    def for_blocks(item, fn):
        def body(b, carry):
            fn(item, b)
            return carry
        lax.fori_loop(0, inb_ref[item], body, 0)

    def unpack(item, b):
        lo, hi = _unpack_halves_bf16(stage_ref[block(b), :])
        xb_ref[block(b), :half] = lo
        xb_ref[block(b), half:] = hi

    @pl.when(j == 0)
    def _():
        @pl.when(wi == 0)
        def _():
            for_blocks(0, lambda it, b: rows_copy(it, b).start())
            acc_ref[...] = jnp.zeros_like(acc_ref)

        @pl.when(wi > 0)
        def _():
            drain_stores(wi - 1)

        for_blocks(wi, lambda it, b: rows_copy(it, b).wait())
        for_blocks(wi, unpack)

        @pl.when(wi + 1 < n_items)
        def _():
            for_blocks(wi + 1, lambda it, b: rows_copy(it, b).start())

    @pl.when(nblk > 0)
    def _():
        bias = bgu_ref[...]
        n_ct, _, ct = wgub_ref.shape

        def convert_weights():
            for c in range(n_ct):
                wgub_ref[c] = wgu_ref[:, c * ct:(c + 1) * ct].astype(BF16)
            _interleave_rows_bf16(wd_ref, wdb_ref)

        def mlp(b):
            x = xb_ref[block(b), :]
            gu = jnp.concatenate([jnp.dot(x, wgub_ref[c], preferred_element_type=F32) for c in range(n_ct)],
                                 axis=1) + bias
            act = _swiglu_interleaved(gu).astype(BF16)
            return jnp.dot(act, wdb_ref[...], preferred_element_type=F32)

        def accumulate(b):
            prev = jnp.where(j == 0, jnp.broadcast_to(bd_ref[...], (tm, acc_ref.shape[1])), acc_ref[block(b), :])
            acc_ref[block(b), :] = prev + mlp(b)

        def finish(b):
            prev = acc_ref[block(b), :] if n_j > 1 else bd_ref[...]
            ost_ref[b & 1] = _pack_halves_bf16(prev + mlp(b))
            y_copy(wi, b).start()

        def run(fn, stores):
            @pl.when(nblk >= 2)
            def _():
                convert_weights()
                fn(0)
                fn(1)

            @pl.when(nblk == 1)
            def _():
                convert_weights()
                fn(0)

            def pair(i, carry):
                if stores:
                    y_copy(wi, 2 * i - 2).wait()
                    y_copy(wi, 2 * i - 1).wait()
                fn(2 * i)
                fn(2 * i + 1)
                return carry
            lax.fori_loop(1, nblk // 2, pair, 0)

            @pl.when(((nblk & 1) == 1) & (nblk >= 3))
            def _():
                if stores:
                    y_copy(wi, nblk - 3).wait()
                fn(nblk - 1)

        if n_j > 1:
            @pl.when(j < n_j - 1)
            def _():
                run(accumulate, False)

        @pl.when(j == n_j - 1)
        def _():
            run(finish, True)

    @pl.when((wi == n_items - 1) & (j == n_j - 1))
    def _():
        drain_stores(wi)
        ost_ref[0] = jnp.zeros(ost_ref.shape[1:], U32)

        def fill(b, carry):
            cp = pltpu.make_async_copy(ost_ref.at[0], y_ref.at[pl.ds(pl.multiple_of(b * tm, tm), tm)],
                                       sem_out.at[0])
            cp.start()
            cp.wait()
            return carry
        lax.fori_loop(ist_ref[n_items] // tm, y_ref.shape[0] // tm, fill, 0)


def _experts(rows, item_e, item_start, item_nblk, w_gate_up, b_gate_up, w_down, b_down, *, tm, xmax):
    n_rows = rows.shape[0]
    n_exp, d, f2 = w_gate_up.shape
    assert rows.shape[1] * 2 == d
    f = f2 // 2
    tf = _pick(f, (MOE_TF, LANES))
    n_j = f // tf
    ct = min(2 * tf, MXU_N)
    n_items = item_e.shape[0]

    def jeff(j, inb, wi):
        return jnp.where(inb[wi] > 0, j, n_j - 1)

    kern = functools.partial(_expert_kernel, tm=tm, n_j=n_j)
    return pl.pallas_call(
        kern,
        grid_spec=pltpu.PrefetchScalarGridSpec(
            num_scalar_prefetch=3,
            grid=(n_items, n_j),
            in_specs=[
                pl.BlockSpec(memory_space=pl.ANY),
                pl.BlockSpec((None, d, 2 * tf), lambda wi, j, ie, ist, inb: (ie[wi], 0, jeff(j, inb, wi))),
                pl.BlockSpec((None, 1, 2 * tf), lambda wi, j, ie, ist, inb: (ie[wi], 0, jeff(j, inb, wi))),
                pl.BlockSpec((None, tf, d), lambda wi, j, ie, ist, inb: (ie[wi], jeff(j, inb, wi), 0)),
                pl.BlockSpec((None, 1, d), lambda wi, j, ie, ist, inb: (ie[wi], 0, 0)),
            ],
            out_specs=pl.BlockSpec(memory_space=pl.ANY),
            scratch_shapes=[pltpu.VMEM((xmax, d), BF16),
                            pltpu.VMEM((xmax, d), F32),
                            pltpu.VMEM((xmax, d // 2), U32),
                            pltpu.VMEM((2 * tf // ct, d, ct), BF16),
                            pltpu.VMEM((tf, d), BF16),
                            pltpu.VMEM((2, tm, d // 2), U32),
                            pltpu.SemaphoreType.DMA(()),
                            pltpu.SemaphoreType.DMA((2,))],
        ),
        out_shape=jax.ShapeDtypeStruct((n_rows, d // 2), U32),
        compiler_params=_params(("arbitrary", "arbitrary")),
        name="experts",
    )(item_e, item_start, item_nblk, rows, w_gate_up, b_gate_up.reshape(n_exp, 1, f2),
      w_down, b_down.reshape(n_exp, 1, d))


def _combine_kernel(pos_ref, y_ref, gate_ref, h_ref, g_ref, b_ref, o_ref, ybuf0_ref, ybuf1_ref, sem,
                    *, alpha):
    tm = h_ref.shape[0]
    i = pl.program_id(0)
    n_tiles = pl.num_programs(0)
    n = tm * TOP_K
    bufs = (ybuf0_ref, ybuf1_ref)

    def gather(tile, s):
        base = tile * n
        for t in range(tm):
            for kk in range(TOP_K):
                _row_copy(y_ref, pos_ref[base + (t * TOP_K + kk)], bufs[s], kk * tm + t,
                          sem.at[s]).start(priority=kk & 1)

    def wait_buf(s):
        pltpu.make_async_copy(y_ref.at[pl.ds(0, n)], bufs[s], sem.at[s]).wait()

    def step(s):
        wait_buf(s)
        gather(jnp.minimum(i + 1, n_tiles - 1), 1 - s)
        gate = gate_ref[...]
        half = y_ref.shape[1]
        ffn_lo = jnp.zeros((tm, half), F32)
        ffn_hi = jnp.zeros((tm, half), F32)
        for kk in range(TOP_K):
            p = bufs[s][kk * tm:(kk + 1) * tm, :]
            gk = gate[:, kk:kk + 1]
            ffn_lo = ffn_lo + gk * lax.bitcast_convert_type(p << 16, F32)
            ffn_hi = ffn_hi + gk * lax.bitcast_convert_type(p & jnp.uint32(0xFFFF0000), F32)
        ffn = jnp.concatenate([ffn_lo, ffn_hi], axis=1)
        o_ref[...] = _layer_norm(alpha * h_ref[...] + ffn, g_ref[...], b_ref[...])

        @pl.when(i == n_tiles - 1)
        def _():
            wait_buf(1 - s)

    @pl.when(i == 0)
    def _():
        gather(0, 0)

    for s in range(2):
        @pl.when((i & 1) == s)
        def _():
            step(s)


def _combine(y, pos, gate, h, ln_g, ln_b, alpha):
    t, d = h.shape
    tm = _pick(t, (128, 64, 32, 16, 8))
    assert y.shape[1] * 2 == d
    return pl.pallas_call(
        functools.partial(_combine_kernel, alpha=alpha),
        grid_spec=pltpu.PrefetchScalarGridSpec(
            num_scalar_prefetch=1,
            grid=(t // tm,),
            in_specs=[pl.BlockSpec(memory_space=pl.ANY),
                      pl.BlockSpec((tm, LANES), lambda i, pos: (i, 0)),
                      pl.BlockSpec((tm, d), lambda i, pos: (i, 0)),
                      pl.BlockSpec((1, d), lambda i, pos: (0, 0)),
                      pl.BlockSpec((1, d), lambda i, pos: (0, 0))],
            out_specs=pl.BlockSpec((tm, d), lambda i, pos: (i, 0)),
            scratch_shapes=[pltpu.VMEM((TOP_K * tm, d // 2), U32), pltpu.VMEM((TOP_K * tm, d // 2), U32),
                            pltpu.SemaphoreType.DMA((2,))],
        ),
        out_shape=jax.ShapeDtypeStruct((t, d), F32),
        compiler_params=_params(("arbitrary",)),
        name="combine",
    )(pos, y, gate, h, ln_g.reshape(1, d), ln_b.reshape(1, d))


def _moe_plan(idx, rank, counts, *, tm, xmax, n_items):
    n_exp = counts.shape[0]
    padded = (counts + tm - 1) // tm * tm
    pstart = jnp.cumsum(padded) - padded
    onehot = idx[..., None] == jnp.arange(n_exp, dtype=idx.dtype)
    pos = (jnp.sum(jnp.where(onehot, pstart, 0), axis=-1) + rank).reshape(-1).astype(I32)
    per_e = (padded + xmax - 1) // xmax
    cum = jnp.cumsum(per_e)
    total = cum[-1]
    wi = jnp.arange(n_items, dtype=I32)
    valid = wi < total
    e_w = jnp.minimum(jnp.searchsorted(cum, jnp.minimum(wi, total - 1), side="right"), n_exp - 1).astype(I32)
    local = jnp.minimum(wi, total - 1) - (cum - per_e)[e_w]
    start = (pstart[e_w] + local * xmax).astype(I32)
    nblk = jnp.where(valid, jnp.clip(padded[e_w] - local * xmax, 0, xmax) // tm, 0).astype(I32)
    start = jnp.concatenate([start, jnp.sum(padded, keepdims=True).astype(I32)])
    return pos, e_w, start, nblk


def _layer(h0, h0b, p, *, bsz, seq, alpha):
    t, d = h0.shape
    w_in = p["w_in"]
    hv = p["gdn_a_log"].shape[0]
    dk = p["gdn_norm_w"].shape[0]
    qkv_dim = p["gdn_conv_w"].shape[1]
    v_dim = hv * dk
    hq = (qkv_dim - v_dim) // (2 * dk)
    scw = p["sc_conv_w"].shape[1]
    assert 2 * hv <= LANES and hv % hq == 0
    n_a = qkv_dim + v_dim
    off_ba = n_a
    off_b = n_a + 2 * hv
    n_b = 3 * scw + 2 * d
    assert w_in.shape[1] == off_b + n_b

    w_in_t = w_in.T
    tm = _pick(t, (1024, 512, 256, 128))
    tn_a = _pick(n_a, (1024, 512, 256, 128))
    (proj_a,) = _matmul(h0b, w_in_t, col0=0, n_cols=n_a, tm=tm, tn=tn_a, out_dtypes=(BF16,), w_t=True)
    tn_b = _pick(n_b, (512, 256, 128))
    (proj_b,) = _matmul(h0b, w_in_t, col0=off_b, n_cols=n_b, tm=tm, tn=tn_b, out_dtypes=(BF16,), w_t=True)
    assert off_ba + LANES <= w_in.shape[1]
    (ba,) = _matmul(h0b, w_in_t, col0=off_ba, n_cols=LANES, tm=tm, tn=LANES, out_dtypes=(F32,), w_t=True)

    gcol, grow = _gdn_gates(ba, p["gdn_a_log"], p["gdn_dt_bias"], hv)
    o_n = _gdn(proj_a, p["gdn_conv_w"], gcol, grow, p["gdn_norm_w"], bsz=bsz, seq=seq, hq=hq, hv=hv, dk=dk)

    u = _short_conv(proj_b, p["sc_conv_w"], bsz=bsz, seq=seq, width=scw)

    tn = _pick(d, (512, 256, 128))
    ga0 = 3 * scw // tn
    gb0 = (3 * scw + d) // tn
    split = 2 if tm % 32 == 0 else 1
    (part_a,) = _matmul(
        o_n, p["w_out_gdn"], col0=0, n_cols=d, tm=tm, tn=tn, out_dtypes=(F32,), row_split=split,
        epilogue=lambda acc, ga: (_sigmoid(ga.astype(F32)) * acc,),
        extras=[(proj_b, (tm, tn), lambda j, i: (i, ga0 + j))])
    (merged,) = _matmul(
        u, p["w_out_sc"], col0=0, n_cols=d, tm=tm, tn=tn, out_dtypes=(BF16,), row_split=split,
        epilogue=lambda acc, gb, pa: (pa + _sigmoid(gb.astype(F32)) * acc,),
        extras=[(proj_b, (tm, tn), lambda j, i: (i, gb0 + j)),
                (part_a, (tm, tn), lambda j, i: (i, j))])
    tm3 = _pick(t, (512, 256, 128))

    def mix_epilogue(acc, hh, g, b):
        hn = _layer_norm(alpha * hh + acc, g, b)
        return hn, _pack_halves_bf16(hn)

    h1, h1p = _matmul(
        merged, p["w_out"].astype(BF16), col0=0, n_cols=d, tm=tm3, tn=d, out_dtypes=(F32, U32), out_div=(1, 2),
        epilogue=mix_epilogue, row_split=2,
        extras=[(h0, (tm3, d), lambda j, i: (i, 0)),
                (p["ln_mix_g"].reshape(1, d), (1, d), lambda j, i: (0, 0)),
                (p["ln_mix_b"].reshape(1, d), (1, d), lambda j, i: (0, 0))])

    n_exp = p["w_router"].shape[1]
    idx, gate, rank, cnt = _router(h1, p["w_router"], p["b_router"])
    mtm = MOE_TM
    xmax = MOE_XMAX
    n_rows = t * TOP_K + n_exp * mtm
    n_items = n_exp + n_rows // xmax
    pos, item_e, item_start, item_nblk = _moe_plan(
        idx[:, :TOP_K], rank[:, :TOP_K], cnt[0, :n_exp], tm=mtm, xmax=xmax, n_items=n_items)
    rows = _dispatch(h1p, pos, n_rows)
    y = _experts(rows, item_e, item_start, item_nblk, p["w_gate_up"], p["b_gate_up"], p["w_down"], p["b_down"],
                 tm=mtm, xmax=xmax)
    h2 = _combine(y, pos, gate, h1, p["ln_ffn_g"], p["ln_ffn_b"], alpha)
    return h2


_LAYER_PARAMS = ("w_in", "gdn_conv_w", "gdn_a_log", "gdn_dt_bias", "gdn_norm_w", "w_out_gdn", "sc_conv_w",
                 "w_out_sc", "w_out", "ln_mix_g", "ln_mix_b", "w_router", "b_router", "w_gate_up", "b_gate_up",
                 "w_down", "b_down", "ln_ffn_g", "ln_ffn_b")


def kernel(x, ln_in_g, ln_in_b, w_in, gdn_conv_w, gdn_a_log, gdn_dt_bias, gdn_norm_w, w_out_gdn, sc_conv_w,
           w_out_sc, w_out, ln_mix_g, ln_mix_b, w_router, b_router, w_gate_up, b_gate_up, w_down, b_down,
           ln_ffn_g, ln_ffn_b):
    stacked = dict(zip(_LAYER_PARAMS, (w_in, gdn_conv_w, gdn_a_log, gdn_dt_bias, gdn_norm_w, w_out_gdn,
                                       sc_conv_w, w_out_sc, w_out, ln_mix_g, ln_mix_b, w_router, b_router,
                                       w_gate_up, b_gate_up, w_down, b_down, ln_ffn_g, ln_ffn_b)))
    bsz, seq, d = x.shape
    depth = w_in.shape[0]
    alpha = (2 * depth) ** 0.25
    h, hb = _ln_in(x.reshape(bsz * seq, d), ln_in_g, ln_in_b)
    for l in range(depth):
        p = {name: arr[l] for name, arr in stacked.items()}
        h = _layer(h, hb, p, bsz=bsz, seq=seq, alpha=alpha)
        if l + 1 < depth:
            hb = h.astype(BF16)
    return h.reshape(bsz, seq, d)
```

```python
import functools

import jax
import jax.numpy as jnp
from jax import lax
from jax.experimental import pallas as pl
from jax.experimental.pallas import tpu as pltpu

F32 = jnp.float32
BF16 = jnp.bfloat16
I32 = jnp.int32
U32 = jnp.uint32

LANES = 128
SUBLANES = 8
MXU_N = 256
VMEM_LIMIT = 60 << 20

TOP_K = 4
SWIGLU_LIMIT = 7.0
SWIGLU_ALPHA = 1.702
LN_EPS = 1e-5
RMS_EPS = 1e-6
GDN_CHUNK = 64
GDN_STEP = 512
GDN_QK_GROUP = 4
GDN_INV_GROUP = 16
MOE_TM = 256
MOE_XMAX = 1280
MOE_TF = 512


def _params(sem):
    return pltpu.CompilerParams(dimension_semantics=sem, vmem_limit_bytes=VMEM_LIMIT)


def _pick(n, candidates):
    for c in candidates:
        if n % c == 0:
            return c
    raise ValueError(f"no tile for {n} in {candidates}")


def _sigmoid(x):
    return 0.5 * jnp.tanh(0.5 * x) + 0.5


def _layer_norm(xf, g, b):
    mu = jnp.mean(xf, axis=-1, keepdims=True)
    xc = xf - mu
    var = jnp.mean(xc * xc, axis=-1, keepdims=True)
    return xc * lax.rsqrt(var + LN_EPS) * g + b


def _pack_halves_bf16(x):
    half = x.shape[1] // 2
    lo = lax.bitcast_convert_type(x[:, :half].astype(BF16).astype(F32), U32) >> 16
    hi = lax.bitcast_convert_type(x[:, half:].astype(BF16).astype(F32), U32) & jnp.uint32(0xFFFF0000)
    return lo | hi


def _unpack_halves_bf16(p):
    lo = lax.bitcast_convert_type(p << 16, F32).astype(BF16)
    hi = lax.bitcast_convert_type(p & jnp.uint32(0xFFFF0000), F32).astype(BF16)
    return lo, hi


def _ln_in_kernel(x_ref, g_ref, b_ref, h_ref, hb_ref):
    h = _layer_norm(x_ref[...], g_ref[...], b_ref[...])
    h_ref[...] = h
    hb_ref[...] = h.astype(BF16)


def _ln_in(x2, g, b):
    t, d = x2.shape
    tm = _pick(t, (512, 256, 128, 64, 32, 16))
    return pl.pallas_call(
        _ln_in_kernel,
        grid=(t // tm,),
        in_specs=[pl.BlockSpec((tm, d), lambda i: (i, 0)),
                  pl.BlockSpec((1, d), lambda i: (0, 0)),
                  pl.BlockSpec((1, d), lambda i: (0, 0))],
        out_specs=[pl.BlockSpec((tm, d), lambda i: (i, 0)),
                   pl.BlockSpec((tm, d), lambda i: (i, 0))],
        out_shape=[jax.ShapeDtypeStruct((t, d), F32), jax.ShapeDtypeStruct((t, d), BF16)],
        compiler_params=_params(("arbitrary",)),
        name="ln_in",
    )(x2, g.reshape(1, d), b.reshape(1, d))


def _mm_kernel(*refs, shift, w_t, row_split, n_extra, n_out, epilogue):
    n_w = 2 if shift else 1
    x_ref, w_refs = refs[0], refs[1:1 + n_w]
    extras = refs[1 + n_w:1 + n_w + n_extra]
    o_refs = refs[1 + n_w + n_extra:1 + n_w + n_extra + n_out]
    wb_ref = refs[-1]
    n_axis = 0 if w_t else 1

    @pl.when(pl.program_id(1) == 0)
    def _():
        if shift:
            tn = w_refs[0].shape[n_axis]
            w = jnp.concatenate([lax.slice_in_dim(w_refs[0][...], shift, tn, axis=n_axis),
                                 lax.slice_in_dim(w_refs[1][...], 0, shift, axis=n_axis)], axis=n_axis)
        else:
            w = w_refs[0][...]
        wb_ref[...] = w.astype(BF16)

    tm = x_ref.shape[0]
    sub = tm // row_split
    for r in range(row_split):
        rows = slice(r * sub, (r + 1) * sub)
        acc = lax.dot_general(x_ref[rows, :], wb_ref[...], (((1,), (1 if w_t else 0,)), ((), ())),
                              preferred_element_type=F32)
        outs = epilogue(acc, *[e[rows, :] if e.shape[0] == tm else e[...] for e in extras])
        for o_ref, o in zip(o_refs, outs):
            o_ref[rows, :] = o.astype(o_ref.dtype)


def _matmul(x, w, *, col0, n_cols, tm, tn, out_dtypes, epilogue=None, extras=(), out_div=None, w_t=False,
            row_split=1):
    t, k = x.shape
    k_axis, n_axis = (1, 0) if w_t else (0, 1)
    assert w.shape[k_axis] == k and n_cols % tn == 0 and t % tm == 0 and col0 + n_cols <= w.shape[n_axis]
    j0, shift = divmod(col0, tn)
    if epilogue is None:
        epilogue = lambda acc: (acc,)
    if out_div is None:
        out_div = (1,) * len(out_dtypes)
    assert tm % (row_split * 16) == 0
    kern = functools.partial(_mm_kernel, shift=shift, w_t=w_t, row_split=row_split, n_extra=len(extras),
                             n_out=len(out_dtypes), epilogue=epilogue)
    w_block = (tn, k) if w_t else (k, tn)
    w_index = lambda jj: (jj, 0) if w_t else (0, jj)
    in_specs = [pl.BlockSpec((tm, k), lambda j, i: (i, 0)),
                pl.BlockSpec(w_block, lambda j, i: w_index(j + j0))]
    ws = [w]
    if shift:
        in_specs.append(pl.BlockSpec(w_block, lambda j, i: w_index(j + j0 + 1)))
        ws.append(w)
    in_specs += [pl.BlockSpec(bs, im) for (_, bs, im) in extras]
    return pl.pallas_call(
        kern,
        grid=(n_cols // tn, t // tm),
        in_specs=in_specs,
        out_specs=[pl.BlockSpec((tm, tn // dv), lambda j, i: (i, j)) for dv in out_div],
        out_shape=[jax.ShapeDtypeStruct((t, n_cols // dv), dt) for dt, dv in zip(out_dtypes, out_div)],
        scratch_shapes=[pltpu.VMEM(w_block, BF16)],
        compiler_params=_params(("arbitrary", "arbitrary")),
        name="matmul",
    )(x, *ws, *[a for (a, _, _) in extras])


def _gates_kernel(ba_ref, alog_ref, dt_ref, col_ref, row_ref, *, hv):
    x = ba_ref[...]
    lane = lax.broadcasted_iota(I32, x.shape, 1)
    beta = _sigmoid(x)
    xs = x + dt_ref[...]
    softplus = jnp.maximum(xs, 0.0) + jnp.log(1.0 + jnp.exp(-jnp.abs(xs)))
    g = -jnp.exp(alog_ref[...]) * softplus
    g = jnp.where((lane >= hv) & (lane < 2 * hv), g, 0.0)
    n = x.shape[0]
    row = lax.broadcasted_iota(I32, x.shape, 0) & (GDN_CHUNK - 1)
    cum = g
    rev = g
    sh = 1
    while sh < GDN_CHUNK:
        cum = cum + jnp.where(row >= sh, pltpu.roll(cum, sh, 0), 0.0)
        rev = rev + jnp.where(row < GDN_CHUNK - sh, pltpu.roll(rev, n - sh, 0), 0.0)
        sh *= 2
    e_cum = pltpu.roll(jnp.exp(cum), hv, 1)
    e_rest = pltpu.roll(jnp.exp(rev - g), 2 * hv, 1)
    out = jnp.where(lane < hv, beta, jnp.where(lane < 2 * hv, cum, jnp.where(lane < 3 * hv, e_cum, e_rest)))
    col_ref[...] = out
    row_ref[...] = out.T


def _gdn_gates(ba, a_log, dt_bias, hv):
    assert 4 * hv <= LANES
    t = ba.shape[0]
    tm = _pick(t, (512, 256, 128))
    pad = lambda v: jnp.zeros((1, LANES), F32).at[0, hv:2 * hv].set(v.astype(F32))
    return pl.pallas_call(
        functools.partial(_gates_kernel, hv=hv),
        grid=(t // tm,),
        in_specs=[pl.BlockSpec((tm, LANES), lambda i: (i, 0)),
                  pl.BlockSpec((1, LANES), lambda i: (0, 0)),
                  pl.BlockSpec((1, LANES), lambda i: (0, 0))],
        out_specs=[pl.BlockSpec((tm, LANES), lambda i: (i, 0)),
                   pl.BlockSpec((LANES, tm), lambda i: (0, i))],
        out_shape=[jax.ShapeDtypeStruct((t, LANES), F32), jax.ShapeDtypeStruct((LANES, t), F32)],
        compiler_params=_params(("arbitrary",)),
        name="gdn_gates",
    )(ba, pad(a_log), pad(dt_bias))


def _causal_conv(x, w, ext_ref):
    n = x.shape[0]
    kw = w.shape[0]
    ext_ref[SUBLANES:, :] = x
    acc = x * w[kw - 1:kw]
    for d in range(1, kw):
        acc = acc + ext_ref[SUBLANES - d:SUBLANES - d + n, :] * w[kw - 1 - d:kw - d]
    ext_ref[:SUBLANES, :] = x[n - SUBLANES:]
    return acc


def _silu(x):
    h = 0.5 * x
    return h + h * jnp.tanh(h)


def _unit_lower_inverses(lows, ii, jj):
    c = lows[0].shape[0]
    eye = jnp.where(ii == jj, 1.0, 0.0)
    pair = (ii >> 1) == (jj >> 1)
    xbs = [(eye - jnp.where(pair, low, 0.0)).astype(BF16) for low in lows]
    lbs = [low.astype(BF16) for low in lows]
    zero = jnp.zeros((c, c), BF16)
    s = 1
    while (2 << s) <= c:
        m = ((ii >> (s + 1)) == (jj >> (s + 1))) & ((ii >> s) != (jj >> s))
        ts = [jnp.dot(jnp.where(m, lb, zero), xb, preferred_element_type=F32) for lb, xb in zip(lbs, xbs)]
        xbs = [jnp.dot(xb, (eye - t).astype(BF16), preferred_element_type=F32).astype(BF16)
               for xb, t in zip(xbs, ts)]
        s += 1
    return xbs


def _gdn_kernel(q_ref, k_ref, v_ref, z_ref, cwq_ref, cwk_ref, cwv_ref, gcol_ref, grow_ref, nw_ref,
                o_ref, s_ref, tq_ref, tk_ref, tv_ref, qn_ref, kn_ref, vn_ref, u_ref, wq_ref, attn_ref, kd_ref,
                *, nq, vper, dk):
    @pl.when(pl.program_id(2) == 0)
    def _():
        s_ref[...] = jnp.zeros_like(s_ref)
        for t_ref in (tq_ref, tk_ref, tv_ref):
            t_ref[:SUBLANES, :] = jnp.zeros((SUBLANES, t_ref.shape[1]), F32)

    tb = q_ref.shape[0]
    c = GDN_CHUNK
    nc = tb // c
    nv = nq * vper
    head = lambda h: slice(h * dk, (h + 1) * dk)

    q = _silu(_causal_conv(q_ref[...].astype(F32), cwq_ref[...], tq_ref))
    k = _silu(_causal_conv(k_ref[...].astype(F32), cwk_ref[...], tk_ref))
    for hq in range(nq):
        qh = q[:, head(hq)]
        kh = k[:, head(hq)]
        qn_ref[:, head(hq)] = qh * (lax.rsqrt(jnp.sum(qh * qh, axis=-1, keepdims=True) + RMS_EPS) * (dk ** -0.5))
        kn_ref[:, head(hq)] = kh * lax.rsqrt(jnp.sum(kh * kh, axis=-1, keepdims=True) + RMS_EPS)

    cols = gcol_ref[...]
    grows = grow_ref[...]
    nw = nw_ref[...]
    ii = lax.broadcasted_iota(I32, (c, c), 0)
    jj = lax.broadcasted_iota(I32, (c, c), 1)
    incl = ii >= jj
    strict = ii > jj

    def col(ci, which, h):
        return cols[ci * c:(ci + 1) * c, which * nv + h:which * nv + h + 1]

    def state_step(ci):
        rows = slice(ci * c, (ci + 1) * c)
        ps = [ci * nv + h for h in range(nv)]
        r1 = [jnp.dot(wq_ref[p], s_ref[h].astype(BF16), preferred_element_type=F32) for h, p in enumerate(ps)]
        v_new = [(u_ref[p] - r[:c]).astype(BF16) for p, r in zip(ps, r1)]
        outs = [r[c:] + jnp.dot(attn_ref[p], vn, preferred_element_type=F32) for p, r, vn in zip(ps, r1, v_new)]
        for h, (p, vn) in enumerate(zip(ps, v_new)):
            g_tot = cols[(ci + 1) * c - 1:(ci + 1) * c, 2 * nv + h:2 * nv + h + 1]
            s_ref[h] = s_ref[h] * g_tot + lax.dot_general(
                kd_ref[p], vn, (((0,), (0,)), ((), ())), preferred_element_type=F32)
        for h, o in enumerate(outs):
            zc = z_ref[rows, head(h)].astype(F32)
            var = jnp.mean(o * o, axis=-1, keepdims=True)
            o_ref[rows, head(h)] = (o * lax.rsqrt(var + RMS_EPS) * nw * _silu(zc)).astype(BF16)

    problems = [(ci, h) for ci in range(nc) for h in range(nv)]
    gram = {}
    chunks_done = 0
    for g0 in range(0, len(problems), GDN_INV_GROUP):
        group = problems[g0:g0 + GDN_INV_GROUP]
        lows = []
        for ci, h in group:
            hq = h // vper
            rows = slice(ci * c, (ci + 1) * c)
            if (ci, hq) not in gram:
                kcb = kn_ref[rows, head(hq)].astype(BF16)
                qcb = qn_ref[rows, head(hq)].astype(BF16)
                gram[(ci, hq)] = lax.dot_general(jnp.concatenate([kcb, qcb], axis=0), kcb,
                                                 (((1,), (1,)), ((), ())), preferred_element_type=F32)
            a = gram[(ci, hq)]
            grow = grows[h:h + 1, rows]
            decay = jnp.where(incl, jnp.exp(jnp.minimum(col(ci, 1, h) - grow, 0.0)), 0.0)
            lows.append(jnp.where(strict, col(ci, 0, h) * a[:c] * decay, 0.0))
            attn_ref[ci * nv + h] = jnp.where(incl, a[c:] * decay, 0.0).astype(BF16)
        if g0 == 0:
            vn_ref[...] = _silu(_causal_conv(v_ref[...].astype(F32), cwv_ref[...], tv_ref))
        xs = _unit_lower_inverses(lows, ii, jj)
        for x, (ci, h) in zip(xs, group):
            hq = h // vper
            rows = slice(ci * c, (ci + 1) * c)
            p = ci * nv + h
            kc = kn_ref[rows, head(hq)]
            bcol = col(ci, 0, h)
            eg = col(ci, 2, h)
            rhs = jnp.concatenate([vn_ref[rows, head(h)] * bcol, kc * (bcol * eg)], axis=1).astype(BF16)
            uw = jnp.dot(x, rhs, preferred_element_type=F32)
            u_ref[p] = uw[:, :dk]
            wq_ref[p, :c, :] = uw[:, dk:].astype(BF16)
            wq_ref[p, c:, :] = (qn_ref[rows, head(hq)] * eg).astype(BF16)
            kd_ref[p] = (kc * col(ci, 3, h)).astype(BF16)
        chunks_ready = (g0 + len(group)) // nv
        for ci in range(chunks_done, chunks_ready):
            state_step(ci)
        chunks_done = chunks_ready


def _gdn(proj_a, conv_w, gcol, grow, norm_w, *, bsz, seq, hq, hv, dk):
    t = bsz * seq
    vper = hv // hq
    nq = _pick(hq, (GDN_QK_GROUP, 2, 1))
    nv = nq * vper
    ng = hq // nq
    tb = _pick(seq, (GDN_STEP, 128, 64))
    ns = seq // tb
    nprob = (tb // GDN_CHUNK) * nv
    qw = nq * dk
    vw = nv * dk
    v_blk0 = 2 * hq * dk // vw
    z_blk0 = (2 * hq * dk + hv * dk) // vw
    kw = conv_w.shape[0]
    gcol = gcol[:, :4 * hv].reshape(t, 4, ng, nv).transpose(2, 0, 1, 3).reshape(ng, t, 4 * nv)
    grow = grow[hv:2 * hv].reshape(ng, nv, t)
    rowblk = lambda b, g, s: b * ns + s
    kern = functools.partial(_gdn_kernel, nq=nq, vper=vper, dk=dk)
    return pl.pallas_call(
        kern,
        grid=(bsz, ng, ns),
        in_specs=[
            pl.BlockSpec((tb, qw), lambda b, g, s: (rowblk(b, g, s), g)),
            pl.BlockSpec((tb, qw), lambda b, g, s: (rowblk(b, g, s), ng + g)),
            pl.BlockSpec((tb, vw), lambda b, g, s: (rowblk(b, g, s), v_blk0 + g)),
            pl.BlockSpec((tb, vw), lambda b, g, s: (rowblk(b, g, s), z_blk0 + g)),
            pl.BlockSpec((kw, qw), lambda b, g, s: (0, g)),
            pl.BlockSpec((kw, qw), lambda b, g, s: (0, ng + g)),
            pl.BlockSpec((kw, vw), lambda b, g, s: (0, v_blk0 + g)),
            pl.BlockSpec((None, tb, 4 * nv), lambda b, g, s: (g, rowblk(b, g, s), 0)),
            pl.BlockSpec((None, nv, tb), lambda b, g, s: (g, 0, rowblk(b, g, s))),
            pl.BlockSpec((1, dk), lambda b, g, s: (0, 0)),
        ],
        out_specs=pl.BlockSpec((tb, vw), lambda b, g, s: (rowblk(b, g, s), g)),
        out_shape=jax.ShapeDtypeStruct((t, hv * dk), BF16),
        scratch_shapes=[pltpu.VMEM((nv, dk, dk), F32),
                        pltpu.VMEM((tb + SUBLANES, qw), F32),
                        pltpu.VMEM((tb + SUBLANES, qw), F32),
                        pltpu.VMEM((tb + SUBLANES, vw), F32),
                        pltpu.VMEM((tb, qw), F32),
                        pltpu.VMEM((tb, qw), F32),
                        pltpu.VMEM((tb, vw), F32),
                        pltpu.VMEM((nprob, GDN_CHUNK, dk), F32),
                        pltpu.VMEM((nprob, 2 * GDN_CHUNK, dk), BF16),
                        pltpu.VMEM((nprob, GDN_CHUNK, GDN_CHUNK), BF16),
                        pltpu.VMEM((nprob, GDN_CHUNK, dk), BF16)],
        compiler_params=_params(("arbitrary", "arbitrary", "arbitrary")),
        name="gdn",
    )(proj_a, proj_a, proj_a, proj_a, conv_w, conv_w, conv_w, gcol, grow, norm_w.reshape(1, dk))


def _sc_kernel(b_ref, c_ref, x_ref, w_ref, u_ref, tail_ref):
    @pl.when(pl.program_id(1) == 0)
    def _():
        tail_ref[:SUBLANES, :] = jnp.zeros((SUBLANES, tail_ref.shape[1]), F32)

    p = c_ref[...].astype(F32) * x_ref[...].astype(F32)
    u_ref[...] = (b_ref[...].astype(F32) * _causal_conv(p, w_ref[...], tail_ref)).astype(BF16)


def _short_conv(proj_b, conv_w, *, bsz, seq, width):
    t = bsz * seq
    ts = _pick(seq, (512, 256, 128, 64))
    ns = seq // ts
    kw = conv_w.shape[0]
    return pl.pallas_call(
        _sc_kernel,
        grid=(bsz, ns),
        in_specs=[pl.BlockSpec((ts, width), lambda b, s: (b * ns + s, 0)),
                  pl.BlockSpec((ts, width), lambda b, s: (b * ns + s, 1)),
                  pl.BlockSpec((ts, width), lambda b, s: (b * ns + s, 2)),
                  pl.BlockSpec((kw, width), lambda b, s: (0, 0))],
        out_specs=pl.BlockSpec((ts, width), lambda b, s: (b * ns + s, 0)),
        out_shape=jax.ShapeDtypeStruct((t, width), BF16),
        scratch_shapes=[pltpu.VMEM((ts + SUBLANES, width), F32)],
        compiler_params=_params(("arbitrary", "arbitrary")),
        name="short_conv",
    )(proj_b, proj_b, proj_b, conv_w)


def _router_kernel(h_ref, w_ref, b_ref, idx_ref, gate_ref, rank_ref, cnt_ref, run_ref):
    @pl.when(pl.program_id(0) == 0)
    def _():
        run_ref[...] = jnp.zeros_like(run_ref)

    tm = h_ref.shape[0]
    h = h_ref[...]
    h_hi = h.astype(BF16)
    h_lo = (h - h_hi.astype(F32)).astype(BF16)
    w2 = w_ref[...]
    first = jnp.dot(h_hi, w2, preferred_element_type=F32)
    logits = (first[:, :LANES] + first[:, LANES:]
              + jnp.dot(h_lo, w2[:, :LANES], preferred_element_type=F32) + b_ref[...])
    lane = lax.broadcasted_iota(I32, logits.shape, 1)
    lane_f = lane.astype(F32)
    cur = logits
    idxs, vals = [], []
    for _ in range(TOP_K):
        m = jnp.max(cur, axis=-1, keepdims=True)
        ix = jnp.min(jnp.where(cur == m, lane_f, float(LANES)), axis=-1, keepdims=True).astype(I32)
        idxs.append(ix)
        vals.append(m)
        cur = jnp.where(lane == ix, -jnp.inf, cur)
    es = [jnp.exp(val - vals[0]) for val in vals]
    den = es[0]
    for e in es[1:]:
        den = den + e
    onehot = jnp.zeros(logits.shape, F32)
    for ix in idxs:
        onehot = onehot + jnp.where(lane == ix, 1.0, 0.0)
    ri = lax.broadcasted_iota(I32, (tm, tm), 0)
    ci = lax.broadcasted_iota(I32, (tm, tm), 1)
    tri = jnp.where(ri > ci, 1.0, 0.0).astype(BF16)
    before = jnp.dot(tri, onehot.astype(BF16), preferred_element_type=F32) + run_ref[...]
    idx_out = jnp.zeros(logits.shape, I32)
    rank_out = jnp.zeros(logits.shape, I32)
    gate_out = jnp.zeros(logits.shape, F32)
    for kk in range(TOP_K):
        rank = jnp.sum(jnp.where(lane == idxs[kk], before, 0.0), axis=-1, keepdims=True).astype(I32)
        idx_out = jnp.where(lane == kk, idxs[kk], idx_out)
        rank_out = jnp.where(lane == kk, rank, rank_out)
        gate_out = jnp.where(lane == kk, es[kk] / den, gate_out)
    idx_ref[...] = idx_out
    rank_ref[...] = rank_out
    gate_ref[...] = gate_out
    run_ref[...] = run_ref[...] + jnp.sum(onehot, axis=0, keepdims=True)
    cnt_ref[...] = run_ref[...].astype(I32)


def _router(h, w_router, b_router):
    t, d = h.shape
    e = w_router.shape[1]
    tm = _pick(t, (256, 128, 64, 32, 16, 8))
    wp = jnp.zeros((d, LANES), F32).at[:, :e].set(w_router)
    w_hi = wp.astype(BF16)
    w_lo = (wp - w_hi.astype(F32)).astype(BF16)
    w2 = jnp.concatenate([w_hi, w_lo], axis=1)
    bp = jnp.full((1, LANES), -1e30, F32).at[0, :e].set(b_router)
    tile = pl.BlockSpec((tm, LANES), lambda i: (i, 0))
    return pl.pallas_call(
        _router_kernel,
        grid=(t // tm,),
        in_specs=[pl.BlockSpec((tm, d), lambda i: (i, 0)),
                  pl.BlockSpec((d, 2 * LANES), lambda i: (0, 0)),
                  pl.BlockSpec((1, LANES), lambda i: (0, 0))],
        out_specs=[tile, tile, tile, pl.BlockSpec((1, LANES), lambda i: (0, 0))],
        out_shape=[jax.ShapeDtypeStruct((t, LANES), I32), jax.ShapeDtypeStruct((t, LANES), F32),
                   jax.ShapeDtypeStruct((t, LANES), I32), jax.ShapeDtypeStruct((1, LANES), I32)],
        scratch_shapes=[pltpu.VMEM((1, LANES), F32)],
        compiler_params=_params(("arbitrary",)),
        name="router",
    )(h, w2, bp)


def _row_copy(src, s_row, dst, d_row, sem):
    return pltpu.make_async_copy(src.at[pl.ds(s_row, 1)], dst.at[pl.ds(d_row, 1)], sem)


def _dispatch_kernel(pos_ref, x_ref, rows_in_ref, rows_ref, sem):
    del rows_in_ref
    tm = x_ref.shape[0]
    base = pl.program_id(0) * (tm * TOP_K)

    for t in range(tm):
        for kk in range(TOP_K):
            _row_copy(x_ref, t, rows_ref, pos_ref[base + (t * TOP_K + kk)], sem).start(priority=kk & 1)
    n = tm * TOP_K
    pltpu.make_async_copy(rows_ref.at[pl.ds(0, n)], rows_ref.at[pl.ds(0, n)], sem).wait()


def _dispatch(h, pos, n_rows):
    t, d = h.shape
    tm = _pick(t, (128, 64, 32, 16, 8))
    rows0 = jnp.zeros((n_rows, d), h.dtype)
    return pl.pallas_call(
        _dispatch_kernel,
        grid_spec=pltpu.PrefetchScalarGridSpec(
            num_scalar_prefetch=1,
            grid=(t // tm,),
            in_specs=[pl.BlockSpec((tm, d), lambda i, pos: (i, 0)),
                      pl.BlockSpec(memory_space=pl.ANY)],
            out_specs=pl.BlockSpec(memory_space=pl.ANY),
            scratch_shapes=[pltpu.SemaphoreType.DMA(())],
        ),
        out_shape=jax.ShapeDtypeStruct((n_rows, d), h.dtype),
        input_output_aliases={2: 0},
        compiler_params=_params(("arbitrary",)),
        name="dispatch",
    )(pos, h, rows0)


def _swiglu_interleaved(gu):
    lane = lax.broadcasted_iota(I32, (gu.shape[0], LANES), 1)
    even = (lane & 1) == 0
    parts = []
    for c in range(gu.shape[1] // LANES):
        g = gu[:, c * LANES:(c + 1) * LANES]
        gt = jnp.minimum(g, SWIGLU_LIMIT)
        glu = gt * _sigmoid(SWIGLU_ALPHA * gt)
        up = jnp.clip(g, -SWIGLU_LIMIT, SWIGLU_LIMIT) + 1.0
        parts.append(jnp.where(even, pltpu.roll(up, LANES - 1, 1) * glu, 0.0))
    outs = [parts[2 * m] + pltpu.roll(parts[2 * m + 1], 1, 1) for m in range(len(parts) // 2)]
    return jnp.concatenate(outs, axis=1) if len(outs) > 1 else outs[0]


def _interleave_rows_bf16(w_ref, out_ref):
    half = LANES // 2
    for m in range(w_ref.shape[0] // LANES):
        lo = w_ref[m * LANES:m * LANES + half, :].astype(BF16).astype(F32)
        hi = w_ref[m * LANES + half:(m + 1) * LANES, :].astype(BF16).astype(F32)
        packed = (lax.bitcast_convert_type(lo, jnp.uint32) >> 16) | (
            lax.bitcast_convert_type(hi, jnp.uint32) & jnp.uint32(0xFFFF0000))
        out_ref[m * LANES:(m + 1) * LANES, :] = pltpu.bitcast(packed, BF16)


def _expert_kernel(ie_ref, ist_ref, inb_ref, ihalf_ref, rows_ref, wgu_ref, bgu_ref, wd_ref, bd_ref, y_ref,
                   xb_ref, acc_ref, stage_ref, wgub_ref, wdb_ref, ost_ref, sem_in, sem_out, *, tm, n_j):
    del ie_ref
    wi = pl.program_id(0)
    j = pl.program_id(1)
    n_items = pl.num_programs(0)
    nblk = inb_ref[wi]
    half_last = ihalf_ref[wi]
    nfull = nblk - half_last
    half = stage_ref.shape[1]

    def block(b):
        return pl.ds(pl.multiple_of(b * tm, tm), tm)

    def hbm_block(item, b):
        return pl.ds(pl.multiple_of(ist_ref[item] + b * tm, tm), tm)

    def rows_copy(item, b):
        return pltpu.make_async_copy(rows_ref.at[hbm_block(item, b)], stage_ref.at[block(b)], sem_in)

    def y_copy(item, b):
        slot = b & 1
        return pltpu.make_async_copy(ost_ref.at[slot], y_ref.at[hbm_block(item, b)], sem_out.at[slot])

    def drain_stores(item):
        n = inb_ref[item]
        for back in (2, 1):
            @pl.when(n >= back)
            def _():
                y_copy(item, n - back).wait()

    def for_blocks(item, fn):
        def body(b, carry):
            fn(item, b)
            return carry
        lax.fori_loop(0, inb_ref[item], body, 0)

    def unpack(item, b):
        lo, hi = _unpack_halves_bf16(stage_ref[block(b), :])
        xb_ref[block(b), :half] = lo
        xb_ref[block(b), half:] = hi

    @pl.when(j == 0)
    def _():
        @pl.when(wi == 0)
        def _():
            for_blocks(0, lambda it, b: rows_copy(it, b).start())
            acc_ref[...] = jnp.zeros_like(acc_ref)
            ost_ref[...] = jnp.zeros_like(ost_ref)

        @pl.when(wi > 0)
        def _():
            drain_stores(wi - 1)

        for_blocks(wi, lambda it, b: rows_copy(it, b).wait())
        for_blocks(wi, unpack)

        @pl.when(wi + 1 < n_items)
        def _():
            for_blocks(wi + 1, lambda it, b: rows_copy(it, b).start())

    @pl.when(nblk > 0)
    def _():
        bias = bgu_ref[...]
        n_ct, _, ct = wgub_ref.shape

        def convert_weights():
            for c in range(n_ct):
                wgub_ref[c] = wgu_ref[:, c * ct:(c + 1) * ct].astype(BF16)
            _interleave_rows_bf16(wd_ref, wdb_ref)

        def mlp(b, m):
            x = xb_ref[pl.ds(pl.multiple_of(b * tm, tm), m), :]
            gu = jnp.concatenate([jnp.dot(x, wgub_ref[c], preferred_element_type=F32) for c in range(n_ct)],
                                 axis=1) + bias
            act = _swiglu_interleaved(gu).astype(BF16)
            return jnp.dot(act, wdb_ref[...], preferred_element_type=F32)

        def accumulate(b, m=tm):
            rows = pl.ds(pl.multiple_of(b * tm, tm), m)
            prev = jnp.where(j == 0, jnp.broadcast_to(bd_ref[...], (m, acc_ref.shape[1])), acc_ref[rows, :])
            acc_ref[rows, :] = prev + mlp(b, m)

        def finish(b, m=tm):
            prev = acc_ref[pl.ds(pl.multiple_of(b * tm, tm), m), :] if n_j > 1 else bd_ref[...]
            ost_ref[b & 1, 0:m, :] = _pack_halves_bf16(prev + mlp(b, m))
            y_copy(wi, b).start()

        def run(fn, stores):
            @pl.when(nfull >= 2)
            def _():
                convert_weights()
                fn(0)
                fn(1)

            @pl.when(nfull == 1)
            def _():
                convert_weights()
                fn(0)

            def pair(i, carry):
                if stores:
                    y_copy(wi, 2 * i - 2).wait()
                    y_copy(wi, 2 * i - 1).wait()
                fn(2 * i)
                fn(2 * i + 1)
                return carry
            lax.fori_loop(1, nfull // 2, pair, 0)

            @pl.when(((nfull & 1) == 1) & (nfull >= 3))
            def _():
                if stores:
                    y_copy(wi, nfull - 3).wait()
                fn(nfull - 1)

            @pl.when((half_last == 1) & (nfull == 0))
            def _():
                convert_weights()
                fn(0, tm // 2)

            @pl.when((half_last == 1) & (nfull > 0))
            def _():
                if stores:
                    @pl.when(nfull >= 2)
                    def _():
                        y_copy(wi, nfull - 2).wait()
                fn(nfull, tm // 2)

        if n_j > 1:
            @pl.when(j < n_j - 1)
            def _():
                run(accumulate, False)

        @pl.when(j == n_j - 1)
        def _():
            run(finish, True)

    @pl.when((wi == n_items - 1) & (j == n_j - 1))
    def _():
        drain_stores(wi)
        ost_ref[0] = jnp.zeros(ost_ref.shape[1:], U32)

        def fill(b, carry):
            cp = pltpu.make_async_copy(ost_ref.at[0], y_ref.at[pl.ds(pl.multiple_of(b * tm, tm), tm)],
                                       sem_out.at[0])
            cp.start()
            cp.wait()
            return carry
        lax.fori_loop(ist_ref[n_items] // tm, y_ref.shape[0] // tm, fill, 0)


def _experts(rows, item_e, item_start, item_nblk, item_half, w_gate_up, b_gate_up, w_down, b_down, *, tm, xmax):
    n_rows = rows.shape[0]
    n_exp, d, f2 = w_gate_up.shape
    assert rows.shape[1] * 2 == d
    f = f2 // 2
    tf = _pick(f, (MOE_TF, LANES))
    n_j = f // tf
    ct = min(2 * tf, MXU_N)
    n_items = item_e.shape[0]

    def jeff(j, inb, wi):
        return jnp.where(inb[wi] > 0, j, n_j - 1)

    kern = functools.partial(_expert_kernel, tm=tm, n_j=n_j)
    return pl.pallas_call(
        kern,
        grid_spec=pltpu.PrefetchScalarGridSpec(
            num_scalar_prefetch=4,
            grid=(n_items, n_j),
            in_specs=[
                pl.BlockSpec(memory_space=pl.ANY),
                pl.BlockSpec((None, d, 2 * tf), lambda wi, j, ie, ist, inb, ih: (ie[wi], 0, jeff(j, inb, wi))),
                pl.BlockSpec((None, 1, 2 * tf), lambda wi, j, ie, ist, inb, ih: (ie[wi], 0, jeff(j, inb, wi))),
                pl.BlockSpec((None, tf, d), lambda wi, j, ie, ist, inb, ih: (ie[wi], jeff(j, inb, wi), 0)),
                pl.BlockSpec((None, 1, d), lambda wi, j, ie, ist, inb, ih: (ie[wi], 0, 0)),
            ],
            out_specs=pl.BlockSpec(memory_space=pl.ANY),
            scratch_shapes=[pltpu.VMEM((xmax, d), BF16),
                            pltpu.VMEM((xmax, d), F32),
                            pltpu.VMEM((xmax, d // 2), U32),
                            pltpu.VMEM((2 * tf // ct, d, ct), BF16),
                            pltpu.VMEM((tf, d), BF16),
                            pltpu.VMEM((2, tm, d // 2), U32),
                            pltpu.SemaphoreType.DMA(()),
                            pltpu.SemaphoreType.DMA((2,))],
        ),
        out_shape=jax.ShapeDtypeStruct((n_rows, d // 2), U32),
        compiler_params=_params(("arbitrary", "arbitrary")),
        name="experts",
    )(item_e, item_start, item_nblk, item_half, rows, w_gate_up, b_gate_up.reshape(n_exp, 1, f2),
      w_down, b_down.reshape(n_exp, 1, d))


def _combine_kernel(pos_ref, y_ref, gate_ref, h_ref, g_ref, b_ref, o_ref, ybuf0_ref, ybuf1_ref, sem,
                    *, alpha):
    tm = h_ref.shape[0]
    i = pl.program_id(0)
    n_tiles = pl.num_programs(0)
    n = tm * TOP_K
    bufs = (ybuf0_ref, ybuf1_ref)

    def gather(tile, s):
        base = tile * n
        for t in range(tm):
            for kk in range(TOP_K):
                _row_copy(y_ref, pos_ref[base + (t * TOP_K + kk)], bufs[s], kk * tm + t,
                          sem.at[s]).start(priority=kk & 1)

    def wait_buf(s):
        pltpu.make_async_copy(y_ref.at[pl.ds(0, n)], bufs[s], sem.at[s]).wait()

    def step(s):
        wait_buf(s)
        gather(jnp.minimum(i + 1, n_tiles - 1), 1 - s)
        gate = gate_ref[...]
        half = y_ref.shape[1]
        ffn_lo = jnp.zeros((tm, half), F32)
        ffn_hi = jnp.zeros((tm, half), F32)
        for kk in range(TOP_K):
            p = bufs[s][kk * tm:(kk + 1) * tm, :]
            gk = gate[:, kk:kk + 1]
            ffn_lo = ffn_lo + gk * lax.bitcast_convert_type(p << 16, F32)
            ffn_hi = ffn_hi + gk * lax.bitcast_convert_type(p & jnp.uint32(0xFFFF0000), F32)
        ffn = jnp.concatenate([ffn_lo, ffn_hi], axis=1)
        o_ref[...] = _layer_norm(alpha * h_ref[...] + ffn, g_ref[...], b_ref[...])

        @pl.when(i == n_tiles - 1)
        def _():
            wait_buf(1 - s)

    @pl.when(i == 0)
    def _():
        gather(0, 0)

    for s in range(2):
        @pl.when((i & 1) == s)
        def _():
            step(s)


def _combine(y, pos, gate, h, ln_g, ln_b, alpha):
    t, d = h.shape
    tm = _pick(t, (128, 64, 32, 16, 8))
    assert y.shape[1] * 2 == d
    return pl.pallas_call(
        functools.partial(_combine_kernel, alpha=alpha),
        grid_spec=pltpu.PrefetchScalarGridSpec(
            num_scalar_prefetch=1,
            grid=(t // tm,),
            in_specs=[pl.BlockSpec(memory_space=pl.ANY),
                      pl.BlockSpec((tm, LANES), lambda i, pos: (i, 0)),
                      pl.BlockSpec((tm, d), lambda i, pos: (i, 0)),
                      pl.BlockSpec((1, d), lambda i, pos: (0, 0)),
                      pl.BlockSpec((1, d), lambda i, pos: (0, 0))],
            out_specs=pl.BlockSpec((tm, d), lambda i, pos: (i, 0)),
            scratch_shapes=[pltpu.VMEM((TOP_K * tm, d // 2), U32), pltpu.VMEM((TOP_K * tm, d // 2), U32),
                            pltpu.SemaphoreType.DMA((2,))],
        ),
        out_shape=jax.ShapeDtypeStruct((t, d), F32),
        compiler_params=_params(("arbitrary",)),
        name="combine",
    )(pos, y, gate, h, ln_g.reshape(1, d), ln_b.reshape(1, d))


def _moe_plan(idx, rank, counts, *, tm, xmax, n_items):
    n_exp = counts.shape[0]
    padded = (counts + tm - 1) // tm * tm
    pstart = jnp.cumsum(padded) - padded
    onehot = idx[..., None] == jnp.arange(n_exp, dtype=idx.dtype)
    pos = (jnp.sum(jnp.where(onehot, pstart, 0), axis=-1) + rank).reshape(-1).astype(I32)
    per_e = (padded + xmax - 1) // xmax
    cum = jnp.cumsum(per_e)
    total = cum[-1]
    wi = jnp.arange(n_items, dtype=I32)
    valid = wi < total
    e_w = jnp.minimum(jnp.searchsorted(cum, jnp.minimum(wi, total - 1), side="right"), n_exp - 1).astype(I32)
    local = jnp.minimum(wi, total - 1) - (cum - per_e)[e_w]
    start = (pstart[e_w] + local * xmax).astype(I32)
    nblk = jnp.where(valid, jnp.clip(padded[e_w] - local * xmax, 0, xmax) // tm, 0).astype(I32)
    real = jnp.clip(counts[e_w] - local * xmax, 0, xmax)
    half_last = ((nblk > 0) & (real - (nblk - 1) * tm <= tm // 2)).astype(I32)
    start = jnp.concatenate([start, jnp.sum(padded, keepdims=True).astype(I32)])
    return pos, e_w, start, nblk, half_last


def _layer(h0, h0b, p, *, bsz, seq, alpha):
    t, d = h0.shape
    w_in = p["w_in"]
    hv = p["gdn_a_log"].shape[0]
    dk = p["gdn_norm_w"].shape[0]
    qkv_dim = p["gdn_conv_w"].shape[1]
    v_dim = hv * dk
    hq = (qkv_dim - v_dim) // (2 * dk)
    scw = p["sc_conv_w"].shape[1]
    assert 2 * hv <= LANES and hv % hq == 0
    n_a = qkv_dim + v_dim
    off_ba = n_a
    off_b = n_a + 2 * hv
    n_b = 3 * scw + 2 * d
    assert w_in.shape[1] == off_b + n_b

    w_in_t = w_in.T
    tm = _pick(t, (1024, 512, 256, 128))
    tn_a = _pick(n_a, (1024, 512, 256, 128))
    (proj_a,) = _matmul(h0b, w_in_t, col0=0, n_cols=n_a, tm=tm, tn=tn_a, out_dtypes=(BF16,), w_t=True)
    tn_b = _pick(n_b, (512, 256, 128))
    (proj_b,) = _matmul(h0b, w_in_t, col0=off_b, n_cols=n_b, tm=tm, tn=tn_b, out_dtypes=(BF16,), w_t=True)
    assert off_ba + LANES <= w_in.shape[1]
    (ba,) = _matmul(h0b, w_in_t, col0=off_ba, n_cols=LANES, tm=tm, tn=LANES, out_dtypes=(F32,), w_t=True)

    gcol, grow = _gdn_gates(ba, p["gdn_a_log"], p["gdn_dt_bias"], hv)
    o_n = _gdn(proj_a, p["gdn_conv_w"], gcol, grow, p["gdn_norm_w"], bsz=bsz, seq=seq, hq=hq, hv=hv, dk=dk)

    u = _short_conv(proj_b, p["sc_conv_w"], bsz=bsz, seq=seq, width=scw)

    tn = _pick(d, (512, 256, 128))
    ga0 = 3 * scw // tn
    gb0 = (3 * scw + d) // tn
    split = 2 if tm % 32 == 0 else 1
    (part_a,) = _matmul(
        o_n, p["w_out_gdn"], col0=0, n_cols=d, tm=tm, tn=tn, out_dtypes=(F32,), row_split=split,
        epilogue=lambda acc, ga: (_sigmoid(ga.astype(F32)) * acc,),
        extras=[(proj_b, (tm, tn), lambda j, i: (i, ga0 + j))])
    (merged,) = _matmul(
        u, p["w_out_sc"], col0=0, n_cols=d, tm=tm, tn=tn, out_dtypes=(BF16,), row_split=split,
        epilogue=lambda acc, gb, pa: (pa + _sigmoid(gb.astype(F32)) * acc,),
        extras=[(proj_b, (tm, tn), lambda j, i: (i, gb0 + j)),
                (part_a, (tm, tn), lambda j, i: (i, j))])
    tm3 = _pick(t, (512, 256, 128))

    def mix_epilogue(acc, hh, g, b):
        hn = _layer_norm(alpha * hh + acc, g, b)
        return hn, _pack_halves_bf16(hn)

    h1, h1p = _matmul(
        merged, p["w_out"].astype(BF16), col0=0, n_cols=d, tm=tm3, tn=d, out_dtypes=(F32, U32), out_div=(1, 2),
        epilogue=mix_epilogue, row_split=2,
        extras=[(h0, (tm3, d), lambda j, i: (i, 0)),
                (p["ln_mix_g"].reshape(1, d), (1, d), lambda j, i: (0, 0)),
                (p["ln_mix_b"].reshape(1, d), (1, d), lambda j, i: (0, 0))])

    n_exp = p["w_router"].shape[1]
    idx, gate, rank, cnt = _router(h1, p["w_router"], p["b_router"])
    mtm = MOE_TM
    xmax = MOE_XMAX
    n_rows = t * TOP_K + n_exp * mtm
    n_items = n_exp + n_rows // xmax
    pos, item_e, item_start, item_nblk, item_half = _moe_plan(
        idx[:, :TOP_K], rank[:, :TOP_K], cnt[0, :n_exp], tm=mtm, xmax=xmax, n_items=n_items)
    rows = _dispatch(h1p, pos, n_rows)
    y = _experts(rows, item_e, item_start, item_nblk, item_half, p["w_gate_up"], p["b_gate_up"], p["w_down"],
                 p["b_down"], tm=mtm, xmax=xmax)
    h2 = _combine(y, pos, gate, h1, p["ln_ffn_g"], p["ln_ffn_b"], alpha)
    return h2


_LAYER_PARAMS = ("w_in", "gdn_conv_w", "gdn_a_log", "gdn_dt_bias", "gdn_norm_w", "w_out_gdn", "sc_conv_w",
                 "w_out_sc", "w_out", "ln_mix_g", "ln_mix_b", "w_router", "b_router", "w_gate_up", "b_gate_up",
                 "w_down", "b_down", "ln_ffn_g", "ln_ffn_b")


def kernel(x, ln_in_g, ln_in_b, w_in, gdn_conv_w, gdn_a_log, gdn_dt_bias, gdn_norm_w, w_out_gdn, sc_conv_w,
           w_out_sc, w_out, ln_mix_g, ln_mix_b, w_router, b_router, w_gate_up, b_gate_up, w_down, b_down,
           ln_ffn_g, ln_ffn_b):
    stacked = dict(zip(_LAYER_PARAMS, (w_in, gdn_conv_w, gdn_a_log, gdn_dt_bias, gdn_norm_w, w_out_gdn,
                                       sc_conv_w, w_out_sc, w_out, ln_mix_g, ln_mix_b, w_router, b_router,
                                       w_gate_up, b_gate_up, w_down, b_down, ln_ffn_g, ln_ffn_b)))
    bsz, seq, d = x.shape
    depth = w_in.shape[0]
    alpha = (2 * depth) ** 0.25
    h, hb = _ln_in(x.reshape(bsz * seq, d), ln_in_g, ln_in_b)
    for l in range(depth):
        p = {name: arr[l] for name, arr in stacked.items()}
        h = _layer(h, hb, p, bsz=bsz, seq=seq, alpha=alpha)
        if l + 1 < depth:
            hb = h.astype(BF16)
    return h.reshape(bsz, seq, d)
```

```python
import functools

import jax
import jax.numpy as jnp
from jax import lax
from jax.experimental import pallas as pl
from jax.experimental.pallas import tpu as pltpu

F32 = jnp.float32
BF16 = jnp.bfloat16
I32 = jnp.int32
U32 = jnp.uint32

LANES = 128
SUBLANES = 8
MXU_N = 256
VMEM_LIMIT = 60 << 20

TOP_K = 4
SWIGLU_LIMIT = 7.0
SWIGLU_ALPHA = 1.702
LN_EPS = 1e-5
RMS_EPS = 1e-6
GDN_CHUNK = 64
GDN_STEP = 512
GDN_QK_GROUP = 4
GDN_INV_GROUP = 16
MOE_TM = 256
MOE_XMAX = 1280
MOE_TF = 512


def _params(sem):
    return pltpu.CompilerParams(dimension_semantics=sem, vmem_limit_bytes=VMEM_LIMIT)


def _pick(n, candidates):
    for c in candidates:
        if n % c == 0:
            return c
    raise ValueError(f"no tile for {n} in {candidates}")


def _sigmoid(x):
    return 0.5 * jnp.tanh(0.5 * x) + 0.5


def _layer_norm(xf, g, b):
    mu = jnp.mean(xf, axis=-1, keepdims=True)
    xc = xf - mu
    var = jnp.mean(xc * xc, axis=-1, keepdims=True)
    return xc * lax.rsqrt(var + LN_EPS) * g + b


def _pack_halves_bf16(x):
    half = x.shape[1] // 2
    lo = lax.bitcast_convert_type(x[:, :half].astype(BF16).astype(F32), U32) >> 16
    hi = lax.bitcast_convert_type(x[:, half:].astype(BF16).astype(F32), U32) & jnp.uint32(0xFFFF0000)
    return lo | hi


def _unpack_halves_bf16(p):
    lo = lax.bitcast_convert_type(p << 16, F32).astype(BF16)
    hi = lax.bitcast_convert_type(p & jnp.uint32(0xFFFF0000), F32).astype(BF16)
    return lo, hi


def _ln_in_kernel(x_ref, g_ref, b_ref, h_ref, hb_ref):
    h = _layer_norm(x_ref[...], g_ref[...], b_ref[...])
    h_ref[...] = h
    hb_ref[...] = h.astype(BF16)


def _ln_in(x2, g, b):
    t, d = x2.shape
    tm = _pick(t, (512, 256, 128, 64, 32, 16))
    return pl.pallas_call(
        _ln_in_kernel,
        grid=(t // tm,),
        in_specs=[pl.BlockSpec((tm, d), lambda i: (i, 0)),
                  pl.BlockSpec((1, d), lambda i: (0, 0)),
                  pl.BlockSpec((1, d), lambda i: (0, 0))],
        out_specs=[pl.BlockSpec((tm, d), lambda i: (i, 0)),
                   pl.BlockSpec((tm, d), lambda i: (i, 0))],
        out_shape=[jax.ShapeDtypeStruct((t, d), F32), jax.ShapeDtypeStruct((t, d), BF16)],
        compiler_params=_params(("arbitrary",)),
        name="ln_in",
    )(x2, g.reshape(1, d), b.reshape(1, d))


def _mm_kernel(*refs, shift, w_t, row_split, n_extra, n_out, epilogue):
    n_w = 2 if shift else 1
    x_ref, w_refs = refs[0], refs[1:1 + n_w]
    extras = refs[1 + n_w:1 + n_w + n_extra]
    o_refs = refs[1 + n_w + n_extra:1 + n_w + n_extra + n_out]
    wb_ref = refs[-1]
    n_axis = 0 if w_t else 1

    @pl.when(pl.program_id(1) == 0)
    def _():
        if shift:
            tn = w_refs[0].shape[n_axis]
            w = jnp.concatenate([lax.slice_in_dim(w_refs[0][...], shift, tn, axis=n_axis),
                                 lax.slice_in_dim(w_refs[1][...], 0, shift, axis=n_axis)], axis=n_axis)
        else:
            w = w_refs[0][...]
        wb_ref[...] = w.astype(BF16)

    tm = x_ref.shape[0]
    sub = tm // row_split
    for r in range(row_split):
        rows = slice(r * sub, (r + 1) * sub)
        acc = lax.dot_general(x_ref[rows, :], wb_ref[...], (((1,), (1 if w_t else 0,)), ((), ())),
                              preferred_element_type=F32)
        outs = epilogue(acc, *[e[rows, :] if e.shape[0] == tm else e[...] for e in extras])
        for o_ref, o in zip(o_refs, outs):
            o_ref[rows, :] = o.astype(o_ref.dtype)


def _matmul(x, w, *, col0, n_cols, tm, tn, out_dtypes, epilogue=None, extras=(), out_div=None, w_t=False,
            row_split=1):
    t, k = x.shape
    k_axis, n_axis = (1, 0) if w_t else (0, 1)
    assert w.shape[k_axis] == k and n_cols % tn == 0 and t % tm == 0 and col0 + n_cols <= w.shape[n_axis]
    j0, shift = divmod(col0, tn)
    by_element = w_t and shift != 0 and col0 % SUBLANES == 0
    if by_element:
        shift = 0
    if epilogue is None:
        epilogue = lambda acc: (acc,)
    if out_div is None:
        out_div = (1,) * len(out_dtypes)
    assert tm % (row_split * 16) == 0
    kern = functools.partial(_mm_kernel, shift=shift, w_t=w_t, row_split=row_split, n_extra=len(extras),
                             n_out=len(out_dtypes), epilogue=epilogue)
    w_block = (tn, k) if w_t else (k, tn)
    w_index = lambda jj: (jj, 0) if w_t else (0, jj)
    in_specs = [pl.BlockSpec((tm, k), lambda j, i: (i, 0)),
                pl.BlockSpec(w_block, lambda j, i: w_index(j + j0))]
    if by_element:
        in_specs[1] = pl.BlockSpec((pl.Element(tn), pl.Element(k)),
                                   lambda j, i: (pl.multiple_of(col0 + j * tn, SUBLANES), 0))
    ws = [w]
    if shift:
        in_specs.append(pl.BlockSpec(w_block, lambda j, i: w_index(j + j0 + 1)))
        ws.append(w)
    in_specs += [pl.BlockSpec(bs, im) for (_, bs, im) in extras]
    return pl.pallas_call(
        kern,
        grid=(n_cols // tn, t // tm),
        in_specs=in_specs,
        out_specs=[pl.BlockSpec((tm, tn // dv), lambda j, i: (i, j)) for dv in out_div],
        out_shape=[jax.ShapeDtypeStruct((t, n_cols // dv), dt) for dt, dv in zip(out_dtypes, out_div)],
        scratch_shapes=[pltpu.VMEM(w_block, BF16)],
        compiler_params=_params(("arbitrary", "arbitrary")),
        name="matmul",
    )(x, *ws, *[a for (a, _, _) in extras])


def _gates_kernel(ba_ref, alog_ref, dt_ref, col_ref, row_ref, *, hv):
    x = ba_ref[...]
    lane = lax.broadcasted_iota(I32, x.shape, 1)
    beta = _sigmoid(x)
    xs = x + dt_ref[...]
    softplus = jnp.maximum(xs, 0.0) + jnp.log(1.0 + jnp.exp(-jnp.abs(xs)))
    g = -jnp.exp(alog_ref[...]) * softplus
    g = jnp.where((lane >= hv) & (lane < 2 * hv), g, 0.0)
    n = x.shape[0]
    row = lax.broadcasted_iota(I32, x.shape, 0) & (GDN_CHUNK - 1)
    cum = g
    rev = g
    sh = 1
    while sh < GDN_CHUNK:
        cum = cum + jnp.where(row >= sh, pltpu.roll(cum, sh, 0), 0.0)
        rev = rev + jnp.where(row < GDN_CHUNK - sh, pltpu.roll(rev, n - sh, 0), 0.0)
        sh *= 2
    e_cum = pltpu.roll(jnp.exp(cum), hv, 1)
    e_rest = pltpu.roll(jnp.exp(rev - g), 2 * hv, 1)
    out = jnp.where(lane < hv, beta, jnp.where(lane < 2 * hv, cum, jnp.where(lane < 3 * hv, e_cum, e_rest)))
    col_ref[...] = out
    row_ref[...] = out.T


def _gdn_gates(ba, a_log, dt_bias, hv):
    assert 4 * hv <= LANES
    t = ba.shape[0]
    tm = _pick(t, (512, 256, 128))
    pad = lambda v: jnp.zeros((1, LANES), F32).at[0, hv:2 * hv].set(v.astype(F32))
    return pl.pallas_call(
        functools.partial(_gates_kernel, hv=hv),
        grid=(t // tm,),
        in_specs=[pl.BlockSpec((tm, LANES), lambda i: (i, 0)),
                  pl.BlockSpec((1, LANES), lambda i: (0, 0)),
                  pl.BlockSpec((1, LANES), lambda i: (0, 0))],
        out_specs=[pl.BlockSpec((tm, LANES), lambda i: (i, 0)),
                   pl.BlockSpec((LANES, tm), lambda i: (0, i))],
        out_shape=[jax.ShapeDtypeStruct((t, LANES), F32), jax.ShapeDtypeStruct((LANES, t), F32)],
        compiler_params=_params(("arbitrary",)),
        name="gdn_gates",
    )(ba, pad(a_log), pad(dt_bias))


def _causal_conv(x, w, ext_ref):
    n = x.shape[0]
    kw = w.shape[0]
    ext_ref[SUBLANES:, :] = x
    acc = x * w[kw - 1:kw]
    for d in range(1, kw):
        acc = acc + ext_ref[SUBLANES - d:SUBLANES - d + n, :] * w[kw - 1 - d:kw - d]
    ext_ref[:SUBLANES, :] = x[n - SUBLANES:]
    return acc


def _silu(x):
    h = 0.5 * x
    return h + h * jnp.tanh(h)


def _unit_lower_inverses(lows, ii, jj):
    c = lows[0].shape[0]
    eye = jnp.where(ii == jj, 1.0, 0.0)
    pair = (ii >> 1) == (jj >> 1)
    xbs = [(eye - jnp.where(pair, low, 0.0)).astype(BF16) for low in lows]
    lbs = [low.astype(BF16) for low in lows]
    zero = jnp.zeros((c, c), BF16)
    s = 1
    while (2 << s) <= c:
        m = ((ii >> (s + 1)) == (jj >> (s + 1))) & ((ii >> s) != (jj >> s))
        ts = [jnp.dot(jnp.where(m, lb, zero), xb, preferred_element_type=F32) for lb, xb in zip(lbs, xbs)]
        xbs = [jnp.dot(xb, (eye - t).astype(BF16), preferred_element_type=F32).astype(BF16)
               for xb, t in zip(xbs, ts)]
        s += 1
    return xbs


def _gdn_kernel(q_ref, k_ref, v_ref, z_ref, cwq_ref, cwk_ref, cwv_ref, gcol_ref, grow_ref, nw_ref,
                o_ref, s_ref, tq_ref, tk_ref, tv_ref, qn_ref, kn_ref, vn_ref, u_ref, wq_ref, attn_ref, kd_ref,
                *, nq, vper, dk):
    @pl.when(pl.program_id(2) == 0)
    def _():
        s_ref[...] = jnp.zeros_like(s_ref)
        for t_ref in (tq_ref, tk_ref, tv_ref):
            t_ref[:SUBLANES, :] = jnp.zeros((SUBLANES, t_ref.shape[1]), F32)

    tb = q_ref.shape[0]
    c = GDN_CHUNK
    nc = tb // c
    nv = nq * vper
    head = lambda h: slice(h * dk, (h + 1) * dk)

    q = _silu(_causal_conv(q_ref[...].astype(F32), cwq_ref[...], tq_ref))
    k = _silu(_causal_conv(k_ref[...].astype(F32), cwk_ref[...], tk_ref))
    for hq in range(nq):
        qh = q[:, head(hq)]
        kh = k[:, head(hq)]
        qn_ref[:, head(hq)] = qh * (lax.rsqrt(jnp.sum(qh * qh, axis=-1, keepdims=True) + RMS_EPS) * (dk ** -0.5))
        kn_ref[:, head(hq)] = kh * lax.rsqrt(jnp.sum(kh * kh, axis=-1, keepdims=True) + RMS_EPS)

    cols = gcol_ref[...]
    grows = grow_ref[...]
    nw = nw_ref[...]
    ii = lax.broadcasted_iota(I32, (c, c), 0)
    jj = lax.broadcasted_iota(I32, (c, c), 1)
    incl = ii >= jj
    strict = ii > jj

    def col(ci, which, h):
        return cols[ci * c:(ci + 1) * c, which * nv + h:which * nv + h + 1]

    def state_step(ci):
        rows = slice(ci * c, (ci + 1) * c)
        ps = [ci * nv + h for h in range(nv)]
        r1 = [jnp.dot(wq_ref[p], s_ref[h].astype(BF16), preferred_element_type=F32) for h, p in enumerate(ps)]
        v_new = [(u_ref[p] - r[:c]).astype(BF16) for p, r in zip(ps, r1)]
        outs = [r[c:] + jnp.dot(attn_ref[p], vn, preferred_element_type=F32) for p, r, vn in zip(ps, r1, v_new)]
        for h, (p, vn) in enumerate(zip(ps, v_new)):
            g_tot = cols[(ci + 1) * c - 1:(ci + 1) * c, 2 * nv + h:2 * nv + h + 1]
            s_ref[h] = s_ref[h] * g_tot + lax.dot_general(
                kd_ref[p], vn, (((0,), (0,)), ((), ())), preferred_element_type=F32)
        for h, o in enumerate(outs):
            zc = z_ref[rows, head(h)].astype(F32)
            var = jnp.mean(o * o, axis=-1, keepdims=True)
            o_ref[rows, head(h)] = (o * lax.rsqrt(var + RMS_EPS) * nw * _silu(zc)).astype(BF16)

    problems = [(ci, h) for ci in range(nc) for h in range(nv)]
    gram = {}
    chunks_done = 0
    for g0 in range(0, len(problems), GDN_INV_GROUP):
        group = problems[g0:g0 + GDN_INV_GROUP]
        lows = []
        for ci, h in group:
            hq = h // vper
            rows = slice(ci * c, (ci + 1) * c)
            if (ci, hq) not in gram:
                kcb = kn_ref[rows, head(hq)].astype(BF16)
                qcb = qn_ref[rows, head(hq)].astype(BF16)
                gram[(ci, hq)] = lax.dot_general(jnp.concatenate([kcb, qcb], axis=0), kcb,
                                                 (((1,), (1,)), ((), ())), preferred_element_type=F32)
            a = gram[(ci, hq)]
            grow = grows[h:h + 1, rows]
            decay = jnp.where(incl, jnp.exp(jnp.minimum(col(ci, 1, h) - grow, 0.0)), 0.0)
            lows.append(jnp.where(strict, col(ci, 0, h) * a[:c] * decay, 0.0))
            attn_ref[ci * nv + h] = jnp.where(incl, a[c:] * decay, 0.0).astype(BF16)
        if g0 == 0:
            vn_ref[...] = _silu(_causal_conv(v_ref[...].astype(F32), cwv_ref[...], tv_ref))
        xs = _unit_lower_inverses(lows, ii, jj)
        for x, (ci, h) in zip(xs, group):
            hq = h // vper
            rows = slice(ci * c, (ci + 1) * c)
            p = ci * nv + h
            kc = kn_ref[rows, head(hq)]
            bcol = col(ci, 0, h)
            eg = col(ci, 2, h)
            rhs = jnp.concatenate([vn_ref[rows, head(h)] * bcol, kc * (bcol * eg)], axis=1).astype(BF16)
            uw = jnp.dot(x, rhs, preferred_element_type=F32)
            u_ref[p] = uw[:, :dk]
            wq_ref[p, :c, :] = uw[:, dk:].astype(BF16)
            wq_ref[p, c:, :] = (qn_ref[rows, head(hq)] * eg).astype(BF16)
            kd_ref[p] = (kc * col(ci, 3, h)).astype(BF16)
        chunks_ready = (g0 + len(group)) // nv
        for ci in range(chunks_done, chunks_ready):
            state_step(ci)
        chunks_done = chunks_ready


def _gdn(proj_a, conv_w, gcol, grow, norm_w, *, bsz, seq, hq, hv, dk):
    t = bsz * seq
    vper = hv // hq
    nq = _pick(hq, (GDN_QK_GROUP, 2, 1))
    nv = nq * vper
    ng = hq // nq
    tb = _pick(seq, (GDN_STEP, 128, 64))
    ns = seq // tb
    nprob = (tb // GDN_CHUNK) * nv
    qw = nq * dk
    vw = nv * dk
    v_blk0 = 2 * hq * dk // vw
    z_blk0 = (2 * hq * dk + hv * dk) // vw
    kw = conv_w.shape[0]
    gcol = gcol[:, :4 * hv].reshape(t, 4, ng, nv).transpose(2, 0, 1, 3).reshape(ng, t, 4 * nv)
    grow = grow[hv:2 * hv].reshape(ng, nv, t)
    rowblk = lambda b, g, s: b * ns + s
    kern = functools.partial(_gdn_kernel, nq=nq, vper=vper, dk=dk)
    return pl.pallas_call(
        kern,
        grid=(bsz, ng, ns),
        in_specs=[
            pl.BlockSpec((tb, qw), lambda b, g, s: (rowblk(b, g, s), g)),
            pl.BlockSpec((tb, qw), lambda b, g, s: (rowblk(b, g, s), ng + g)),
            pl.BlockSpec((tb, vw), lambda b, g, s: (rowblk(b, g, s), v_blk0 + g)),
            pl.BlockSpec((tb, vw), lambda b, g, s: (rowblk(b, g, s), z_blk0 + g)),
            pl.BlockSpec((kw, qw), lambda b, g, s: (0, g)),
            pl.BlockSpec((kw, qw), lambda b, g, s: (0, ng + g)),
            pl.BlockSpec((kw, vw), lambda b, g, s: (0, v_blk0 + g)),
            pl.BlockSpec((None, tb, 4 * nv), lambda b, g, s: (g, rowblk(b, g, s), 0)),
            pl.BlockSpec((None, nv, tb), lambda b, g, s: (g, 0, rowblk(b, g, s))),
            pl.BlockSpec((1, dk), lambda b, g, s: (0, 0)),
        ],
        out_specs=pl.BlockSpec((tb, vw), lambda b, g, s: (rowblk(b, g, s), g)),
        out_shape=jax.ShapeDtypeStruct((t, hv * dk), BF16),
        scratch_shapes=[pltpu.VMEM((nv, dk, dk), F32),
                        pltpu.VMEM((tb + SUBLANES, qw), F32),
                        pltpu.VMEM((tb + SUBLANES, qw), F32),
                        pltpu.VMEM((tb + SUBLANES, vw), F32),
                        pltpu.VMEM((tb, qw), F32),
                        pltpu.VMEM((tb, qw), F32),
                        pltpu.VMEM((tb, vw), F32),
                        pltpu.VMEM((nprob, GDN_CHUNK, dk), F32),
                        pltpu.VMEM((nprob, 2 * GDN_CHUNK, dk), BF16),
                        pltpu.VMEM((nprob, GDN_CHUNK, GDN_CHUNK), BF16),
                        pltpu.VMEM((nprob, GDN_CHUNK, dk), BF16)],
        compiler_params=_params(("arbitrary", "arbitrary", "arbitrary")),
        name="gdn",
    )(proj_a, proj_a, proj_a, proj_a, conv_w, conv_w, conv_w, gcol, grow, norm_w.reshape(1, dk))


def _sc_kernel(b_ref, c_ref, x_ref, w_ref, u_ref, tail_ref):
    @pl.when(pl.program_id(1) == 0)
    def _():
        tail_ref[:SUBLANES, :] = jnp.zeros((SUBLANES, tail_ref.shape[1]), F32)

    p = c_ref[...].astype(F32) * x_ref[...].astype(F32)
    u_ref[...] = (b_ref[...].astype(F32) * _causal_conv(p, w_ref[...], tail_ref)).astype(BF16)


def _short_conv(proj_b, conv_w, *, bsz, seq, width):
    t = bsz * seq
    ts = _pick(seq, (512, 256, 128, 64))
    ns = seq // ts
    kw = conv_w.shape[0]
    return pl.pallas_call(
        _sc_kernel,
        grid=(bsz, ns),
        in_specs=[pl.BlockSpec((ts, width), lambda b, s: (b * ns + s, 0)),
                  pl.BlockSpec((ts, width), lambda b, s: (b * ns + s, 1)),
                  pl.BlockSpec((ts, width), lambda b, s: (b * ns + s, 2)),
                  pl.BlockSpec((kw, width), lambda b, s: (0, 0))],
        out_specs=pl.BlockSpec((ts, width), lambda b, s: (b * ns + s, 0)),
        out_shape=jax.ShapeDtypeStruct((t, width), BF16),
        scratch_shapes=[pltpu.VMEM((ts + SUBLANES, width), F32)],
        compiler_params=_params(("arbitrary", "arbitrary")),
        name="short_conv",
    )(proj_b, proj_b, proj_b, conv_w)


def _router_kernel(h_ref, w_ref, b_ref, idx_ref, gate_ref, rank_ref, cnt_ref, run_ref):
    @pl.when(pl.program_id(0) == 0)
    def _():
        run_ref[...] = jnp.zeros_like(run_ref)

    tm = h_ref.shape[0]
    h = h_ref[...]
    h_hi = h.astype(BF16)
    h_lo = (h - h_hi.astype(F32)).astype(BF16)
    w2 = w_ref[...]
    first = jnp.dot(h_hi, w2, preferred_element_type=F32)
    logits = (first[:, :LANES] + first[:, LANES:]
              + jnp.dot(h_lo, w2[:, :LANES], preferred_element_type=F32) + b_ref[...])
    lane = lax.broadcasted_iota(I32, logits.shape, 1)
    lane_f = lane.astype(F32)
    cur = logits
    idxs, vals = [], []
    for _ in range(TOP_K):
        m = jnp.max(cur, axis=-1, keepdims=True)
        ix = jnp.min(jnp.where(cur == m, lane_f, float(LANES)), axis=-1, keepdims=True).astype(I32)
        idxs.append(ix)
        vals.append(m)
        cur = jnp.where(lane == ix, -jnp.inf, cur)
    es = [jnp.exp(val - vals[0]) for val in vals]
    den = es[0]
    for e in es[1:]:
        den = den + e
    onehot = jnp.zeros(logits.shape, F32)
    for ix in idxs:
        onehot = onehot + jnp.where(lane == ix, 1.0, 0.0)
    ri = lax.broadcasted_iota(I32, (tm, tm), 0)
    ci = lax.broadcasted_iota(I32, (tm, tm), 1)
    tri = jnp.where(ri > ci, 1.0, 0.0).astype(BF16)
    before = jnp.dot(tri, onehot.astype(BF16), preferred_element_type=F32) + run_ref[...]
    idx_out = jnp.zeros(logits.shape, I32)
    rank_out = jnp.zeros(logits.shape, I32)
    gate_out = jnp.zeros(logits.shape, F32)
    for kk in range(TOP_K):
        rank = jnp.sum(jnp.where(lane == idxs[kk], before, 0.0), axis=-1, keepdims=True).astype(I32)
        idx_out = jnp.where(lane == kk, idxs[kk], idx_out)
        rank_out = jnp.where(lane == kk, rank, rank_out)
        gate_out = jnp.where(lane == kk, es[kk] / den, gate_out)
    idx_ref[...] = idx_out
    rank_ref[...] = rank_out
    gate_ref[...] = gate_out
    run_ref[...] = run_ref[...] + jnp.sum(onehot, axis=0, keepdims=True)
    cnt_ref[...] = run_ref[...].astype(I32)


def _router(h, w_router, b_router):
    t, d = h.shape
    e = w_router.shape[1]
    tm = _pick(t, (256, 128, 64, 32, 16, 8))
    wp = jnp.zeros((d, LANES), F32).at[:, :e].set(w_router)
    w_hi = wp.astype(BF16)
    w_lo = (wp - w_hi.astype(F32)).astype(BF16)
    w2 = jnp.concatenate([w_hi, w_lo], axis=1)
    bp = jnp.full((1, LANES), -1e30, F32).at[0, :e].set(b_router)
    tile = pl.BlockSpec((tm, LANES), lambda i: (i, 0))
    return pl.pallas_call(
        _router_kernel,
        grid=(t // tm,),
        in_specs=[pl.BlockSpec((tm, d), lambda i: (i, 0)),
                  pl.BlockSpec((d, 2 * LANES), lambda i: (0, 0)),
                  pl.BlockSpec((1, LANES), lambda i: (0, 0))],
        out_specs=[tile, tile, tile, pl.BlockSpec((1, LANES), lambda i: (0, 0))],
        out_shape=[jax.ShapeDtypeStruct((t, LANES), I32), jax.ShapeDtypeStruct((t, LANES), F32),
                   jax.ShapeDtypeStruct((t, LANES), I32), jax.ShapeDtypeStruct((1, LANES), I32)],
        scratch_shapes=[pltpu.VMEM((1, LANES), F32)],
        compiler_params=_params(("arbitrary",)),
        name="router",
    )(h, w2, bp)


def _row_copy(src, s_row, dst, d_row, sem):
    return pltpu.make_async_copy(src.at[pl.ds(s_row, 1)], dst.at[pl.ds(d_row, 1)], sem)


def _dispatch_kernel(pos_ref, x_ref, rows_in_ref, rows_ref, sem):
    del rows_in_ref
    tm = x_ref.shape[0]
    base = pl.program_id(0) * (tm * TOP_K)

    for t in range(tm):
        for kk in range(TOP_K):
            _row_copy(x_ref, t, rows_ref, pos_ref[base + (t * TOP_K + kk)], sem).start(priority=kk & 1)
    n = tm * TOP_K
    pltpu.make_async_copy(rows_ref.at[pl.ds(0, n)], rows_ref.at[pl.ds(0, n)], sem).wait()


def _dispatch(h, pos, n_rows):
    t, d = h.shape
    tm = _pick(t, (128, 64, 32, 16, 8))
    rows0 = jnp.zeros((n_rows, d), h.dtype)
    return pl.pallas_call(
        _dispatch_kernel,
        grid_spec=pltpu.PrefetchScalarGridSpec(
            num_scalar_prefetch=1,
            grid=(t // tm,),
            in_specs=[pl.BlockSpec((tm, d), lambda i, pos: (i, 0)),
                      pl.BlockSpec(memory_space=pl.ANY)],
            out_specs=pl.BlockSpec(memory_space=pl.ANY),
            scratch_shapes=[pltpu.SemaphoreType.DMA(())],
        ),
        out_shape=jax.ShapeDtypeStruct((n_rows, d), h.dtype),
        input_output_aliases={2: 0},
        compiler_params=_params(("arbitrary",)),
        name="dispatch",
    )(pos, h, rows0)


def _swiglu_interleaved(gu):
    lane = lax.broadcasted_iota(I32, (gu.shape[0], LANES), 1)
    even = (lane & 1) == 0
    parts = []
    for c in range(gu.shape[1] // LANES):
        g = gu[:, c * LANES:(c + 1) * LANES]
        gt = jnp.minimum(g, SWIGLU_LIMIT)
        glu = gt * _sigmoid(SWIGLU_ALPHA * gt)
        up = jnp.clip(g, -SWIGLU_LIMIT, SWIGLU_LIMIT) + 1.0
        parts.append(jnp.where(even, pltpu.roll(up, LANES - 1, 1) * glu, 0.0))
    outs = [parts[2 * m] + pltpu.roll(parts[2 * m + 1], 1, 1) for m in range(len(parts) // 2)]
    return jnp.concatenate(outs, axis=1) if len(outs) > 1 else outs[0]


def _interleave_rows_bf16(w_ref, out_ref):
    half = LANES // 2
    for m in range(w_ref.shape[0] // LANES):
        lo = w_ref[m * LANES:m * LANES + half, :].astype(BF16).astype(F32)
        hi = w_ref[m * LANES + half:(m + 1) * LANES, :].astype(BF16).astype(F32)
        packed = (lax.bitcast_convert_type(lo, jnp.uint32) >> 16) | (
            lax.bitcast_convert_type(hi, jnp.uint32) & jnp.uint32(0xFFFF0000))
        out_ref[m * LANES:(m + 1) * LANES, :] = pltpu.bitcast(packed, BF16)


def _expert_kernel(ie_ref, ist_ref, inb_ref, ihalf_ref, rows_ref, wgu_ref, bgu_ref, wd_ref, bd_ref, y_ref,
                   xb_ref, acc_ref, stage_ref, wgub_ref, wdb_ref, ost_ref, sem_in, sem_out, *, tm, n_j):
    del ie_ref
    wi = pl.program_id(0)
    j = pl.program_id(1)
    n_items = pl.num_programs(0)
    nblk = inb_ref[wi]
    half_last = ihalf_ref[wi]
    nfull = nblk - half_last
    half = stage_ref.shape[1]

    def block(b):
        return pl.ds(pl.multiple_of(b * tm, tm), tm)

    def hbm_block(item, b):
        return pl.ds(pl.multiple_of(ist_ref[item] + b * tm, tm), tm)

    def rows_copy(item, b):
        return pltpu.make_async_copy(rows_ref.at[hbm_block(item, b)], stage_ref.at[block(b)], sem_in)

    def y_copy(item, b):
        slot = b & 1
        return pltpu.make_async_copy(ost_ref.at[slot], y_ref.at[hbm_block(item, b)], sem_out.at[slot])

    def drain_stores(item):
        n = inb_ref[item]
        for back in (2, 1):
            @pl.when(n >= back)
            def _():
                y_copy(item, n - back).wait()

    def for_blocks(item, fn):
        def body(b, carry):
            fn(item, b)
            return carry
        lax.fori_loop(0, inb_ref[item], body, 0)

    def unpack(item, b):
        lo, hi = _unpack_halves_bf16(stage_ref[block(b), :])
        xb_ref[block(b), :half] = lo
        xb_ref[block(b), half:] = hi

    @pl.when(j == 0)
    def _():
        @pl.when(wi == 0)
        def _():
            for_blocks(0, lambda it, b: rows_copy(it, b).start())
            acc_ref[...] = jnp.zeros_like(acc_ref)
            ost_ref[...] = jnp.zeros_like(ost_ref)

        @pl.when(wi > 0)
        def _():
            drain_stores(wi - 1)

        for_blocks(wi, lambda it, b: rows_copy(it, b).wait())
        for_blocks(wi, unpack)

        @pl.when(wi + 1 < n_items)
        def _():
            for_blocks(wi + 1, lambda it, b: rows_copy(it, b).start())

    @pl.when(nblk > 0)
    def _():
        bias = bgu_ref[...]
        n_ct, _, ct = wgub_ref.shape

        def convert_weights():
            for c in range(n_ct):
                wgub_ref[c] = wgu_ref[:, c * ct:(c + 1) * ct].astype(BF16)
            _interleave_rows_bf16(wd_ref, wdb_ref)

        def mlp(b, m):
            x = xb_ref[pl.ds(pl.multiple_of(b * tm, tm), m), :]
            gu = jnp.concatenate([jnp.dot(x, wgub_ref[c], preferred_element_type=F32) for c in range(n_ct)],
                                 axis=1) + bias
            act = _swiglu_interleaved(gu).astype(BF16)
            return jnp.dot(act, wdb_ref[...], preferred_element_type=F32)

        def accumulate(b, m=tm):
            rows = pl.ds(pl.multiple_of(b * tm, tm), m)
            prev = jnp.where(j == 0, jnp.broadcast_to(bd_ref[...], (m, acc_ref.shape[1])), acc_ref[rows, :])
            acc_ref[rows, :] = prev + mlp(b, m)

        def finish(b, m=tm):
            prev = acc_ref[pl.ds(pl.multiple_of(b * tm, tm), m), :] if n_j > 1 else bd_ref[...]
            ost_ref[b & 1, 0:m, :] = _pack_halves_bf16(prev + mlp(b, m))
            y_copy(wi, b).start()

        def run(fn, stores):
            @pl.when(nfull >= 2)
            def _():
                convert_weights()
                fn(0)
                fn(1)

            @pl.when(nfull == 1)
            def _():
                convert_weights()
                fn(0)

            def pair(i, carry):
                if stores:
                    y_copy(wi, 2 * i - 2).wait()
                    y_copy(wi, 2 * i - 1).wait()
                fn(2 * i)
                fn(2 * i + 1)
                return carry
            lax.fori_loop(1, nfull // 2, pair, 0)

            @pl.when(((nfull & 1) == 1) & (nfull >= 3))
            def _():
                if stores:
                    y_copy(wi, nfull - 3).wait()
                fn(nfull - 1)

            @pl.when((half_last == 1) & (nfull == 0))
            def _():
                convert_weights()
                fn(0, tm // 2)

            @pl.when((half_last == 1) & (nfull > 0))
            def _():
                if stores:
                    @pl.when(nfull >= 2)
                    def _():
                        y_copy(wi, nfull - 2).wait()
                fn(nfull, tm // 2)

        if n_j > 1:
            @pl.when(j < n_j - 1)
            def _():
                run(accumulate, False)

        @pl.when(j == n_j - 1)
        def _():
            run(finish, True)

    @pl.when((wi == n_items - 1) & (j == n_j - 1))
    def _():
        drain_stores(wi)
        ost_ref[0] = jnp.zeros(ost_ref.shape[1:], U32)

        def fill(b, carry):
            cp = pltpu.make_async_copy(ost_ref.at[0], y_ref.at[pl.ds(pl.multiple_of(b * tm, tm), tm)],
                                       sem_out.at[0])
            cp.start()
            cp.wait()
            return carry
        lax.fori_loop(ist_ref[n_items] // tm, y_ref.shape[0] // tm, fill, 0)


def _experts(rows, item_e, item_start, item_nblk, item_half, w_gate_up, b_gate_up, w_down, b_down, *, tm, xmax):
    n_rows = rows.shape[0]
    n_exp, d, f2 = w_gate_up.shape
    assert rows.shape[1] * 2 == d
    f = f2 // 2
    tf = _pick(f, (MOE_TF, LANES))
    n_j = f // tf
    ct = min(2 * tf, MXU_N)
    n_items = item_e.shape[0]

    def jeff(j, inb, wi):
        return jnp.where(inb[wi] > 0, j, n_j - 1)

    kern = functools.partial(_expert_kernel, tm=tm, n_j=n_j)
    return pl.pallas_call(
        kern,
        grid_spec=pltpu.PrefetchScalarGridSpec(
            num_scalar_prefetch=4,
            grid=(n_items, n_j),
            in_specs=[
                pl.BlockSpec(memory_space=pl.ANY),
                pl.BlockSpec((None, d, 2 * tf), lambda wi, j, ie, ist, inb, ih: (ie[wi], 0, jeff(j, inb, wi))),
                pl.BlockSpec((None, 1, 2 * tf), lambda wi, j, ie, ist, inb, ih: (ie[wi], 0, jeff(j, inb, wi))),
                pl.BlockSpec((None, tf, d), lambda wi, j, ie, ist, inb, ih: (ie[wi], jeff(j, inb, wi), 0)),
                pl.BlockSpec((None, 1, d), lambda wi, j, ie, ist, inb, ih: (ie[wi], 0, 0)),
            ],
            out_specs=pl.BlockSpec(memory_space=pl.ANY),
            scratch_shapes=[pltpu.VMEM((xmax, d), BF16),
                            pltpu.VMEM((xmax, d), F32),
                            pltpu.VMEM((xmax, d // 2), U32),
                            pltpu.VMEM((2 * tf // ct, d, ct), BF16),
                            pltpu.VMEM((tf, d), BF16),
                            pltpu.VMEM((2, tm, d // 2), U32),
                            pltpu.SemaphoreType.DMA(()),
                            pltpu.SemaphoreType.DMA((2,))],
        ),
        out_shape=jax.ShapeDtypeStruct((n_rows, d // 2), U32),
        compiler_params=_params(("arbitrary", "arbitrary")),
        name="experts",
    )(item_e, item_start, item_nblk, item_half, rows, w_gate_up, b_gate_up.reshape(n_exp, 1, f2),
      w_down, b_down.reshape(n_exp, 1, d))


def _combine_kernel(pos_ref, y_ref, gate_ref, h_ref, g_ref, b_ref, o_ref, ybuf0_ref, ybuf1_ref, sem,
                    *, alpha):
    tm = h_ref.shape[0]
    i = pl.program_id(0)
    n_tiles = pl.num_programs(0)
    n = tm * TOP_K
    bufs = (ybuf0_ref, ybuf1_ref)

    def gather(tile, s):
        base = tile * n
        for t in range(tm):
            for kk in range(TOP_K):
                _row_copy(y_ref, pos_ref[base + (t * TOP_K + kk)], bufs[s], kk * tm + t,
                          sem.at[s]).start(priority=kk & 1)

    def wait_buf(s):
        pltpu.make_async_copy(y_ref.at[pl.ds(0, n)], bufs[s], sem.at[s]).wait()

    def step(s):
        wait_buf(s)
        gather(jnp.minimum(i + 1, n_tiles - 1), 1 - s)
        gate = gate_ref[...]
        half = y_ref.shape[1]
        ffn_lo = jnp.zeros((tm, half), F32)
        ffn_hi = jnp.zeros((tm, half), F32)
        for kk in range(TOP_K):
            p = bufs[s][kk * tm:(kk + 1) * tm, :]
            gk = gate[:, kk:kk + 1]
            ffn_lo = ffn_lo + gk * lax.bitcast_convert_type(p << 16, F32)
            ffn_hi = ffn_hi + gk * lax.bitcast_convert_type(p & jnp.uint32(0xFFFF0000), F32)
        ffn = jnp.concatenate([ffn_lo, ffn_hi], axis=1)
        o_ref[...] = _layer_norm(alpha * h_ref[...] + ffn, g_ref[...], b_ref[...])

        @pl.when(i == n_tiles - 1)
        def _():
            wait_buf(1 - s)

    @pl.when(i == 0)
    def _():
        gather(0, 0)

    for s in range(2):
        @pl.when((i & 1) == s)
        def _():
            step(s)


def _combine(y, pos, gate, h, ln_g, ln_b, alpha):
    t, d = h.shape
    tm = _pick(t, (128, 64, 32, 16, 8))
    assert y.shape[1] * 2 == d
    return pl.pallas_call(
        functools.partial(_combine_kernel, alpha=alpha),
        grid_spec=pltpu.PrefetchScalarGridSpec(
            num_scalar_prefetch=1,
            grid=(t // tm,),
            in_specs=[pl.BlockSpec(memory_space=pl.ANY),
                      pl.BlockSpec((tm, LANES), lambda i, pos: (i, 0)),
                      pl.BlockSpec((tm, d), lambda i, pos: (i, 0)),
                      pl.BlockSpec((1, d), lambda i, pos: (0, 0)),
                      pl.BlockSpec((1, d), lambda i, pos: (0, 0))],
            out_specs=pl.BlockSpec((tm, d), lambda i, pos: (i, 0)),
            scratch_shapes=[pltpu.VMEM((TOP_K * tm, d // 2), U32), pltpu.VMEM((TOP_K * tm, d // 2), U32),
                            pltpu.SemaphoreType.DMA((2,))],
        ),
        out_shape=jax.ShapeDtypeStruct((t, d), F32),
        compiler_params=_params(("arbitrary",)),
        name="combine",
    )(pos, y, gate, h, ln_g.reshape(1, d), ln_b.reshape(1, d))


def _moe_plan(idx, rank, counts, *, tm, xmax, n_items):
    n_exp = counts.shape[0]
    padded = (counts + tm - 1) // tm * tm
    pstart = jnp.cumsum(padded) - padded
    onehot = idx[..., None] == jnp.arange(n_exp, dtype=idx.dtype)
    pos = (jnp.sum(jnp.where(onehot, pstart, 0), axis=-1) + rank).reshape(-1).astype(I32)
    per_e = (padded + xmax - 1) // xmax
    cum = jnp.cumsum(per_e)
    total = cum[-1]
    wi = jnp.arange(n_items, dtype=I32)
    valid = wi < total
    e_w = jnp.minimum(jnp.searchsorted(cum, jnp.minimum(wi, total - 1), side="right"), n_exp - 1).astype(I32)
    local = jnp.minimum(wi, total - 1) - (cum - per_e)[e_w]
    start = (pstart[e_w] + local * xmax).astype(I32)
    nblk = jnp.where(valid, jnp.clip(padded[e_w] - local * xmax, 0, xmax) // tm, 0).astype(I32)
    real = jnp.clip(counts[e_w] - local * xmax, 0, xmax)
    half_last = ((nblk > 0) & (real - (nblk - 1) * tm <= tm // 2)).astype(I32)
    start = jnp.concatenate([start, jnp.sum(padded, keepdims=True).astype(I32)])
    return pos, e_w, start, nblk, half_last


def _layer(h0, h0b, p, *, bsz, seq, alpha):
    t, d = h0.shape
    w_in = p["w_in"]
    hv = p["gdn_a_log"].shape[0]
    dk = p["gdn_norm_w"].shape[0]
    qkv_dim = p["gdn_conv_w"].shape[1]
    v_dim = hv * dk
    hq = (qkv_dim - v_dim) // (2 * dk)
    scw = p["sc_conv_w"].shape[1]
    assert 2 * hv <= LANES and hv % hq == 0
    n_a = qkv_dim + v_dim
    off_ba = n_a
    off_b = n_a + 2 * hv
    n_b = 3 * scw + 2 * d
    assert w_in.shape[1] == off_b + n_b

    w_in_t = w_in.T
    tm = _pick(t, (1024, 512, 256, 128))
    tn_a = _pick(n_a, (1024, 512, 256, 128))
    (proj_a,) = _matmul(h0b, w_in_t, col0=0, n_cols=n_a, tm=tm, tn=tn_a, out_dtypes=(BF16,), w_t=True)
    tn_b = _pick(n_b, (1024, 512, 256, 128))
    (proj_b,) = _matmul(h0b, w_in_t, col0=off_b, n_cols=n_b, tm=tm, tn=tn_b, out_dtypes=(BF16,), w_t=True)
    assert off_ba + LANES <= w_in.shape[1]
    (ba,) = _matmul(h0b, w_in_t, col0=off_ba, n_cols=LANES, tm=tm, tn=LANES, out_dtypes=(F32,), w_t=True)

    gcol, grow = _gdn_gates(ba, p["gdn_a_log"], p["gdn_dt_bias"], hv)
    o_n = _gdn(proj_a, p["gdn_conv_w"], gcol, grow, p["gdn_norm_w"], bsz=bsz, seq=seq, hq=hq, hv=hv, dk=dk)

    u = _short_conv(proj_b, p["sc_conv_w"], bsz=bsz, seq=seq, width=scw)

    tn = _pick(d, (512, 256, 128))
    ga0 = 3 * scw // tn
    gb0 = (3 * scw + d) // tn
    split = 2 if tm % 32 == 0 else 1
    (part_a,) = _matmul(
        o_n, p["w_out_gdn"], col0=0, n_cols=d, tm=tm, tn=tn, out_dtypes=(F32,), row_split=split,
        epilogue=lambda acc, ga: (_sigmoid(ga.astype(F32)) * acc,),
        extras=[(proj_b, (tm, tn), lambda j, i: (i, ga0 + j))])
    (merged,) = _matmul(
        u, p["w_out_sc"], col0=0, n_cols=d, tm=tm, tn=tn, out_dtypes=(BF16,), row_split=split,
        epilogue=lambda acc, gb, pa: (pa + _sigmoid(gb.astype(F32)) * acc,),
        extras=[(proj_b, (tm, tn), lambda j, i: (i, gb0 + j)),
                (part_a, (tm, tn), lambda j, i: (i, j))])
    tm3 = _pick(t, (512, 256, 128))

    def mix_epilogue(acc, hh, g, b):
        hn = _layer_norm(alpha * hh + acc, g, b)
        return hn, _pack_halves_bf16(hn)

    h1, h1p = _matmul(
        merged, p["w_out"].astype(BF16), col0=0, n_cols=d, tm=tm3, tn=d, out_dtypes=(F32, U32), out_div=(1, 2),
        epilogue=mix_epilogue, row_split=2,
        extras=[(h0, (tm3, d), lambda j, i: (i, 0)),
                (p["ln_mix_g"].reshape(1, d), (1, d), lambda j, i: (0, 0)),
                (p["ln_mix_b"].reshape(1, d), (1, d), lambda j, i: (0, 0))])

    n_exp = p["w_router"].shape[1]
    idx, gate, rank, cnt = _router(h1, p["w_router"], p["b_router"])
    mtm = MOE_TM
    xmax = MOE_XMAX
    n_rows = t * TOP_K + n_exp * mtm
    n_items = n_exp + n_rows // xmax
    pos, item_e, item_start, item_nblk, item_half = _moe_plan(
        idx[:, :TOP_K], rank[:, :TOP_K], cnt[0, :n_exp], tm=mtm, xmax=xmax, n_items=n_items)
    rows = _dispatch(h1p, pos, n_rows)
    y = _experts(rows, item_e, item_start, item_nblk, item_half, p["w_gate_up"], p["b_gate_up"], p["w_down"],
                 p["b_down"], tm=mtm, xmax=xmax)
    h2 = _combine(y, pos, gate, h1, p["ln_ffn_g"], p["ln_ffn_b"], alpha)
    return h2


_LAYER_PARAMS = ("w_in", "gdn_conv_w", "gdn_a_log", "gdn_dt_bias", "gdn_norm_w", "w_out_gdn", "sc_conv_w",
                 "w_out_sc", "w_out", "ln_mix_g", "ln_mix_b", "w_router", "b_router", "w_gate_up", "b_gate_up",
                 "w_down", "b_down", "ln_ffn_g", "ln_ffn_b")


def kernel(x, ln_in_g, ln_in_b, w_in, gdn_conv_w, gdn_a_log, gdn_dt_bias, gdn_norm_w, w_out_gdn, sc_conv_w,
           w_out_sc, w_out, ln_mix_g, ln_mix_b, w_router, b_router, w_gate_up, b_gate_up, w_down, b_down,
           ln_ffn_g, ln_ffn_b):
    stacked = dict(zip(_LAYER_PARAMS, (w_in, gdn_conv_w, gdn_a_log, gdn_dt_bias, gdn_norm_w, w_out_gdn,
                                       sc_conv_w, w_out_sc, w_out, ln_mix_g, ln_mix_b, w_router, b_router,
                                       w_gate_up, b_gate_up, w_down, b_down, ln_ffn_g, ln_ffn_b)))
    bsz, seq, d = x.shape
    depth = w_in.shape[0]
    alpha = (2 * depth) ** 0.25
    h, hb = _ln_in(x.reshape(bsz * seq, d), ln_in_g, ln_in_b)
    for l in range(depth):
        p = {name: arr[l] for name, arr in stacked.items()}
        h = _layer(h, hb, p, bsz=bsz, seq=seq, alpha=alpha)
        if l + 1 < depth:
            hb = h.astype(BF16)
    return h.reshape(bsz, seq, d)
```

```python
import functools

import jax
import jax.numpy as jnp
from jax import lax
from jax.experimental import pallas as pl
from jax.experimental.pallas import tpu as pltpu

F32 = jnp.float32
BF16 = jnp.bfloat16
I32 = jnp.int32
U32 = jnp.uint32

LANES = 128
SUBLANES = 8
MXU_N = 256
VMEM_LIMIT = 60 << 20

TOP_K = 4
SWIGLU_LIMIT = 7.0
SWIGLU_ALPHA = 1.702
LN_EPS = 1e-5
RMS_EPS = 1e-6
GDN_CHUNK = 64
GDN_STEP = 512
GDN_QK_GROUP = 4
GDN_INV_GROUP = 16
MOE_TM = 256
MOE_XMAX = 1280
MOE_TF = 512


def _params(sem):
    return pltpu.CompilerParams(dimension_semantics=sem, vmem_limit_bytes=VMEM_LIMIT)


def _pick(n, candidates):
    for c in candidates:
        if n % c == 0:
            return c
    raise ValueError(f"no tile for {n} in {candidates}")


def _sigmoid(x):
    return 0.5 * jnp.tanh(0.5 * x) + 0.5


def _layer_norm(xf, g, b):
    mu = jnp.mean(xf, axis=-1, keepdims=True)
    xc = xf - mu
    var = jnp.mean(xc * xc, axis=-1, keepdims=True)
    return xc * lax.rsqrt(var + LN_EPS) * g + b


def _pack_halves_bf16(x):
    half = x.shape[1] // 2
    lo = lax.bitcast_convert_type(x[:, :half].astype(BF16).astype(F32), U32) >> 16
    hi = lax.bitcast_convert_type(x[:, half:].astype(BF16).astype(F32), U32) & jnp.uint32(0xFFFF0000)
    return lo | hi


def _unpack_halves_bf16(p):
    lo = lax.bitcast_convert_type(p << 16, F32).astype(BF16)
    hi = lax.bitcast_convert_type(p & jnp.uint32(0xFFFF0000), F32).astype(BF16)
    return lo, hi


def _ln_in_kernel(x_ref, g_ref, b_ref, h_ref, hb_ref):
    h = _layer_norm(x_ref[...], g_ref[...], b_ref[...])
    h_ref[...] = h
    hb_ref[...] = h.astype(BF16)


def _ln_in(x2, g, b):
    t, d = x2.shape
    tm = _pick(t, (512, 256, 128, 64, 32, 16))
    return pl.pallas_call(
        _ln_in_kernel,
        grid=(t // tm,),
        in_specs=[pl.BlockSpec((tm, d), lambda i: (i, 0)),
                  pl.BlockSpec((1, d), lambda i: (0, 0)),
                  pl.BlockSpec((1, d), lambda i: (0, 0))],
        out_specs=[pl.BlockSpec((tm, d), lambda i: (i, 0)),
                   pl.BlockSpec((tm, d), lambda i: (i, 0))],
        out_shape=[jax.ShapeDtypeStruct((t, d), F32), jax.ShapeDtypeStruct((t, d), BF16)],
        compiler_params=_params(("arbitrary",)),
        name="ln_in",
    )(x2, g.reshape(1, d), b.reshape(1, d))


def _mm_kernel(*refs, shift, w_t, row_split, n_extra, n_out, epilogue):
    n_w = 2 if shift else 1
    x_ref, w_refs = refs[0], refs[1:1 + n_w]
    extras = refs[1 + n_w:1 + n_w + n_extra]
    o_refs = refs[1 + n_w + n_extra:1 + n_w + n_extra + n_out]
    wb_ref = refs[-1]
    n_axis = 0 if w_t else 1

    @pl.when(pl.program_id(1) == 0)
    def _():
        if shift:
            tn = w_refs[0].shape[n_axis]
            w = jnp.concatenate([lax.slice_in_dim(w_refs[0][...], shift, tn, axis=n_axis),
                                 lax.slice_in_dim(w_refs[1][...], 0, shift, axis=n_axis)], axis=n_axis)
        else:
            w = w_refs[0][...]
        wb_ref[...] = w.astype(BF16)

    tm = x_ref.shape[0]
    sub = tm // row_split
    for r in range(row_split):
        rows = slice(r * sub, (r + 1) * sub)
        acc = lax.dot_general(x_ref[rows, :], wb_ref[...], (((1,), (1 if w_t else 0,)), ((), ())),
                              preferred_element_type=F32)
        outs = epilogue(acc, *[e[rows, :] if e.shape[0] == tm else e[...] for e in extras])
        for o_ref, o in zip(o_refs, outs):
            o_ref[rows, :] = o.astype(o_ref.dtype)


def _matmul(x, w, *, col0, n_cols, tm, tn, out_dtypes, epilogue=None, extras=(), out_div=None, w_t=False,
            row_split=1):
    t, k = x.shape
    k_axis, n_axis = (1, 0) if w_t else (0, 1)
    assert w.shape[k_axis] == k and n_cols % tn == 0 and t % tm == 0 and col0 + n_cols <= w.shape[n_axis]
    j0, shift = divmod(col0, tn)
    by_element = w_t and shift != 0 and col0 % SUBLANES == 0
    if by_element:
        shift = 0
    if epilogue is None:
        epilogue = lambda acc: (acc,)
    if out_div is None:
        out_div = (1,) * len(out_dtypes)
    assert tm % (row_split * 16) == 0
    kern = functools.partial(_mm_kernel, shift=shift, w_t=w_t, row_split=row_split, n_extra=len(extras),
                             n_out=len(out_dtypes), epilogue=epilogue)
    w_block = (tn, k) if w_t else (k, tn)
    w_index = lambda jj: (jj, 0) if w_t else (0, jj)
    in_specs = [pl.BlockSpec((tm, k), lambda j, i: (i, 0)),
                pl.BlockSpec(w_block, lambda j, i: w_index(j + j0))]
    if by_element:
        in_specs[1] = pl.BlockSpec((pl.Element(tn), pl.Element(k)),
                                   lambda j, i: (pl.multiple_of(col0 + j * tn, SUBLANES), 0))
    ws = [w]
    if shift:
        in_specs.append(pl.BlockSpec(w_block, lambda j, i: w_index(j + j0 + 1)))
        ws.append(w)
    in_specs += [pl.BlockSpec(bs, im) for (_, bs, im) in extras]
    return pl.pallas_call(
        kern,
        grid=(n_cols // tn, t // tm),
        in_specs=in_specs,
        out_specs=[pl.BlockSpec((tm, tn // dv), lambda j, i: (i, j)) for dv in out_div],
        out_shape=[jax.ShapeDtypeStruct((t, n_cols // dv), dt) for dt, dv in zip(out_dtypes, out_div)],
        scratch_shapes=[pltpu.VMEM(w_block, BF16)],
        compiler_params=_params(("arbitrary", "arbitrary")),
        name="matmul",
    )(x, *ws, *[a for (a, _, _) in extras])


def _gates_kernel(ba_ref, alog_ref, dt_ref, col_ref, row_ref, *, hv):
    x = ba_ref[...]
    lane = lax.broadcasted_iota(I32, x.shape, 1)
    beta = _sigmoid(x)
    xs = x + dt_ref[...]
    softplus = jnp.maximum(xs, 0.0) + jnp.log(1.0 + jnp.exp(-jnp.abs(xs)))
    g = -jnp.exp(alog_ref[...]) * softplus
    g = jnp.where((lane >= hv) & (lane < 2 * hv), g, 0.0)
    n = x.shape[0]
    row = lax.broadcasted_iota(I32, x.shape, 0) & (GDN_CHUNK - 1)
    cum = g
    rev = g
    sh = 1
    while sh < GDN_CHUNK:
        cum = cum + jnp.where(row >= sh, pltpu.roll(cum, sh, 0), 0.0)
        rev = rev + jnp.where(row < GDN_CHUNK - sh, pltpu.roll(rev, n - sh, 0), 0.0)
        sh *= 2
    e_cum = pltpu.roll(jnp.exp(cum), hv, 1)
    e_rest = pltpu.roll(jnp.exp(rev - g), 2 * hv, 1)
    out = jnp.where(lane < hv, beta, jnp.where(lane < 2 * hv, cum, jnp.where(lane < 3 * hv, e_cum, e_rest)))
    col_ref[...] = out
    row_ref[...] = out.T


def _gdn_gates(ba, a_log, dt_bias, hv):
    assert 4 * hv <= LANES
    t = ba.shape[0]
    tm = _pick(t, (512, 256, 128))
    pad = lambda v: jnp.zeros((1, LANES), F32).at[0, hv:2 * hv].set(v.astype(F32))
    return pl.pallas_call(
        functools.partial(_gates_kernel, hv=hv),
        grid=(t // tm,),
        in_specs=[pl.BlockSpec((tm, LANES), lambda i: (i, 0)),
                  pl.BlockSpec((1, LANES), lambda i: (0, 0)),
                  pl.BlockSpec((1, LANES), lambda i: (0, 0))],
        out_specs=[pl.BlockSpec((tm, LANES), lambda i: (i, 0)),
                   pl.BlockSpec((LANES, tm), lambda i: (0, i))],
        out_shape=[jax.ShapeDtypeStruct((t, LANES), F32), jax.ShapeDtypeStruct((LANES, t), F32)],
        compiler_params=_params(("arbitrary",)),
        name="gdn_gates",
    )(ba, pad(a_log), pad(dt_bias))


def _causal_conv(x, w, ext_ref):
    n = x.shape[0]
    kw = w.shape[0]
    ext_ref[SUBLANES:, :] = x
    acc = x * w[kw - 1:kw]
    for d in range(1, kw):
        acc = acc + ext_ref[SUBLANES - d:SUBLANES - d + n, :] * w[kw - 1 - d:kw - d]
    ext_ref[:SUBLANES, :] = x[n - SUBLANES:]
    return acc


def _silu(x):
    h = 0.5 * x
    return h + h * jnp.tanh(h)


def _unit_lower_inverses(lows, ii, jj):
    c = lows[0].shape[0]
    eye = jnp.where(ii == jj, 1.0, 0.0)
    pair = (ii >> 1) == (jj >> 1)
    xbs = [(eye - jnp.where(pair, low, 0.0)).astype(BF16) for low in lows]
    lbs = [low.astype(BF16) for low in lows]
    zero = jnp.zeros((c, c), BF16)
    s = 1
    while (2 << s) <= c:
        m = ((ii >> (s + 1)) == (jj >> (s + 1))) & ((ii >> s) != (jj >> s))
        ts = [jnp.dot(jnp.where(m, lb, zero), xb, preferred_element_type=F32) for lb, xb in zip(lbs, xbs)]
        xbs = [jnp.dot(xb, (eye - t).astype(BF16), preferred_element_type=F32).astype(BF16)
               for xb, t in zip(xbs, ts)]
        s += 1
    return xbs


def _gdn_kernel(q_ref, k_ref, v_ref, z_ref, cwq_ref, cwk_ref, cwv_ref, gcol_ref, grow_ref, nw_ref,
                o_ref, s_ref, tq_ref, tk_ref, tv_ref, qn_ref, kn_ref, vn_ref, u_ref, wq_ref, attn_ref, kd_ref,
                *, nq, vper, dk):
    @pl.when(pl.program_id(2) == 0)
    def _():
        s_ref[...] = jnp.zeros_like(s_ref)
        for t_ref in (tq_ref, tk_ref, tv_ref):
            t_ref[:SUBLANES, :] = jnp.zeros((SUBLANES, t_ref.shape[1]), F32)

    tb = q_ref.shape[0]
    c = GDN_CHUNK
    nc = tb // c
    nv = nq * vper
    head = lambda h: slice(h * dk, (h + 1) * dk)

    q = _silu(_causal_conv(q_ref[...].astype(F32), cwq_ref[...], tq_ref))
    k = _silu(_causal_conv(k_ref[...].astype(F32), cwk_ref[...], tk_ref))
    for hq in range(nq):
        qh = q[:, head(hq)]
        kh = k[:, head(hq)]
        qn_ref[:, head(hq)] = qh * (lax.rsqrt(jnp.sum(qh * qh, axis=-1, keepdims=True) + RMS_EPS) * (dk ** -0.5))
        kn_ref[:, head(hq)] = kh * lax.rsqrt(jnp.sum(kh * kh, axis=-1, keepdims=True) + RMS_EPS)

    cols = gcol_ref[...]
    grows = grow_ref[...]
    nw = nw_ref[...]
    ii = lax.broadcasted_iota(I32, (c, c), 0)
    jj = lax.broadcasted_iota(I32, (c, c), 1)
    incl = ii >= jj
    strict = ii > jj

    def col(ci, which, h):
        return cols[ci * c:(ci + 1) * c, which * nv + h:which * nv + h + 1]

    def state_step(ci):
        rows = slice(ci * c, (ci + 1) * c)
        ps = [ci * nv + h for h in range(nv)]
        r1 = [jnp.dot(wq_ref[p], s_ref[h].astype(BF16), preferred_element_type=F32) for h, p in enumerate(ps)]
        v_new = [(u_ref[p] - r[:c]).astype(BF16) for p, r in zip(ps, r1)]
        outs = [r[c:] + jnp.dot(attn_ref[p], vn, preferred_element_type=F32) for p, r, vn in zip(ps, r1, v_new)]
        for h, (p, vn) in enumerate(zip(ps, v_new)):
            g_tot = cols[(ci + 1) * c - 1:(ci + 1) * c, 2 * nv + h:2 * nv + h + 1]
            s_ref[h] = s_ref[h] * g_tot + lax.dot_general(
                kd_ref[p], vn, (((0,), (0,)), ((), ())), preferred_element_type=F32)
        for h, o in enumerate(outs):
            zc = z_ref[rows, head(h)].astype(F32)
            var = jnp.mean(o * o, axis=-1, keepdims=True)
            o_ref[rows, head(h)] = (o * lax.rsqrt(var + RMS_EPS) * nw * _silu(zc)).astype(BF16)

    problems = [(ci, h) for ci in range(nc) for h in range(nv)]
    gram = {}
    chunks_done = 0
    for g0 in range(0, len(problems), GDN_INV_GROUP):
        group = problems[g0:g0 + GDN_INV_GROUP]
        lows = []
        for ci, h in group:
            hq = h // vper
            rows = slice(ci * c, (ci + 1) * c)
            if (ci, hq) not in gram:
                kcb = kn_ref[rows, head(hq)].astype(BF16)
                qcb = qn_ref[rows, head(hq)].astype(BF16)
                gram[(ci, hq)] = lax.dot_general(jnp.concatenate([kcb, qcb], axis=0), kcb,
                                                 (((1,), (1,)), ((), ())), preferred_element_type=F32)
            a = gram[(ci, hq)]
            grow = grows[h:h + 1, rows]
            decay = jnp.where(incl, jnp.exp(jnp.minimum(col(ci, 1, h) - grow, 0.0)), 0.0)
            lows.append(jnp.where(strict, col(ci, 0, h) * a[:c] * decay, 0.0))
            attn_ref[ci * nv + h] = jnp.where(incl, a[c:] * decay, 0.0).astype(BF16)
        if g0 == 0:
            vn_ref[...] = _silu(_causal_conv(v_ref[...].astype(F32), cwv_ref[...], tv_ref))
        xs = _unit_lower_inverses(lows, ii, jj)
        for x, (ci, h) in zip(xs, group):
            hq = h // vper
            rows = slice(ci * c, (ci + 1) * c)
            p = ci * nv + h
            kc = kn_ref[rows, head(hq)]
            bcol = col(ci, 0, h)
            eg = col(ci, 2, h)
            rhs = jnp.concatenate([vn_ref[rows, head(h)] * bcol, kc * (bcol * eg)], axis=1).astype(BF16)
            uw = jnp.dot(x, rhs, preferred_element_type=F32)
            u_ref[p] = uw[:, :dk]
            wq_ref[p, :c, :] = uw[:, dk:].astype(BF16)
            wq_ref[p, c:, :] = (qn_ref[rows, head(hq)] * eg).astype(BF16)
            kd_ref[p] = (kc * col(ci, 3, h)).astype(BF16)
        chunks_ready = (g0 + len(group)) // nv
        for ci in range(chunks_done, chunks_ready):
            state_step(ci)
        chunks_done = chunks_ready


def _gdn(proj_a, conv_w, gcol, grow, norm_w, *, bsz, seq, hq, hv, dk):
    t = bsz * seq
    vper = hv // hq
    nq = _pick(hq, (GDN_QK_GROUP, 2, 1))
    nv = nq * vper
    ng = hq // nq
    tb = _pick(seq, (GDN_STEP, 128, 64))
    ns = seq // tb
    nprob = (tb // GDN_CHUNK) * nv
    qw = nq * dk
    vw = nv * dk
    v_blk0 = 2 * hq * dk // vw
    z_blk0 = (2 * hq * dk + hv * dk) // vw
    kw = conv_w.shape[0]
    gcol = gcol[:, :4 * hv].reshape(t, 4, ng, nv).transpose(2, 0, 1, 3).reshape(ng, t, 4 * nv)
    grow = grow[hv:2 * hv].reshape(ng, nv, t)
    rowblk = lambda b, g, s: b * ns + s
    kern = functools.partial(_gdn_kernel, nq=nq, vper=vper, dk=dk)
    return pl.pallas_call(
        kern,
        grid=(bsz, ng, ns),
        in_specs=[
            pl.BlockSpec((tb, qw), lambda b, g, s: (rowblk(b, g, s), g)),
            pl.BlockSpec((tb, qw), lambda b, g, s: (rowblk(b, g, s), ng + g)),
            pl.BlockSpec((tb, vw), lambda b, g, s: (rowblk(b, g, s), v_blk0 + g)),
            pl.BlockSpec((tb, vw), lambda b, g, s: (rowblk(b, g, s), z_blk0 + g)),
            pl.BlockSpec((kw, qw), lambda b, g, s: (0, g)),
            pl.BlockSpec((kw, qw), lambda b, g, s: (0, ng + g)),
            pl.BlockSpec((kw, vw), lambda b, g, s: (0, v_blk0 + g)),
            pl.BlockSpec((None, tb, 4 * nv), lambda b, g, s: (g, rowblk(b, g, s), 0)),
            pl.BlockSpec((None, nv, tb), lambda b, g, s: (g, 0, rowblk(b, g, s))),
            pl.BlockSpec((1, dk), lambda b, g, s: (0, 0)),
        ],
        out_specs=pl.BlockSpec((tb, vw), lambda b, g, s: (rowblk(b, g, s), g)),
        out_shape=jax.ShapeDtypeStruct((t, hv * dk), BF16),
        scratch_shapes=[pltpu.VMEM((nv, dk, dk), F32),
                        pltpu.VMEM((tb + SUBLANES, qw), F32),
                        pltpu.VMEM((tb + SUBLANES, qw), F32),
                        pltpu.VMEM((tb + SUBLANES, vw), F32),
                        pltpu.VMEM((tb, qw), F32),
                        pltpu.VMEM((tb, qw), F32),
                        pltpu.VMEM((tb, vw), F32),
                        pltpu.VMEM((nprob, GDN_CHUNK, dk), F32),
                        pltpu.VMEM((nprob, 2 * GDN_CHUNK, dk), BF16),
                        pltpu.VMEM((nprob, GDN_CHUNK, GDN_CHUNK), BF16),
                        pltpu.VMEM((nprob, GDN_CHUNK, dk), BF16)],
        compiler_params=_params(("arbitrary", "arbitrary", "arbitrary")),
        name="gdn",
    )(proj_a, proj_a, proj_a, proj_a, conv_w, conv_w, conv_w, gcol, grow, norm_w.reshape(1, dk))


def _sc_kernel(b_ref, c_ref, x_ref, w_ref, u_ref, tail_ref):
    @pl.when(pl.program_id(1) == 0)
    def _():
        tail_ref[:SUBLANES, :] = jnp.zeros((SUBLANES, tail_ref.shape[1]), F32)

    p = c_ref[...].astype(F32) * x_ref[...].astype(F32)
    u_ref[...] = (b_ref[...].astype(F32) * _causal_conv(p, w_ref[...], tail_ref)).astype(BF16)


def _short_conv(proj_b, conv_w, *, bsz, seq, width):
    t = bsz * seq
    ts = _pick(seq, (512, 256, 128, 64))
    ns = seq // ts
    kw = conv_w.shape[0]
    return pl.pallas_call(
        _sc_kernel,
        grid=(bsz, ns),
        in_specs=[pl.BlockSpec((ts, width), lambda b, s: (b * ns + s, 0)),
                  pl.BlockSpec((ts, width), lambda b, s: (b * ns + s, 1)),
                  pl.BlockSpec((ts, width), lambda b, s: (b * ns + s, 2)),
                  pl.BlockSpec((kw, width), lambda b, s: (0, 0))],
        out_specs=pl.BlockSpec((ts, width), lambda b, s: (b * ns + s, 0)),
        out_shape=jax.ShapeDtypeStruct((t, width), BF16),
        scratch_shapes=[pltpu.VMEM((ts + SUBLANES, width), F32)],
        compiler_params=_params(("arbitrary", "arbitrary")),
        name="short_conv",
    )(proj_b, proj_b, proj_b, conv_w)


def _router_kernel(h_ref, w_ref, b_ref, idx_ref, gate_ref, rank_ref, cnt_ref, run_ref):
    @pl.when(pl.program_id(0) == 0)
    def _():
        run_ref[...] = jnp.zeros_like(run_ref)

    tm = h_ref.shape[0]
    h = h_ref[...]
    h_hi = h.astype(BF16)
    h_lo = (h - h_hi.astype(F32)).astype(BF16)
    w2 = w_ref[...]
    first = jnp.dot(h_hi, w2, preferred_element_type=F32)
    logits = (first[:, :LANES] + first[:, LANES:]
              + jnp.dot(h_lo, w2[:, :LANES], preferred_element_type=F32) + b_ref[...])
    lane = lax.broadcasted_iota(I32, logits.shape, 1)
    lane_f = lane.astype(F32)
    cur = logits
    idxs, vals = [], []
    for _ in range(TOP_K):
        m = jnp.max(cur, axis=-1, keepdims=True)
        ix = jnp.min(jnp.where(cur == m, lane_f, float(LANES)), axis=-1, keepdims=True).astype(I32)
        idxs.append(ix)
        vals.append(m)
        cur = jnp.where(lane == ix, -jnp.inf, cur)
    es = [jnp.exp(val - vals[0]) for val in vals]
    den = es[0]
    for e in es[1:]:
        den = den + e
    onehot = jnp.zeros(logits.shape, F32)
    for ix in idxs:
        onehot = onehot + jnp.where(lane == ix, 1.0, 0.0)
    ri = lax.broadcasted_iota(I32, (tm, tm), 0)
    ci = lax.broadcasted_iota(I32, (tm, tm), 1)
    tri = jnp.where(ri > ci, 1.0, 0.0).astype(BF16)
    before = jnp.dot(tri, onehot.astype(BF16), preferred_element_type=F32) + run_ref[...]
    idx_out = jnp.zeros(logits.shape, I32)
    rank_out = jnp.zeros(logits.shape, I32)
    gate_out = jnp.zeros(logits.shape, F32)
    for kk in range(TOP_K):
        rank = jnp.sum(jnp.where(lane == idxs[kk], before, 0.0), axis=-1, keepdims=True).astype(I32)
        idx_out = jnp.where(lane == kk, idxs[kk], idx_out)
        rank_out = jnp.where(lane == kk, rank, rank_out)
        gate_out = jnp.where(lane == kk, es[kk] / den, gate_out)
    idx_ref[...] = idx_out
    rank_ref[...] = rank_out
    gate_ref[...] = gate_out
    run_ref[...] = run_ref[...] + jnp.sum(onehot, axis=0, keepdims=True)
    cnt_ref[...] = run_ref[...].astype(I32)


def _router(h, w_router, b_router):
    t, d = h.shape
    e = w_router.shape[1]
    tm = _pick(t, (256, 128, 64, 32, 16, 8))
    wp = jnp.zeros((d, LANES), F32).at[:, :e].set(w_router)
    w_hi = wp.astype(BF16)
    w_lo = (wp - w_hi.astype(F32)).astype(BF16)
    w2 = jnp.concatenate([w_hi, w_lo], axis=1)
    bp = jnp.full((1, LANES), -1e30, F32).at[0, :e].set(b_router)
    tile = pl.BlockSpec((tm, LANES), lambda i: (i, 0))
    return pl.pallas_call(
        _router_kernel,
        grid=(t // tm,),
        in_specs=[pl.BlockSpec((tm, d), lambda i: (i, 0)),
                  pl.BlockSpec((d, 2 * LANES), lambda i: (0, 0)),
                  pl.BlockSpec((1, LANES), lambda i: (0, 0))],
        out_specs=[tile, tile, tile, pl.BlockSpec((1, LANES), lambda i: (0, 0))],
        out_shape=[jax.ShapeDtypeStruct((t, LANES), I32), jax.ShapeDtypeStruct((t, LANES), F32),
                   jax.ShapeDtypeStruct((t, LANES), I32), jax.ShapeDtypeStruct((1, LANES), I32)],
        scratch_shapes=[pltpu.VMEM((1, LANES), F32)],
        compiler_params=_params(("arbitrary",)),
        name="router",
    )(h, w2, bp)


def _row_copy(src, s_row, dst, d_row, sem):
    return pltpu.make_async_copy(src.at[pl.ds(s_row, 1)], dst.at[pl.ds(d_row, 1)], sem)


def _dispatch_kernel(pos_ref, x_ref, rows_in_ref, rows_ref, sem):
    del rows_in_ref
    tm = x_ref.shape[0]
    base = pl.program_id(0) * (tm * TOP_K)

    for t in range(tm):
        for kk in range(TOP_K):
            _row_copy(x_ref, t, rows_ref, pos_ref[base + (t * TOP_K + kk)], sem).start(priority=kk & 1)
    n = tm * TOP_K
    pltpu.make_async_copy(rows_ref.at[pl.ds(0, n)], rows_ref.at[pl.ds(0, n)], sem).wait()


def _dispatch(h, pos, n_rows):
    t, d = h.shape
    tm = _pick(t, (128, 64, 32, 16, 8))
    rows0 = jnp.zeros((n_rows, d), h.dtype)
    return pl.pallas_call(
        _dispatch_kernel,
        grid_spec=pltpu.PrefetchScalarGridSpec(
            num_scalar_prefetch=1,
            grid=(t // tm,),
            in_specs=[pl.BlockSpec((tm, d), lambda i, pos: (i, 0)),
                      pl.BlockSpec(memory_space=pl.ANY)],
            out_specs=pl.BlockSpec(memory_space=pl.ANY),
            scratch_shapes=[pltpu.SemaphoreType.DMA(())],
        ),
        out_shape=jax.ShapeDtypeStruct((n_rows, d), h.dtype),
        input_output_aliases={2: 0},
        compiler_params=_params(("arbitrary",)),
        name="dispatch",
    )(pos, h, rows0)


def _swiglu_interleaved(gu):
    lane = lax.broadcasted_iota(I32, (gu.shape[0], LANES), 1)
    even = (lane & 1) == 0
    parts = []
    for c in range(gu.shape[1] // LANES):
        g = gu[:, c * LANES:(c + 1) * LANES]
        gt = jnp.minimum(g, SWIGLU_LIMIT)
        glu = gt * _sigmoid(SWIGLU_ALPHA * gt)
        up = jnp.clip(g, -SWIGLU_LIMIT, SWIGLU_LIMIT) + 1.0
        parts.append(jnp.where(even, pltpu.roll(up, LANES - 1, 1) * glu, 0.0))
    outs = [parts[2 * m] + pltpu.roll(parts[2 * m + 1], 1, 1) for m in range(len(parts) // 2)]
    return jnp.concatenate(outs, axis=1) if len(outs) > 1 else outs[0]


def _interleave_rows_bf16(w_ref, out_ref):
    half = LANES // 2
    for m in range(w_ref.shape[0] // LANES):
        lo = w_ref[m * LANES:m * LANES + half, :].astype(BF16).astype(F32)
        hi = w_ref[m * LANES + half:(m + 1) * LANES, :].astype(BF16).astype(F32)
        packed = (lax.bitcast_convert_type(lo, jnp.uint32) >> 16) | (
            lax.bitcast_convert_type(hi, jnp.uint32) & jnp.uint32(0xFFFF0000))
        out_ref[m * LANES:(m + 1) * LANES, :] = pltpu.bitcast(packed, BF16)


def _expert_kernel(ie_ref, ist_ref, inb_ref, ihalf_ref, isrc_ref, tok_ref, hp_ref, wgu_ref, bgu_ref, wd_ref,
                   bd_ref, y_ref, xb_ref, acc_ref, stage_ref, wgub_ref, wdb_ref, ost_ref, sem_in, sem_out,
                   *, tm, n_j):
    del ie_ref
    wi = pl.program_id(0)
    j = pl.program_id(1)
    n_items = pl.num_programs(0)
    nblk = inb_ref[wi]
    half_last = ihalf_ref[wi]
    nfull = nblk - half_last
    half = stage_ref.shape[1]
    n_stage = stage_ref.shape[0]
    per_pass = n_stage // n_j

    def block(b):
        return pl.ds(pl.multiple_of(b * tm, tm), tm)

    def hbm_block(item, b):
        return pl.ds(pl.multiple_of(ist_ref[item] + b * tm, tm), tm)

    def gather_rows(item, first, count):
        base = isrc_ref[item]
        for r in range(count):
            _row_copy(hp_ref, tok_ref[base + (first + r)], stage_ref, first + r, sem_in).start(priority=r & 1)

    def wait_rows():
        pltpu.make_async_copy(hp_ref.at[pl.ds(0, n_stage)], stage_ref, sem_in).wait()

    def y_copy(item, b):
        slot = b & 1
        return pltpu.make_async_copy(ost_ref.at[slot], y_ref.at[hbm_block(item, b)], sem_out.at[slot])

    def drain_stores(item):
        n = inb_ref[item]
        for back in (2, 1):
            @pl.when(n >= back)
            def _():
                y_copy(item, n - back).wait()

    def for_blocks(item, fn):
        def body(b, carry):
            fn(item, b)
            return carry
        lax.fori_loop(0, inb_ref[item], body, 0)

    def unpack(item, b):
        lo, hi = _unpack_halves_bf16(stage_ref[block(b), :])
        xb_ref[block(b), :half] = lo
        xb_ref[block(b), half:] = hi

    @pl.when(j == 0)
    def _():
        @pl.when(wi == 0)
        def _():
            gather_rows(0, 0, n_stage)
            acc_ref[...] = jnp.zeros_like(acc_ref)
            ost_ref[...] = jnp.zeros_like(ost_ref)

        @pl.when(wi > 0)
        def _():
            drain_stores(wi - 1)

        @pl.when((wi == 0) | (inb_ref[jnp.maximum(wi - 1, 0)] > 0))
        def _():
            wait_rows()

        for_blocks(wi, unpack)

    @pl.when(nblk > 0)
    def _():
        bias = bgu_ref[...]
        n_ct, _, ct = wgub_ref.shape

        def convert_weights():
            for c in range(n_ct):
                wgub_ref[c] = wgu_ref[:, c * ct:(c + 1) * ct].astype(BF16)
            _interleave_rows_bf16(wd_ref, wdb_ref)

        def mlp(b, m):
            x = xb_ref[pl.ds(pl.multiple_of(b * tm, tm), m), :]
            gu = jnp.concatenate([jnp.dot(x, wgub_ref[c], preferred_element_type=F32) for c in range(n_ct)],
                                 axis=1) + bias
            act = _swiglu_interleaved(gu).astype(BF16)
            return jnp.dot(act, wdb_ref[...], preferred_element_type=F32)

        def accumulate(b, m=tm):
            rows = pl.ds(pl.multiple_of(b * tm, tm), m)
            prev = jnp.where(j == 0, jnp.broadcast_to(bd_ref[...], (m, acc_ref.shape[1])), acc_ref[rows, :])
            acc_ref[rows, :] = prev + mlp(b, m)

        def finish(b, m=tm):
            prev = acc_ref[pl.ds(pl.multiple_of(b * tm, tm), m), :] if n_j > 1 else bd_ref[...]
            ost_ref[b & 1, 0:m, :] = _pack_halves_bf16(prev + mlp(b, m))
            y_copy(wi, b).start()

        def run(fn, stores):
            def prologue():
                convert_weights()
                gather_rows(wi + 1, j * per_pass, per_pass)

            @pl.when(nfull >= 2)
            def _():
                prologue()
                fn(0)
                fn(1)

            @pl.when(nfull == 1)
            def _():
                prologue()
                fn(0)

            def pair(i, carry):
                if stores:
                    y_copy(wi, 2 * i - 2).wait()
                    y_copy(wi, 2 * i - 1).wait()
                fn(2 * i)
                fn(2 * i + 1)
                return carry
            lax.fori_loop(1, nfull // 2, pair, 0)

            @pl.when(((nfull & 1) == 1) & (nfull >= 3))
            def _():
                if stores:
                    y_copy(wi, nfull - 3).wait()
                fn(nfull - 1)

            @pl.when((half_last == 1) & (nfull == 0))
            def _():
                prologue()
                fn(0, tm // 2)

            @pl.when((half_last == 1) & (nfull > 0))
            def _():
                if stores:
                    @pl.when(nfull >= 2)
                    def _():
                        y_copy(wi, nfull - 2).wait()
                fn(nfull, tm // 2)

        if n_j > 1:
            @pl.when(j < n_j - 1)
            def _():
                run(accumulate, False)

        @pl.when(j == n_j - 1)
        def _():
            run(finish, True)

    @pl.when((wi == n_items - 1) & (j == n_j - 1))
    def _():
        @pl.when(nblk > 0)
        def _():
            wait_rows()
        drain_stores(wi)
        ost_ref[0] = jnp.zeros(ost_ref.shape[1:], U32)

        def fill(b, carry):
            cp = pltpu.make_async_copy(ost_ref.at[0], y_ref.at[pl.ds(pl.multiple_of(b * tm, tm), tm)],
                                       sem_out.at[0])
            cp.start()
            cp.wait()
            return carry
        lax.fori_loop(ist_ref[n_items] // tm, y_ref.shape[0] // tm, fill, 0)


def _experts(hp, tok_list, n_rows, item_e, item_start, item_nblk, item_half, item_src, w_gate_up, b_gate_up,
             w_down, b_down, *, tm, xmax):
    n_exp, d, f2 = w_gate_up.shape
    assert hp.shape[1] * 2 == d
    f = f2 // 2
    tf = _pick(f, (MOE_TF, LANES))
    n_j = f // tf
    ct = min(2 * tf, MXU_N)
    n_items = item_e.shape[0]
    n_stage = -(-xmax // (n_j * SUBLANES)) * (n_j * SUBLANES)
    assert hp.shape[0] >= n_stage and tok_list.shape[0] >= TOP_K * hp.shape[0] + n_stage

    def jeff(j, inb, wi):
        return jnp.where(inb[wi] > 0, j, n_j - 1)

    def wspec(shape, imap):
        return pl.BlockSpec(shape, lambda wi, j, ie, ist, inb, ih, isrc, tok: imap(wi, j, ie, inb))

    kern = functools.partial(_expert_kernel, tm=tm, n_j=n_j)
    return pl.pallas_call(
        kern,
        grid_spec=pltpu.PrefetchScalarGridSpec(
            num_scalar_prefetch=6,
            grid=(n_items, n_j),
            in_specs=[
                pl.BlockSpec(memory_space=pl.ANY),
                wspec((None, d, 2 * tf), lambda wi, j, ie, inb: (ie[wi], 0, jeff(j, inb, wi))),
                wspec((None, 1, 2 * tf), lambda wi, j, ie, inb: (ie[wi], 0, jeff(j, inb, wi))),
                wspec((None, tf, d), lambda wi, j, ie, inb: (ie[wi], jeff(j, inb, wi), 0)),
                wspec((None, 1, d), lambda wi, j, ie, inb: (ie[wi], 0, 0)),
            ],
            out_specs=pl.BlockSpec(memory_space=pl.ANY),
            scratch_shapes=[pltpu.VMEM((xmax, d), BF16),
                            pltpu.VMEM((xmax, d), F32),
                            pltpu.VMEM((n_stage, d // 2), U32),
                            pltpu.VMEM((2 * tf // ct, d, ct), BF16),
                            pltpu.VMEM((tf, d), BF16),
                            pltpu.VMEM((2, tm, d // 2), U32),
                            pltpu.SemaphoreType.DMA(()),
                            pltpu.SemaphoreType.DMA((2,))],
        ),
        out_shape=jax.ShapeDtypeStruct((n_rows, d // 2), U32),
        compiler_params=_params(("arbitrary", "arbitrary")),
        name="experts",
    )(item_e, item_start, item_nblk, item_half, item_src, tok_list, hp, w_gate_up,
      b_gate_up.reshape(n_exp, 1, f2), w_down, b_down.reshape(n_exp, 1, d))


def _combine_kernel(pos_ref, y_ref, gate_ref, h_ref, g_ref, b_ref, o_ref, ybuf0_ref, ybuf1_ref, sem,
                    *, alpha):
    tm = h_ref.shape[0]
    i = pl.program_id(0)
    n_tiles = pl.num_programs(0)
    n = tm * TOP_K
    bufs = (ybuf0_ref, ybuf1_ref)

    def gather(tile, s):
        base = tile * n
        for t in range(tm):
            for kk in range(TOP_K):
                _row_copy(y_ref, pos_ref[base + (t * TOP_K + kk)], bufs[s], kk * tm + t,
                          sem.at[s]).start(priority=kk & 1)

    def wait_buf(s):
        pltpu.make_async_copy(y_ref.at[pl.ds(0, n)], bufs[s], sem.at[s]).wait()

    def step(s):
        wait_buf(s)
        gather(jnp.minimum(i + 1, n_tiles - 1), 1 - s)
        gate = gate_ref[...]
        half = y_ref.shape[1]
        ffn_lo = jnp.zeros((tm, half), F32)
        ffn_hi = jnp.zeros((tm, half), F32)
        for kk in range(TOP_K):
            p = bufs[s][kk * tm:(kk + 1) * tm, :]
            gk = gate[:, kk:kk + 1]
            ffn_lo = ffn_lo + gk * lax.bitcast_convert_type(p << 16, F32)
            ffn_hi = ffn_hi + gk * lax.bitcast_convert_type(p & jnp.uint32(0xFFFF0000), F32)
        ffn = jnp.concatenate([ffn_lo, ffn_hi], axis=1)
        o_ref[...] = _layer_norm(alpha * h_ref[...] + ffn, g_ref[...], b_ref[...])

        @pl.when(i == n_tiles - 1)
        def _():
            wait_buf(1 - s)

    @pl.when(i == 0)
    def _():
        gather(0, 0)

    for s in range(2):
        @pl.when((i & 1) == s)
        def _():
            step(s)


def _combine(y, pos, gate, h, ln_g, ln_b, alpha):
    t, d = h.shape
    tm = _pick(t, (128, 64, 32, 16, 8))
    assert y.shape[1] * 2 == d
    return pl.pallas_call(
        functools.partial(_combine_kernel, alpha=alpha),
        grid_spec=pltpu.PrefetchScalarGridSpec(
            num_scalar_prefetch=1,
            grid=(t // tm,),
            in_specs=[pl.BlockSpec(memory_space=pl.ANY),
                      pl.BlockSpec((tm, LANES), lambda i, pos: (i, 0)),
                      pl.BlockSpec((tm, d), lambda i, pos: (i, 0)),
                      pl.BlockSpec((1, d), lambda i, pos: (0, 0)),
                      pl.BlockSpec((1, d), lambda i, pos: (0, 0))],
            out_specs=pl.BlockSpec((tm, d), lambda i, pos: (i, 0)),
            scratch_shapes=[pltpu.VMEM((TOP_K * tm, d // 2), U32), pltpu.VMEM((TOP_K * tm, d // 2), U32),
                            pltpu.SemaphoreType.DMA((2,))],
        ),
        out_shape=jax.ShapeDtypeStruct((t, d), F32),
        compiler_params=_params(("arbitrary",)),
        name="combine",
    )(pos, y, gate, h, ln_g.reshape(1, d), ln_b.reshape(1, d))


def _moe_plan(idx, rank, counts, *, tm, xmax, n_items):
    n_exp = counts.shape[0]
    padded = (counts + tm - 1) // tm * tm
    pstart = jnp.cumsum(padded) - padded
    onehot = idx[..., None] == jnp.arange(n_exp, dtype=idx.dtype)
    pos = (jnp.sum(jnp.where(onehot, pstart, 0), axis=-1) + rank).reshape(-1).astype(I32)
    per_e = (padded + xmax - 1) // xmax
    cum = jnp.cumsum(per_e)
    total = cum[-1]
    wi = jnp.arange(n_items, dtype=I32)
    valid = wi < total
    e_w = jnp.minimum(jnp.searchsorted(cum, jnp.minimum(wi, total - 1), side="right"), n_exp - 1).astype(I32)
    local = jnp.minimum(wi, total - 1) - (cum - per_e)[e_w]
    start = (pstart[e_w] + local * xmax).astype(I32)
    nblk = jnp.where(valid, jnp.clip(padded[e_w] - local * xmax, 0, xmax) // tm, 0).astype(I32)
    real = jnp.clip(counts[e_w] - local * xmax, 0, xmax)
    half_last = ((nblk > 0) & (real - (nblk - 1) * tm <= tm // 2)).astype(I32)
    start = jnp.concatenate([start, jnp.sum(padded, keepdims=True).astype(I32)])
    order = jnp.argsort(idx.reshape(-1), stable=True)
    tok_list = jnp.concatenate([(order // idx.shape[1]).astype(I32), jnp.zeros((xmax + 64,), I32)])
    dense = jnp.cumsum(counts) - counts
    src = jnp.where(valid, dense[e_w] + local * xmax, 0).astype(I32)
    src = jnp.concatenate([src, jnp.zeros((1,), I32)])
    return pos, e_w, start, nblk, half_last, src, tok_list


def _layer(h0, h0b, p, *, bsz, seq, alpha):
    t, d = h0.shape
    w_in = p["w_in"]
    hv = p["gdn_a_log"].shape[0]
    dk = p["gdn_norm_w"].shape[0]
    qkv_dim = p["gdn_conv_w"].shape[1]
    v_dim = hv * dk
    hq = (qkv_dim - v_dim) // (2 * dk)
    scw = p["sc_conv_w"].shape[1]
    assert 2 * hv <= LANES and hv % hq == 0
    n_a = qkv_dim + v_dim
    off_ba = n_a
    off_b = n_a + 2 * hv
    n_b = 3 * scw + 2 * d
    assert w_in.shape[1] == off_b + n_b

    w_in_t = w_in.T
    tm = _pick(t, (1024, 512, 256, 128))
    tn_a = _pick(n_a, (1024, 512, 256, 128))
    (proj_a,) = _matmul(h0b, w_in_t, col0=0, n_cols=n_a, tm=tm, tn=tn_a, out_dtypes=(BF16,), w_t=True)
    tn_b = _pick(n_b, (1024, 512, 256, 128))
    (proj_b,) = _matmul(h0b, w_in_t, col0=off_b, n_cols=n_b, tm=tm, tn=tn_b, out_dtypes=(BF16,), w_t=True)
    assert off_ba + LANES <= w_in.shape[1]
    (ba,) = _matmul(h0b, w_in_t, col0=off_ba, n_cols=LANES, tm=tm, tn=LANES, out_dtypes=(F32,), w_t=True)

    gcol, grow = _gdn_gates(ba, p["gdn_a_log"], p["gdn_dt_bias"], hv)
    o_n = _gdn(proj_a, p["gdn_conv_w"], gcol, grow, p["gdn_norm_w"], bsz=bsz, seq=seq, hq=hq, hv=hv, dk=dk)

    u = _short_conv(proj_b, p["sc_conv_w"], bsz=bsz, seq=seq, width=scw)

    tn = _pick(d, (512, 256, 128))
    ga0 = 3 * scw // tn
    gb0 = (3 * scw + d) // tn
    split = 2 if tm % 32 == 0 else 1
    (part_a,) = _matmul(
        o_n, p["w_out_gdn"], col0=0, n_cols=d, tm=tm, tn=tn, out_dtypes=(F32,), row_split=split,
        epilogue=lambda acc, ga: (_sigmoid(ga.astype(F32)) * acc,),
        extras=[(proj_b, (tm, tn), lambda j, i: (i, ga0 + j))])
    (merged,) = _matmul(
        u, p["w_out_sc"], col0=0, n_cols=d, tm=tm, tn=tn, out_dtypes=(BF16,), row_split=split,
        epilogue=lambda acc, gb, pa: (pa + _sigmoid(gb.astype(F32)) * acc,),
        extras=[(proj_b, (tm, tn), lambda j, i: (i, gb0 + j)),
                (part_a, (tm, tn), lambda j, i: (i, j))])
    tm3 = _pick(t, (512, 256, 128))

    def mix_epilogue(acc, hh, g, b):
        hn = _layer_norm(alpha * hh + acc, g, b)
        return hn, _pack_halves_bf16(hn)

    h1, h1p = _matmul(
        merged, p["w_out"].astype(BF16), col0=0, n_cols=d, tm=tm3, tn=d, out_dtypes=(F32, U32), out_div=(1, 2),
        epilogue=mix_epilogue, row_split=2,
        extras=[(h0, (tm3, d), lambda j, i: (i, 0)),
                (p["ln_mix_g"].reshape(1, d), (1, d), lambda j, i: (0, 0)),
                (p["ln_mix_b"].reshape(1, d), (1, d), lambda j, i: (0, 0))])

    n_exp = p["w_router"].shape[1]
    idx, gate, rank, cnt = _router(h1, p["w_router"], p["b_router"])
    mtm = MOE_TM
    xmax = MOE_XMAX
    n_rows = t * TOP_K + n_exp * mtm
    n_items = n_exp + n_rows // xmax
    pos, item_e, item_start, item_nblk, item_half, item_src, tok_list = _moe_plan(
        idx[:, :TOP_K], rank[:, :TOP_K], cnt[0, :n_exp], tm=mtm, xmax=xmax, n_items=n_items)
    y = _experts(h1p, tok_list, n_rows, item_e, item_start, item_nblk, item_half, item_src, p["w_gate_up"],
                 p["b_gate_up"], p["w_down"], p["b_down"], tm=mtm, xmax=xmax)
    h2 = _combine(y, pos, gate, h1, p["ln_ffn_g"], p["ln_ffn_b"], alpha)
    return h2


_LAYER_PARAMS = ("w_in", "gdn_conv_w", "gdn_a_log", "gdn_dt_bias", "gdn_norm_w", "w_out_gdn", "sc_conv_w",
                 "w_out_sc", "w_out", "ln_mix_g", "ln_mix_b", "w_router", "b_router", "w_gate_up", "b_gate_up",
                 "w_down", "b_down", "ln_ffn_g", "ln_ffn_b")


def kernel(x, ln_in_g, ln_in_b, w_in, gdn_conv_w, gdn_a_log, gdn_dt_bias, gdn_norm_w, w_out_gdn, sc_conv_w,
           w_out_sc, w_out, ln_mix_g, ln_mix_b, w_router, b_router, w_gate_up, b_gate_up, w_down, b_down,
           ln_ffn_g, ln_ffn_b):
    stacked = dict(zip(_LAYER_PARAMS, (w_in, gdn_conv_w, gdn_a_log, gdn_dt_bias, gdn_norm_w, w_out_gdn,
                                       sc_conv_w, w_out_sc, w_out, ln_mix_g, ln_mix_b, w_router, b_router,
                                       w_gate_up, b_gate_up, w_down, b_down, ln_ffn_g, ln_ffn_b)))
    bsz, seq, d = x.shape
    depth = w_in.shape[0]
    alpha = (2 * depth) ** 0.25
    h, hb = _ln_in(x.reshape(bsz * seq, d), ln_in_g, ln_in_b)
    for l in range(depth):
        p = {name: arr[l] for name, arr in stacked.items()}
        h = _layer(h, hb, p, bsz=bsz, seq=seq, alpha=alpha)
        if l + 1 < depth:
            hb = h.astype(BF16)
    return h.reshape(bsz, seq, d)
```

```python
import functools

import jax
import jax.numpy as jnp
from jax import lax
from jax.experimental import pallas as pl
from jax.experimental.pallas import tpu as pltpu

F32 = jnp.float32
BF16 = jnp.bfloat16
I32 = jnp.int32
U32 = jnp.uint32

LANES = 128
SUBLANES = 8
MXU_N = 256
VMEM_LIMIT = 60 << 20

TOP_K = 4
SWIGLU_LIMIT = 7.0
SWIGLU_ALPHA = 1.702
LN_EPS = 1e-5
RMS_EPS = 1e-6
GDN_CHUNK = 64
GDN_STEP = 512
GDN_QK_GROUP = 4
GDN_INV_GROUP = 16
MOE_TM = 256
MOE_XMAX = 1280
MOE_TF = 512


def _params(sem):
    return pltpu.CompilerParams(dimension_semantics=sem, vmem_limit_bytes=VMEM_LIMIT)


def _pick(n, candidates):
    for c in candidates:
        if n % c == 0:
            return c
    raise ValueError(f"no tile for {n} in {candidates}")


def _sigmoid(x):
    return 0.5 * jnp.tanh(0.5 * x) + 0.5


def _layer_norm(xf, g, b):
    mu = jnp.mean(xf, axis=-1, keepdims=True)
    xc = xf - mu
    var = jnp.mean(xc * xc, axis=-1, keepdims=True)
    return xc * lax.rsqrt(var + LN_EPS) * g + b


def _pack_halves_bf16(x):
    half = x.shape[1] // 2
    lo = lax.bitcast_convert_type(x[:, :half].astype(BF16).astype(F32), U32) >> 16
    hi = lax.bitcast_convert_type(x[:, half:].astype(BF16).astype(F32), U32) & jnp.uint32(0xFFFF0000)
    return lo | hi


def _unpack_halves_bf16(p):
    lo = lax.bitcast_convert_type(p << 16, F32).astype(BF16)
    hi = lax.bitcast_convert_type(p & jnp.uint32(0xFFFF0000), F32).astype(BF16)
    return lo, hi


def _ln_in_kernel(x_ref, g_ref, b_ref, h_ref, hb_ref):
    h = _layer_norm(x_ref[...], g_ref[...], b_ref[...])
    h_ref[...] = h
    hb_ref[...] = h.astype(BF16)


def _ln_in(x2, g, b):
    t, d = x2.shape
    tm = _pick(t, (512, 256, 128, 64, 32, 16))
    return pl.pallas_call(
        _ln_in_kernel,
        grid=(t // tm,),
        in_specs=[pl.BlockSpec((tm, d), lambda i: (i, 0)),
                  pl.BlockSpec((1, d), lambda i: (0, 0)),
                  pl.BlockSpec((1, d), lambda i: (0, 0))],
        out_specs=[pl.BlockSpec((tm, d), lambda i: (i, 0)),
                   pl.BlockSpec((tm, d), lambda i: (i, 0))],
        out_shape=[jax.ShapeDtypeStruct((t, d), F32), jax.ShapeDtypeStruct((t, d), BF16)],
        compiler_params=_params(("arbitrary",)),
        name="ln_in",
    )(x2, g.reshape(1, d), b.reshape(1, d))


def _mm_kernel(*refs, shift, w_t, row_split, n_extra, n_out, epilogue):
    n_w = 2 if shift else 1
    x_ref, w_refs = refs[0], refs[1:1 + n_w]
    extras = refs[1 + n_w:1 + n_w + n_extra]
    o_refs = refs[1 + n_w + n_extra:1 + n_w + n_extra + n_out]
    wb_ref = refs[-1]
    n_axis = 0 if w_t else 1

    @pl.when(pl.program_id(1) == 0)
    def _():
        if shift:
            tn = w_refs[0].shape[n_axis]
            w = jnp.concatenate([lax.slice_in_dim(w_refs[0][...], shift, tn, axis=n_axis),
                                 lax.slice_in_dim(w_refs[1][...], 0, shift, axis=n_axis)], axis=n_axis)
        else:
            w = w_refs[0][...]
        wb_ref[...] = w.astype(BF16)

    tm = x_ref.shape[0]
    sub = tm // row_split
    for r in range(row_split):
        rows = slice(r * sub, (r + 1) * sub)
        acc = lax.dot_general(x_ref[rows, :], wb_ref[...], (((1,), (1 if w_t else 0,)), ((), ())),
                              preferred_element_type=F32)
        outs = epilogue(acc, *[e[rows, :] if e.shape[0] == tm else e[...] for e in extras])
        for o_ref, o in zip(o_refs, outs):
            o_ref[rows, :] = o.astype(o_ref.dtype)


def _matmul(x, w, *, col0, n_cols, tm, tn, out_dtypes, epilogue=None, extras=(), out_div=None, w_t=False,
            row_split=1):
    t, k = x.shape
    k_axis, n_axis = (1, 0) if w_t else (0, 1)
    assert w.shape[k_axis] == k and n_cols % tn == 0 and t % tm == 0 and col0 + n_cols <= w.shape[n_axis]
    j0, shift = divmod(col0, tn)
    by_element = w_t and shift != 0 and col0 % SUBLANES == 0
    if by_element:
        shift = 0
    if epilogue is None:
        epilogue = lambda acc: (acc,)
    if out_div is None:
        out_div = (1,) * len(out_dtypes)
    assert tm % (row_split * 16) == 0
    kern = functools.partial(_mm_kernel, shift=shift, w_t=w_t, row_split=row_split, n_extra=len(extras),
                             n_out=len(out_dtypes), epilogue=epilogue)
    w_block = (tn, k) if w_t else (k, tn)
    w_index = lambda jj: (jj, 0) if w_t else (0, jj)
    in_specs = [pl.BlockSpec((tm, k), lambda j, i: (i, 0)),
                pl.BlockSpec(w_block, lambda j, i: w_index(j + j0))]
    if by_element:
        in_specs[1] = pl.BlockSpec((pl.Element(tn), pl.Element(k)),
                                   lambda j, i: (pl.multiple_of(col0 + j * tn, SUBLANES), 0))
    ws = [w]
    if shift:
        in_specs.append(pl.BlockSpec(w_block, lambda j, i: w_index(j + j0 + 1)))
        ws.append(w)
    in_specs += [pl.BlockSpec(bs, im) for (_, bs, im) in extras]
    return pl.pallas_call(
        kern,
        grid=(n_cols // tn, t // tm),
        in_specs=in_specs,
        out_specs=[pl.BlockSpec((tm, tn // dv), lambda j, i: (i, j)) for dv in out_div],
        out_shape=[jax.ShapeDtypeStruct((t, n_cols // dv), dt) for dt, dv in zip(out_dtypes, out_div)],
        scratch_shapes=[pltpu.VMEM(w_block, BF16)],
        compiler_params=_params(("arbitrary", "arbitrary")),
        name="matmul",
    )(x, *ws, *[a for (a, _, _) in extras])


def _gates_kernel(ba_ref, alog_ref, dt_ref, col_ref, row_ref, *, hv):
    x = ba_ref[...]
    lane = lax.broadcasted_iota(I32, x.shape, 1)
    beta = _sigmoid(x)
    xs = x + dt_ref[...]
    softplus = jnp.maximum(xs, 0.0) + jnp.log(1.0 + jnp.exp(-jnp.abs(xs)))
    g = -jnp.exp(alog_ref[...]) * softplus
    g = jnp.where((lane >= hv) & (lane < 2 * hv), g, 0.0)
    n = x.shape[0]
    row = lax.broadcasted_iota(I32, x.shape, 0) & (GDN_CHUNK - 1)
    cum = g
    rev = g
    sh = 1
    while sh < GDN_CHUNK:
        cum = cum + jnp.where(row >= sh, pltpu.roll(cum, sh, 0), 0.0)
        rev = rev + jnp.where(row < GDN_CHUNK - sh, pltpu.roll(rev, n - sh, 0), 0.0)
        sh *= 2
    e_cum = pltpu.roll(jnp.exp(cum), hv, 1)
    e_rest = pltpu.roll(jnp.exp(rev - g), 2 * hv, 1)
    out = jnp.where(lane < hv, beta, jnp.where(lane < 2 * hv, cum, jnp.where(lane < 3 * hv, e_cum, e_rest)))
    col_ref[...] = out
    row_ref[...] = out.T


def _gdn_gates(ba, a_log, dt_bias, hv):
    assert 4 * hv <= LANES
    t = ba.shape[0]
    tm = _pick(t, (512, 256, 128))
    pad = lambda v: jnp.zeros((1, LANES), F32).at[0, hv:2 * hv].set(v.astype(F32))
    return pl.pallas_call(
        functools.partial(_gates_kernel, hv=hv),
        grid=(t // tm,),
        in_specs=[pl.BlockSpec((tm, LANES), lambda i: (i, 0)),
                  pl.BlockSpec((1, LANES), lambda i: (0, 0)),
                  pl.BlockSpec((1, LANES), lambda i: (0, 0))],
        out_specs=[pl.BlockSpec((tm, LANES), lambda i: (i, 0)),
                   pl.BlockSpec((LANES, tm), lambda i: (0, i))],
        out_shape=[jax.ShapeDtypeStruct((t, LANES), F32), jax.ShapeDtypeStruct((LANES, t), F32)],
        compiler_params=_params(("arbitrary",)),
        name="gdn_gates",
    )(ba, pad(a_log), pad(dt_bias))


def _causal_conv(x, w, ext_ref):
    n = x.shape[0]
    kw = w.shape[0]
    ext_ref[SUBLANES:, :] = x
    acc = x * w[kw - 1:kw]
    for d in range(1, kw):
        acc = acc + ext_ref[SUBLANES - d:SUBLANES - d + n, :] * w[kw - 1 - d:kw - d]
    ext_ref[:SUBLANES, :] = x[n - SUBLANES:]
    return acc


def _silu(x):
    h = 0.5 * x
    return h + h * jnp.tanh(h)


def _unit_lower_inverses(lows, ii, jj):
    c = lows[0].shape[0]
    eye = jnp.where(ii == jj, 1.0, 0.0)
    pair = (ii >> 1) == (jj >> 1)
    xbs = [(eye - jnp.where(pair, low, 0.0)).astype(BF16) for low in lows]
    lbs = [low.astype(BF16) for low in lows]
    zero = jnp.zeros((c, c), BF16)
    s = 1
    while (2 << s) <= c:
        m = ((ii >> (s + 1)) == (jj >> (s + 1))) & ((ii >> s) != (jj >> s))
        ts = [jnp.dot(jnp.where(m, lb, zero), xb, preferred_element_type=F32) for lb, xb in zip(lbs, xbs)]
        xbs = [jnp.dot(xb, (eye - t).astype(BF16), preferred_element_type=F32).astype(BF16)
               for xb, t in zip(xbs, ts)]
        s += 1
    return xbs


def _gdn_kernel(q_ref, k_ref, v_ref, z_ref, cwq_ref, cwk_ref, cwv_ref, gcol_ref, grow_ref, nw_ref,
                o_ref, s_ref, tq_ref, tk_ref, tv_ref, qn_ref, kn_ref, vn_ref, u_ref, wq_ref, attn_ref, kd_ref,
                *, nq, vper, dk):
    @pl.when(pl.program_id(2) == 0)
    def _():
        s_ref[...] = jnp.zeros_like(s_ref)
        for t_ref in (tq_ref, tk_ref, tv_ref):
            t_ref[:SUBLANES, :] = jnp.zeros((SUBLANES, t_ref.shape[1]), F32)

    tb = q_ref.shape[0]
    c = GDN_CHUNK
    nc = tb // c
    nv = nq * vper
    head = lambda h: slice(h * dk, (h + 1) * dk)

    q = _silu(_causal_conv(q_ref[...].astype(F32), cwq_ref[...], tq_ref))
    k = _silu(_causal_conv(k_ref[...].astype(F32), cwk_ref[...], tk_ref))
    for hq in range(nq):
        qh = q[:, head(hq)]
        kh = k[:, head(hq)]
        qn_ref[:, head(hq)] = qh * (lax.rsqrt(jnp.sum(qh * qh, axis=-1, keepdims=True) + RMS_EPS) * (dk ** -0.5))
        kn_ref[:, head(hq)] = kh * lax.rsqrt(jnp.sum(kh * kh, axis=-1, keepdims=True) + RMS_EPS)

    cols = gcol_ref[...]
    grows = grow_ref[...]
    nw = nw_ref[...]
    ii = lax.broadcasted_iota(I32, (c, c), 0)
    jj = lax.broadcasted_iota(I32, (c, c), 1)
    incl = ii >= jj
    strict = ii > jj

    def col(ci, which, h):
        return cols[ci * c:(ci + 1) * c, which * nv + h:which * nv + h + 1]

    def state_step(ci):
        rows = slice(ci * c, (ci + 1) * c)
        ps = [ci * nv + h for h in range(nv)]
        r1 = [jnp.dot(wq_ref[p], s_ref[h].astype(BF16), preferred_element_type=F32) for h, p in enumerate(ps)]
        v_new = [(u_ref[p] - r[:c]).astype(BF16) for p, r in zip(ps, r1)]
        outs = [r[c:] + jnp.dot(attn_ref[p], vn, preferred_element_type=F32) for p, r, vn in zip(ps, r1, v_new)]
        for h, (p, vn) in enumerate(zip(ps, v_new)):
            g_tot = cols[(ci + 1) * c - 1:(ci + 1) * c, 2 * nv + h:2 * nv + h + 1]
            s_ref[h] = s_ref[h] * g_tot + lax.dot_general(
                kd_ref[p], vn, (((0,), (0,)), ((), ())), preferred_element_type=F32)
        for h, o in enumerate(outs):
            zc = z_ref[rows, head(h)].astype(F32)
            var = jnp.mean(o * o, axis=-1, keepdims=True)
            o_ref[rows, head(h)] = (o * lax.rsqrt(var + RMS_EPS) * nw * _silu(zc)).astype(BF16)

    problems = [(ci, h) for ci in range(nc) for h in range(nv)]
    gram = {}
    chunks_done = 0
    for g0 in range(0, len(problems), GDN_INV_GROUP):
        group = problems[g0:g0 + GDN_INV_GROUP]
        lows = []
        for ci, h in group:
            hq = h // vper
            rows = slice(ci * c, (ci + 1) * c)
            if (ci, hq) not in gram:
                kcb = kn_ref[rows, head(hq)].astype(BF16)
                qcb = qn_ref[rows, head(hq)].astype(BF16)
                gram[(ci, hq)] = lax.dot_general(jnp.concatenate([kcb, qcb], axis=0), kcb,
                                                 (((1,), (1,)), ((), ())), preferred_element_type=F32)
            a = gram[(ci, hq)]
            grow = grows[h:h + 1, rows]
            decay = jnp.where(incl, jnp.exp(jnp.minimum(col(ci, 1, h) - grow, 0.0)), 0.0)
            lows.append(jnp.where(strict, col(ci, 0, h) * a[:c] * decay, 0.0))
            attn_ref[ci * nv + h] = jnp.where(incl, a[c:] * decay, 0.0).astype(BF16)
        if g0 == 0:
            vn_ref[...] = _silu(_causal_conv(v_ref[...].astype(F32), cwv_ref[...], tv_ref))
        xs = _unit_lower_inverses(lows, ii, jj)
        for x, (ci, h) in zip(xs, group):
            hq = h // vper
            rows = slice(ci * c, (ci + 1) * c)
            p = ci * nv + h
            kc = kn_ref[rows, head(hq)]
            bcol = col(ci, 0, h)
            eg = col(ci, 2, h)
            rhs = jnp.concatenate([vn_ref[rows, head(h)] * bcol, kc * (bcol * eg)], axis=1).astype(BF16)
            uw = jnp.dot(x, rhs, preferred_element_type=F32)
            u_ref[p] = uw[:, :dk]
            wq_ref[p, :c, :] = uw[:, dk:].astype(BF16)
            wq_ref[p, c:, :] = (qn_ref[rows, head(hq)] * eg).astype(BF16)
            kd_ref[p] = (kc * col(ci, 3, h)).astype(BF16)
        chunks_ready = (g0 + len(group)) // nv
        for ci in range(chunks_done, chunks_ready):
            state_step(ci)
        chunks_done = chunks_ready


def _gdn(proj_a, conv_w, gcol, grow, norm_w, *, bsz, seq, hq, hv, dk):
    t = bsz * seq
    vper = hv // hq
    nq = _pick(hq, (GDN_QK_GROUP, 2, 1))
    nv = nq * vper
    ng = hq // nq
    tb = _pick(seq, (GDN_STEP, 128, 64))
    ns = seq // tb
    nprob = (tb // GDN_CHUNK) * nv
    qw = nq * dk
    vw = nv * dk
    v_blk0 = 2 * hq * dk // vw
    z_blk0 = (2 * hq * dk + hv * dk) // vw
    kw = conv_w.shape[0]
    gcol = gcol[:, :4 * hv].reshape(t, 4, ng, nv).transpose(2, 0, 1, 3).reshape(ng, t, 4 * nv)
    grow = grow[hv:2 * hv].reshape(ng, nv, t)
    rowblk = lambda b, g, s: b * ns + s
    kern = functools.partial(_gdn_kernel, nq=nq, vper=vper, dk=dk)
    return pl.pallas_call(
        kern,
        grid=(bsz, ng, ns),
        in_specs=[
            pl.BlockSpec((tb, qw), lambda b, g, s: (rowblk(b, g, s), g)),
            pl.BlockSpec((tb, qw), lambda b, g, s: (rowblk(b, g, s), ng + g)),
            pl.BlockSpec((tb, vw), lambda b, g, s: (rowblk(b, g, s), v_blk0 + g)),
            pl.BlockSpec((tb, vw), lambda b, g, s: (rowblk(b, g, s), z_blk0 + g)),
            pl.BlockSpec((kw, qw), lambda b, g, s: (0, g)),
            pl.BlockSpec((kw, qw), lambda b, g, s: (0, ng + g)),
            pl.BlockSpec((kw, vw), lambda b, g, s: (0, v_blk0 + g)),
            pl.BlockSpec((None, tb, 4 * nv), lambda b, g, s: (g, rowblk(b, g, s), 0)),
            pl.BlockSpec((None, nv, tb), lambda b, g, s: (g, 0, rowblk(b, g, s))),
            pl.BlockSpec((1, dk), lambda b, g, s: (0, 0)),
        ],
        out_specs=pl.BlockSpec((tb, vw), lambda b, g, s: (rowblk(b, g, s), g)),
        out_shape=jax.ShapeDtypeStruct((t, hv * dk), BF16),
        scratch_shapes=[pltpu.VMEM((nv, dk, dk), F32),
                        pltpu.VMEM((tb + SUBLANES, qw), F32),
                        pltpu.VMEM((tb + SUBLANES, qw), F32),
                        pltpu.VMEM((tb + SUBLANES, vw), F32),
                        pltpu.VMEM((tb, qw), F32),
                        pltpu.VMEM((tb, qw), F32),
                        pltpu.VMEM((tb, vw), F32),
                        pltpu.VMEM((nprob, GDN_CHUNK, dk), F32),
                        pltpu.VMEM((nprob, 2 * GDN_CHUNK, dk), BF16),
                        pltpu.VMEM((nprob, GDN_CHUNK, GDN_CHUNK), BF16),
                        pltpu.VMEM((nprob, GDN_CHUNK, dk), BF16)],
        compiler_params=_params(("arbitrary", "arbitrary", "arbitrary")),
        name="gdn",
    )(proj_a, proj_a, proj_a, proj_a, conv_w, conv_w, conv_w, gcol, grow, norm_w.reshape(1, dk))


def _sc_kernel(b_ref, c_ref, x_ref, w_ref, u_ref, tail_ref):
    @pl.when(pl.program_id(1) == 0)
    def _():
        tail_ref[:SUBLANES, :] = jnp.zeros((SUBLANES, tail_ref.shape[1]), F32)

    p = c_ref[...].astype(F32) * x_ref[...].astype(F32)
    u_ref[...] = (b_ref[...].astype(F32) * _causal_conv(p, w_ref[...], tail_ref)).astype(BF16)


def _short_conv(proj_b, conv_w, *, bsz, seq, width):
    t = bsz * seq
    ts = _pick(seq, (512, 256, 128, 64))
    ns = seq // ts
    kw = conv_w.shape[0]
    return pl.pallas_call(
        _sc_kernel,
        grid=(bsz, ns),
        in_specs=[pl.BlockSpec((ts, width), lambda b, s: (b * ns + s, 0)),
                  pl.BlockSpec((ts, width), lambda b, s: (b * ns + s, 1)),
                  pl.BlockSpec((ts, width), lambda b, s: (b * ns + s, 2)),
                  pl.BlockSpec((kw, width), lambda b, s: (0, 0))],
        out_specs=pl.BlockSpec((ts, width), lambda b, s: (b * ns + s, 0)),
        out_shape=jax.ShapeDtypeStruct((t, width), BF16),
        scratch_shapes=[pltpu.VMEM((ts + SUBLANES, width), F32)],
        compiler_params=_params(("arbitrary", "arbitrary")),
        name="short_conv",
    )(proj_b, proj_b, proj_b, conv_w)


def _router_kernel(h_ref, w_ref, b_ref, idx_ref, gate_ref, rank_ref, cnt_ref, run_ref):
    @pl.when(pl.program_id(0) == 0)
    def _():
        run_ref[...] = jnp.zeros_like(run_ref)

    tm = h_ref.shape[0]
    h = h_ref[...]
    h_hi = h.astype(BF16)
    h_lo = (h - h_hi.astype(F32)).astype(BF16)
    w2 = w_ref[...]
    first = jnp.dot(h_hi, w2, preferred_element_type=F32)
    logits = (first[:, :LANES] + first[:, LANES:]
              + jnp.dot(h_lo, w2[:, :LANES], preferred_element_type=F32) + b_ref[...])
    lane = lax.broadcasted_iota(I32, logits.shape, 1)
    lane_f = lane.astype(F32)
    cur = logits
    idxs, vals = [], []
    for _ in range(TOP_K):
        m = jnp.max(cur, axis=-1, keepdims=True)
        ix = jnp.min(jnp.where(cur == m, lane_f, float(LANES)), axis=-1, keepdims=True).astype(I32)
        idxs.append(ix)
        vals.append(m)
        cur = jnp.where(lane == ix, -jnp.inf, cur)
    es = [jnp.exp(val - vals[0]) for val in vals]
    den = es[0]
    for e in es[1:]:
        den = den + e
    onehot = jnp.zeros(logits.shape, F32)
    for ix in idxs:
        onehot = onehot + jnp.where(lane == ix, 1.0, 0.0)
    ri = lax.broadcasted_iota(I32, (tm, tm), 0)
    ci = lax.broadcasted_iota(I32, (tm, tm), 1)
    tri = jnp.where(ri > ci, 1.0, 0.0).astype(BF16)
    before = jnp.dot(tri, onehot.astype(BF16), preferred_element_type=F32) + run_ref[...]
    idx_out = jnp.zeros(logits.shape, I32)
    rank_out = jnp.zeros(logits.shape, I32)
    gate_out = jnp.zeros(logits.shape, F32)
    for kk in range(TOP_K):
        rank = jnp.sum(jnp.where(lane == idxs[kk], before, 0.0), axis=-1, keepdims=True).astype(I32)
        idx_out = jnp.where(lane == kk, idxs[kk], idx_out)
        rank_out = jnp.where(lane == kk, rank, rank_out)
        gate_out = jnp.where(lane == kk, es[kk] / den, gate_out)
    idx_ref[...] = idx_out
    rank_ref[...] = rank_out
    gate_ref[...] = gate_out
    run_ref[...] = run_ref[...] + jnp.sum(onehot, axis=0, keepdims=True)
    cnt_ref[...] = run_ref[...].astype(I32)


def _router(h, w_router, b_router):
    t, d = h.shape
    e = w_router.shape[1]
    tm = _pick(t, (256, 128, 64, 32, 16, 8))
    wp = jnp.zeros((d, LANES), F32).at[:, :e].set(w_router)
    w_hi = wp.astype(BF16)
    w_lo = (wp - w_hi.astype(F32)).astype(BF16)
    w2 = jnp.concatenate([w_hi, w_lo], axis=1)
    bp = jnp.full((1, LANES), -1e30, F32).at[0, :e].set(b_router)
    tile = pl.BlockSpec((tm, LANES), lambda i: (i, 0))
    return pl.pallas_call(
        _router_kernel,
        grid=(t // tm,),
        in_specs=[pl.BlockSpec((tm, d), lambda i: (i, 0)),
                  pl.BlockSpec((d, 2 * LANES), lambda i: (0, 0)),
                  pl.BlockSpec((1, LANES), lambda i: (0, 0))],
        out_specs=[tile, tile, tile, pl.BlockSpec((1, LANES), lambda i: (0, 0))],
        out_shape=[jax.ShapeDtypeStruct((t, LANES), I32), jax.ShapeDtypeStruct((t, LANES), F32),
                   jax.ShapeDtypeStruct((t, LANES), I32), jax.ShapeDtypeStruct((1, LANES), I32)],
        scratch_shapes=[pltpu.VMEM((1, LANES), F32)],
        compiler_params=_params(("arbitrary",)),
        name="router",
    )(h, w2, bp)


def _row_copy(src, s_row, dst, d_row, sem):
    return pltpu.make_async_copy(src.at[pl.ds(s_row, 1)], dst.at[pl.ds(d_row, 1)], sem)


def _dispatch_kernel(pos_ref, fill_ref, x_ref, rows_ref, sem, zsem, zero_ref):
    tm = x_ref.shape[0]
    base = pl.program_id(0) * (tm * TOP_K)

    @pl.when(pl.program_id(0) == 0)
    def _():
        zb = zero_ref.shape[0]
        n_exp = fill_ref.shape[0] - 1
        zero_ref[...] = jnp.zeros_like(zero_ref)

        def zero_copy(row):
            return pltpu.make_async_copy(zero_ref, rows_ref.at[pl.ds(pl.multiple_of(row, zb), zb)], zsem)

        def each_block(fn):
            def per_expert(e, carry):
                @pl.when(fill_ref[e] >= 0)
                def _():
                    fn(zero_copy(fill_ref[e]))
                return carry
            lax.fori_loop(0, n_exp, per_expert, 0)

            def per_tail(b, carry):
                fn(zero_copy(b * zb))
                return carry
            lax.fori_loop(fill_ref[n_exp] // zb, rows_ref.shape[0] // zb, per_tail, 0)

        each_block(lambda cp: cp.start())
        each_block(lambda cp: cp.wait())

    for t in range(tm):
        for kk in range(TOP_K):
            _row_copy(x_ref, t, rows_ref, pos_ref[base + (t * TOP_K + kk)], sem).start(priority=kk & 1)
    n = tm * TOP_K
    pltpu.make_async_copy(rows_ref.at[pl.ds(0, n)], rows_ref.at[pl.ds(0, n)], sem).wait()


def _dispatch(h, pos, fills, n_rows, zb):
    t, d = h.shape
    tm = _pick(t, (128, 64, 32, 16, 8))
    assert n_rows % zb == 0
    return pl.pallas_call(
        _dispatch_kernel,
        grid_spec=pltpu.PrefetchScalarGridSpec(
            num_scalar_prefetch=2,
            grid=(t // tm,),
            in_specs=[pl.BlockSpec((tm, d), lambda i, pos, fills: (i, 0))],
            out_specs=pl.BlockSpec(memory_space=pl.ANY),
            scratch_shapes=[pltpu.SemaphoreType.DMA(()), pltpu.SemaphoreType.DMA(()),
                            pltpu.VMEM((zb, d), h.dtype)],
        ),
        out_shape=jax.ShapeDtypeStruct((n_rows, d), h.dtype),
        compiler_params=_params(("arbitrary",)),
        name="dispatch",
    )(pos, fills, h)


def _swiglu_interleaved(gu):
    lane = lax.broadcasted_iota(I32, (gu.shape[0], LANES), 1)
    even = (lane & 1) == 0
    parts = []
    for c in range(gu.shape[1] // LANES):
        g = gu[:, c * LANES:(c + 1) * LANES]
        gt = jnp.minimum(g, SWIGLU_LIMIT)
        glu = gt * _sigmoid(SWIGLU_ALPHA * gt)
        up = jnp.clip(g, -SWIGLU_LIMIT, SWIGLU_LIMIT) + 1.0
        parts.append(jnp.where(even, pltpu.roll(up, LANES - 1, 1) * glu, 0.0))
    outs = [parts[2 * m] + pltpu.roll(parts[2 * m + 1], 1, 1) for m in range(len(parts) // 2)]
    return jnp.concatenate(outs, axis=1) if len(outs) > 1 else outs[0]


def _interleave_rows_bf16(w_ref, out_ref):
    half = LANES // 2
    for m in range(w_ref.shape[0] // LANES):
        lo = w_ref[m * LANES:m * LANES + half, :].astype(BF16).astype(F32)
        hi = w_ref[m * LANES + half:(m + 1) * LANES, :].astype(BF16).astype(F32)
        packed = (lax.bitcast_convert_type(lo, jnp.uint32) >> 16) | (
            lax.bitcast_convert_type(hi, jnp.uint32) & jnp.uint32(0xFFFF0000))
        out_ref[m * LANES:(m + 1) * LANES, :] = pltpu.bitcast(packed, BF16)


def _expert_kernel(ie_ref, ist_ref, inb_ref, ihalf_ref, rows_ref, wgu_ref, bgu_ref, wd_ref, bd_ref, y_ref,
                   xb_ref, acc_ref, stage_ref, wgub_ref, wdb_ref, ost_ref, sem_in, sem_out, *, tm, n_j):
    del ie_ref
    wi = pl.program_id(0)
    j = pl.program_id(1)
    n_items = pl.num_programs(0)
    nblk = inb_ref[wi]
    half_last = ihalf_ref[wi]
    nfull = nblk - half_last
    half = stage_ref.shape[1]

    def block(b):
        return pl.ds(pl.multiple_of(b * tm, tm), tm)

    def hbm_block(item, b):
        return pl.ds(pl.multiple_of(ist_ref[item] + b * tm, tm), tm)

    def rows_copy(item, b):
        return pltpu.make_async_copy(rows_ref.at[hbm_block(item, b)], stage_ref.at[block(b)], sem_in)

    def y_copy(item, b):
        slot = b & 1
        return pltpu.make_async_copy(ost_ref.at[slot], y_ref.at[hbm_block(item, b)], sem_out.at[slot])

    def drain_stores(item):
        n = inb_ref[item]
        for back in (2, 1):
            @pl.when(n >= back)
            def _():
                y_copy(item, n - back).wait()

    def for_blocks(item, fn):
        def body(b, carry):
            fn(item, b)
            return carry
        lax.fori_loop(0, inb_ref[item], body, 0)

    def unpack(item, b):
        lo, hi = _unpack_halves_bf16(stage_ref[block(b), :])
        xb_ref[block(b), :half] = lo
        xb_ref[block(b), half:] = hi

    @pl.when(j == 0)
    def _():
        @pl.when(wi == 0)
        def _():
            for_blocks(0, lambda it, b: rows_copy(it, b).start())
            acc_ref[...] = jnp.zeros_like(acc_ref)
            ost_ref[...] = jnp.zeros_like(ost_ref)

        @pl.when(wi > 0)
        def _():
            drain_stores(wi - 1)

        for_blocks(wi, lambda it, b: rows_copy(it, b).wait())
        for_blocks(wi, unpack)

        @pl.when(wi + 1 < n_items)
        def _():
            for_blocks(wi + 1, lambda it, b: rows_copy(it, b).start())

    @pl.when(nblk > 0)
    def _():
        bias = bgu_ref[...]
        n_ct, _, ct = wgub_ref.shape

        def convert_weights():
            for c in range(n_ct):
                wgub_ref[c] = wgu_ref[:, c * ct:(c + 1) * ct].astype(BF16)
            _interleave_rows_bf16(wd_ref, wdb_ref)

        def mlp(b, m):
            x = xb_ref[pl.ds(pl.multiple_of(b * tm, tm), m), :]
            gu = jnp.concatenate([jnp.dot(x, wgub_ref[c], preferred_element_type=F32) for c in range(n_ct)],
                                 axis=1) + bias
            act = _swiglu_interleaved(gu).astype(BF16)
            return jnp.dot(act, wdb_ref[...], preferred_element_type=F32)

        def accumulate(b, m=tm):
            rows = pl.ds(pl.multiple_of(b * tm, tm), m)
            prev = jnp.where(j == 0, jnp.broadcast_to(bd_ref[...], (m, acc_ref.shape[1])), acc_ref[rows, :])
            acc_ref[rows, :] = prev + mlp(b, m)

        def finish(b, m=tm):
            prev = acc_ref[pl.ds(pl.multiple_of(b * tm, tm), m), :] if n_j > 1 else bd_ref[...]
            ost_ref[b & 1, 0:m, :] = _pack_halves_bf16(prev + mlp(b, m))
            y_copy(wi, b).start()

        def run(fn, stores):
            @pl.when(nfull >= 2)
            def _():
                convert_weights()
                fn(0)
                fn(1)

            @pl.when(nfull == 1)
            def _():
                convert_weights()
                fn(0)

            def pair(i, carry):
                if stores:
                    y_copy(wi, 2 * i - 2).wait()
                    y_copy(wi, 2 * i - 1).wait()
                fn(2 * i)
                fn(2 * i + 1)
                return carry
            lax.fori_loop(1, nfull // 2, pair, 0)

            @pl.when(((nfull & 1) == 1) & (nfull >= 3))
            def _():
                if stores:
                    y_copy(wi, nfull - 3).wait()
                fn(nfull - 1)

            @pl.when((half_last == 1) & (nfull == 0))
            def _():
                convert_weights()
                fn(0, tm // 2)

            @pl.when((half_last == 1) & (nfull > 0))
            def _():
                if stores:
                    @pl.when(nfull >= 2)
                    def _():
                        y_copy(wi, nfull - 2).wait()
                fn(nfull, tm // 2)

        if n_j > 1:
            @pl.when(j < n_j - 1)
            def _():
                run(accumulate, False)

        @pl.when(j == n_j - 1)
        def _():
            run(finish, True)

    @pl.when((wi == n_items - 1) & (j == n_j - 1))
    def _():
        drain_stores(wi)
        ost_ref[0] = jnp.zeros(ost_ref.shape[1:], U32)

        def fill(b, carry):
            cp = pltpu.make_async_copy(ost_ref.at[0], y_ref.at[pl.ds(pl.multiple_of(b * tm, tm), tm)],
                                       sem_out.at[0])
            cp.start()
            cp.wait()
            return carry
        lax.fori_loop(ist_ref[n_items] // tm, y_ref.shape[0] // tm, fill, 0)


def _experts(rows, item_e, item_start, item_nblk, item_half, w_gate_up, b_gate_up, w_down, b_down, *, tm, xmax):
    n_rows = rows.shape[0]
    n_exp, d, f2 = w_gate_up.shape
    assert rows.shape[1] * 2 == d
    f = f2 // 2
    tf = _pick(f, (MOE_TF, LANES))
    n_j = f // tf
    ct = min(2 * tf, MXU_N)
    n_items = item_e.shape[0]

    def jeff(j, inb, wi):
        return jnp.where(inb[wi] > 0, j, n_j - 1)

    kern = functools.partial(_expert_kernel, tm=tm, n_j=n_j)
    return pl.pallas_call(
        kern,
        grid_spec=pltpu.PrefetchScalarGridSpec(
            num_scalar_prefetch=4,
            grid=(n_items, n_j),
            in_specs=[
                pl.BlockSpec(memory_space=pl.ANY),
                pl.BlockSpec((None, d, 2 * tf), lambda wi, j, ie, ist, inb, ih: (ie[wi], 0, jeff(j, inb, wi))),
                pl.BlockSpec((None, 1, 2 * tf), lambda wi, j, ie, ist, inb, ih: (ie[wi], 0, jeff(j, inb, wi))),
                pl.BlockSpec((None, tf, d), lambda wi, j, ie, ist, inb, ih: (ie[wi], jeff(j, inb, wi), 0)),
                pl.BlockSpec((None, 1, d), lambda wi, j, ie, ist, inb, ih: (ie[wi], 0, 0)),
            ],
            out_specs=pl.BlockSpec(memory_space=pl.ANY),
            scratch_shapes=[pltpu.VMEM((xmax, d), BF16),
                            pltpu.VMEM((xmax, d), F32),
                            pltpu.VMEM((xmax, d // 2), U32),
                            pltpu.VMEM((2 * tf // ct, d, ct), BF16),
                            pltpu.VMEM((tf, d), BF16),
                            pltpu.VMEM((2, tm, d // 2), U32),
                            pltpu.SemaphoreType.DMA(()),
                            pltpu.SemaphoreType.DMA((2,))],
        ),
        out_shape=jax.ShapeDtypeStruct((n_rows, d // 2), U32),
        compiler_params=_params(("arbitrary", "arbitrary")),
        name="experts",
    )(item_e, item_start, item_nblk, item_half, rows, w_gate_up, b_gate_up.reshape(n_exp, 1, f2),
      w_down, b_down.reshape(n_exp, 1, d))


def _combine_kernel(pos_ref, y_ref, gate_ref, h_ref, g_ref, b_ref, o_ref, ybuf0_ref, ybuf1_ref, sem,
                    *, alpha):
    tm = h_ref.shape[0]
    i = pl.program_id(0)
    n_tiles = pl.num_programs(0)
    n = tm * TOP_K
    bufs = (ybuf0_ref, ybuf1_ref)

    def gather(tile, s):
        base = tile * n
        for t in range(tm):
            for kk in range(TOP_K):
                _row_copy(y_ref, pos_ref[base + (t * TOP_K + kk)], bufs[s], kk * tm + t,
                          sem.at[s]).start(priority=kk & 1)

    def wait_buf(s):
        pltpu.make_async_copy(y_ref.at[pl.ds(0, n)], bufs[s], sem.at[s]).wait()

    def step(s):
        wait_buf(s)
        gather(jnp.minimum(i + 1, n_tiles - 1), 1 - s)
        gate = gate_ref[...]
        half = y_ref.shape[1]
        ffn_lo = jnp.zeros((tm, half), F32)
        ffn_hi = jnp.zeros((tm, half), F32)
        for kk in range(TOP_K):
            p = bufs[s][kk * tm:(kk + 1) * tm, :]
            gk = gate[:, kk:kk + 1]
            ffn_lo = ffn_lo + gk * lax.bitcast_convert_type(p << 16, F32)
            ffn_hi = ffn_hi + gk * lax.bitcast_convert_type(p & jnp.uint32(0xFFFF0000), F32)
        ffn = jnp.concatenate([ffn_lo, ffn_hi], axis=1)
        o_ref[...] = _layer_norm(alpha * h_ref[...] + ffn, g_ref[...], b_ref[...])

        @pl.when(i == n_tiles - 1)
        def _():
            wait_buf(1 - s)

    @pl.when(i == 0)
    def _():
        gather(0, 0)

    for s in range(2):
        @pl.when((i & 1) == s)
        def _():
            step(s)


def _combine(y, pos, gate, h, ln_g, ln_b, alpha):
    t, d = h.shape
    tm = _pick(t, (128, 64, 32, 16, 8))
    assert y.shape[1] * 2 == d
    return pl.pallas_call(
        functools.partial(_combine_kernel, alpha=alpha),
        grid_spec=pltpu.PrefetchScalarGridSpec(
            num_scalar_prefetch=1,
            grid=(t // tm,),
            in_specs=[pl.BlockSpec(memory_space=pl.ANY),
                      pl.BlockSpec((tm, LANES), lambda i, pos: (i, 0)),
                      pl.BlockSpec((tm, d), lambda i, pos: (i, 0)),
                      pl.BlockSpec((1, d), lambda i, pos: (0, 0)),
                      pl.BlockSpec((1, d), lambda i, pos: (0, 0))],
            out_specs=pl.BlockSpec((tm, d), lambda i, pos: (i, 0)),
            scratch_shapes=[pltpu.VMEM((TOP_K * tm, d // 2), U32), pltpu.VMEM((TOP_K * tm, d // 2), U32),
                            pltpu.SemaphoreType.DMA((2,))],
        ),
        out_shape=jax.ShapeDtypeStruct((t, d), F32),
        compiler_params=_params(("arbitrary",)),
        name="combine",
    )(pos, y, gate, h, ln_g.reshape(1, d), ln_b.reshape(1, d))


def _moe_plan(idx, rank, counts, *, tm, xmax, n_items):
    n_exp = counts.shape[0]
    padded = (counts + tm - 1) // tm * tm
    pstart = jnp.cumsum(padded) - padded
    onehot = idx[..., None] == jnp.arange(n_exp, dtype=idx.dtype)
    pos = (jnp.sum(jnp.where(onehot, pstart, 0), axis=-1) + rank).reshape(-1).astype(I32)
    per_e = (padded + xmax - 1) // xmax
    cum = jnp.cumsum(per_e)
    total = cum[-1]
    wi = jnp.arange(n_items, dtype=I32)
    valid = wi < total
    e_w = jnp.minimum(jnp.searchsorted(cum, jnp.minimum(wi, total - 1), side="right"), n_exp - 1).astype(I32)
    local = jnp.minimum(wi, total - 1) - (cum - per_e)[e_w]
    start = (pstart[e_w] + local * xmax).astype(I32)
    nblk = jnp.where(valid, jnp.clip(padded[e_w] - local * xmax, 0, xmax) // tm, 0).astype(I32)
    real = jnp.clip(counts[e_w] - local * xmax, 0, xmax)
    half_last = ((nblk > 0) & (real - (nblk - 1) * tm <= tm // 2)).astype(I32)
    start = jnp.concatenate([start, jnp.sum(padded, keepdims=True).astype(I32)])
    fills = jnp.concatenate([jnp.where(padded > 0, pstart + padded - tm, -1),
                             jnp.sum(padded, keepdims=True)]).astype(I32)
    return pos, e_w, start, nblk, half_last, fills


def _layer(h0, h0b, p, *, bsz, seq, alpha):
    t, d = h0.shape
    w_in = p["w_in"]
    hv = p["gdn_a_log"].shape[0]
    dk = p["gdn_norm_w"].shape[0]
    qkv_dim = p["gdn_conv_w"].shape[1]
    v_dim = hv * dk
    hq = (qkv_dim - v_dim) // (2 * dk)
    scw = p["sc_conv_w"].shape[1]
    assert 2 * hv <= LANES and hv % hq == 0
    n_a = qkv_dim + v_dim
    off_ba = n_a
    off_b = n_a + 2 * hv
    n_b = 3 * scw + 2 * d
    assert w_in.shape[1] == off_b + n_b

    w_in_t = w_in.T
    tm = _pick(t, (1024, 512, 256, 128))
    tn_a = _pick(n_a, (1024, 512, 256, 128))
    (proj_a,) = _matmul(h0b, w_in_t, col0=0, n_cols=n_a, tm=tm, tn=tn_a, out_dtypes=(BF16,), w_t=True)
    tn_b = _pick(n_b, (1024, 512, 256, 128))
    (proj_b,) = _matmul(h0b, w_in_t, col0=off_b, n_cols=n_b, tm=tm, tn=tn_b, out_dtypes=(BF16,), w_t=True)
    assert off_ba + LANES <= w_in.shape[1]
    (ba,) = _matmul(h0b, w_in_t, col0=off_ba, n_cols=LANES, tm=tm, tn=LANES, out_dtypes=(F32,), w_t=True)

    gcol, grow = _gdn_gates(ba, p["gdn_a_log"], p["gdn_dt_bias"], hv)
    o_n = _gdn(proj_a, p["gdn_conv_w"], gcol, grow, p["gdn_norm_w"], bsz=bsz, seq=seq, hq=hq, hv=hv, dk=dk)

    u = _short_conv(proj_b, p["sc_conv_w"], bsz=bsz, seq=seq, width=scw)

    tn = _pick(d, (512, 256, 128))
    ga0 = 3 * scw // tn
    gb0 = (3 * scw + d) // tn
    split = 2 if tm % 32 == 0 else 1
    (part_a,) = _matmul(
        o_n, p["w_out_gdn"], col0=0, n_cols=d, tm=tm, tn=tn, out_dtypes=(F32,), row_split=split,
        epilogue=lambda acc, ga: (_sigmoid(ga.astype(F32)) * acc,),
        extras=[(proj_b, (tm, tn), lambda j, i: (i, ga0 + j))])
    (merged,) = _matmul(
        u, p["w_out_sc"], col0=0, n_cols=d, tm=tm, tn=tn, out_dtypes=(BF16,), row_split=split,
        epilogue=lambda acc, gb, pa: (pa + _sigmoid(gb.astype(F32)) * acc,),
        extras=[(proj_b, (tm, tn), lambda j, i: (i, gb0 + j)),
                (part_a, (tm, tn), lambda j, i: (i, j))])
    tm3 = _pick(t, (512, 256, 128))

    def mix_epilogue(acc, hh, g, b):
        hn = _layer_norm(alpha * hh + acc, g, b)
        return hn, _pack_halves_bf16(hn)

    h1, h1p = _matmul(
        merged, p["w_out"].astype(BF16), col0=0, n_cols=d, tm=tm3, tn=d, out_dtypes=(F32, U32), out_div=(1, 2),
        epilogue=mix_epilogue, row_split=2,
        extras=[(h0, (tm3, d), lambda j, i: (i, 0)),
                (p["ln_mix_g"].reshape(1, d), (1, d), lambda j, i: (0, 0)),
                (p["ln_mix_b"].reshape(1, d), (1, d), lambda j, i: (0, 0))])

    n_exp = p["w_router"].shape[1]
    idx, gate, rank, cnt = _router(h1, p["w_router"], p["b_router"])
    mtm = MOE_TM
    xmax = MOE_XMAX
    n_rows = t * TOP_K + n_exp * mtm
    n_items = n_exp + n_rows // xmax
    pos, item_e, item_start, item_nblk, item_half, fills = _moe_plan(
        idx[:, :TOP_K], rank[:, :TOP_K], cnt[0, :n_exp], tm=mtm, xmax=xmax, n_items=n_items)
    rows = _dispatch(h1p, pos, fills, n_rows, mtm)
    y = _experts(rows, item_e, item_start, item_nblk, item_half, p["w_gate_up"], p["b_gate_up"], p["w_down"],
                 p["b_down"], tm=mtm, xmax=xmax)
    h2 = _combine(y, pos, gate, h1, p["ln_ffn_g"], p["ln_ffn_b"], alpha)
    return h2


_LAYER_PARAMS = ("w_in", "gdn_conv_w", "gdn_a_log", "gdn_dt_bias", "gdn_norm_w", "w_out_gdn", "sc_conv_w",
                 "w_out_sc", "w_out", "ln_mix_g", "ln_mix_b", "w_router", "b_router", "w_gate_up", "b_gate_up",
                 "w_down", "b_down", "ln_ffn_g", "ln_ffn_b")


def kernel(x, ln_in_g, ln_in_b, w_in, gdn_conv_w, gdn_a_log, gdn_dt_bias, gdn_norm_w, w_out_gdn, sc_conv_w,
           w_out_sc, w_out, ln_mix_g, ln_mix_b, w_router, b_router, w_gate_up, b_gate_up, w_down, b_down,
           ln_ffn_g, ln_ffn_b):
    stacked = dict(zip(_LAYER_PARAMS, (w_in, gdn_conv_w, gdn_a_log, gdn_dt_bias, gdn_norm_w, w_out_gdn,
                                       sc_conv_w, w_out_sc, w_out, ln_mix_g, ln_mix_b, w_router, b_router,
                                       w_gate_up, b_gate_up, w_down, b_down, ln_ffn_g, ln_ffn_b)))
    bsz, seq, d = x.shape
    depth = w_in.shape[0]
    alpha = (2 * depth) ** 0.25
    h, hb = _ln_in(x.reshape(bsz * seq, d), ln_in_g, ln_in_b)
    for l in range(depth):
        p = {name: arr[l] for name, arr in stacked.items()}
        h = _layer(h, hb, p, bsz=bsz, seq=seq, alpha=alpha)
        if l + 1 < depth:
            hb = h.astype(BF16)
    return h.reshape(bsz, seq, d)
```
